```python
import jax, jax.numpy as jnp
from jax import lax
import numpy as np

D_MODEL = 1024
BATCH = 16
SEQ = 2048
DEPTH = 1

MEM_LEN = 256
HEAD_DIM = 64
SB_HEADS = 8
RWKV_HEADS = 8
SB_WIDTH = SB_HEADS * HEAD_DIM
RWKV_WIDTH = RWKV_HEADS * HEAD_DIM
MIX_WIDTH = SB_WIDTH + RWKV_WIDTH
DECAY_LORA = 64
AAA_LORA = 64
GATE_LORA = 160
RWKV_IN = 3 * RWKV_WIDTH + DECAY_LORA + AAA_LORA + GATE_LORA
MIX_IN = 3 * SB_WIDTH + RWKV_IN
MEM_HEADS = 4
MEM_HEAD_DIM = 128
MEM_WIDTH = MEM_HEADS * MEM_HEAD_DIM
D_FF = 2816
SB_BLOCK = 128
NORM_EPS = 1e-6
LNX_EPS = 64e-5
RWKV_SPLITS = [RWKV_WIDTH, 2 * RWKV_WIDTH, 3 * RWKV_WIDTH,
               3 * RWKV_WIDTH + DECAY_LORA, 3 * RWKV_WIDTH + DECAY_LORA + AAA_LORA]

kernel_name = 'sb_rwkv7_macaron_sandwich_hybrid'


def rms_norm(x, g):
    xf = x.astype(jnp.float32)
    y = xf * lax.rsqrt(jnp.mean(xf * xf, axis=-1, keepdims=True) + NORM_EPS)
    return (y * g.astype(jnp.float32)).astype(x.dtype)


def swiglu(h, w_in, w_out):
    gate, up = jnp.split(h @ w_in, 2, axis=-1)
    return (jax.nn.silu(gate) * up) @ w_out


def token_shift(p):
    return jnp.pad(p, ((0, 0), (1, 0), (0, 0)))[:, :-1]


def stick_breaking_attention(q, k, v):
    S = q.shape[1]
    scale = q.shape[-1] ** -0.5
    outs = []
    for blk in range(S // SB_BLOCK):
        q0 = blk * SB_BLOCK
        q1 = q0 + SB_BLOCK
        z = jnp.einsum('bthd,bshd->bhts', q[:, q0:q1], k[:, :q1]).astype(jnp.float32) * scale
        causal = jnp.arange(q1)[None, :] < (q0 + jnp.arange(SB_BLOCK))[:, None]
        log_one_minus = jnp.where(causal, jax.nn.log_sigmoid(-z), 0.0)
        log_tail = lax.cumsum(log_one_minus, axis=3, reverse=True) - log_one_minus
        a = jnp.where(causal, jnp.exp(jax.nn.log_sigmoid(z) + log_tail), 0.0)
        outs.append(jnp.einsum('bhts,bshd->bthd', a.astype(v.dtype), v[:, :q1]))
    return jnp.concatenate(outs, axis=1)


def rwkv7_time_mix(p, mu, w0, w2, a0, a2, g2, k_k, k_a, r_k, lnx_g, lnx_b):
    B, S, _ = p.shape
    H, N = RWKV_HEADS, HEAD_DIM
    p = p + (token_shift(p) - p) * mu
    r, k, v, xw, xa, xg = jnp.split(p, RWKV_SPLITS, axis=-1)
    w = -jax.nn.softplus(-(w0 + jnp.tanh(xw) @ w2)) - 0.5
    decay = jnp.exp(-jnp.exp(w.astype(jnp.float32)))
    a = jax.nn.sigmoid(a0 + xa @ a2)
    g = jax.nn.sigmoid(xg) @ g2
    heads = lambda t: t.reshape(B, S, H, N)
    kk = heads(k * k_k).astype(jnp.float32)
    kk = kk / jnp.maximum(jnp.linalg.norm(kk, axis=-1, keepdims=True), 1e-12)
    k = k * (1.0 + (a - 1.0) * k_a)
    rh, kh, vh, ah = heads(r), heads(k), heads(v), heads(a)
    seq_first = lambda t: jnp.moveaxis(t.astype(jnp.float32), 1, 0)
    xs = (seq_first(rh), seq_first(heads(decay)), seq_first(kh), seq_first(vh),
          seq_first(-kk), seq_first(kk * ah))

    def step(state, inp):
        r_t, w_t, k_t, v_t, a_t, b_t = inp
        sa = jnp.einsum('bhvk,bhk->bhv', state, a_t)
        state = (state * w_t[:, :, None, :] + sa[..., :, None] * b_t[..., None, :]
                 + v_t[..., :, None] * k_t[..., None, :])
        return state, jnp.einsum('bhvk,bhk->bhv', state, r_t)

    state0 = jnp.zeros((B, H, N, N), jnp.float32)
    _, y = lax.scan(step, state0, xs)
    y = jnp.moveaxis(y, 0, 1)
    mean = jnp.mean(y, axis=-1, keepdims=True)
    var = jnp.mean(jnp.square(y - mean), axis=-1, keepdims=True)
    y = (y - mean) * lax.rsqrt(var + LNX_EPS)
    y = y * lnx_g.astype(jnp.float32).reshape(H, N) + lnx_b.astype(jnp.float32).reshape(H, N)
    bonus = jnp.sum((rh * kh * r_k).astype(jnp.float32), axis=-1, keepdims=True) * vh.astype(jnp.float32)
    return (y + bonus).reshape(B, S, RWKV_WIDTH).astype(p.dtype) * g


def memory_cross_attention(h, mem_n, w_q, w_kv, w_o):
    B, S, _ = h.shape
    M = mem_n.shape[1]
    q = (h @ w_q).reshape(B, S, MEM_HEADS, MEM_HEAD_DIM)
    km, vm = jnp.split(mem_n @ w_kv, 2, axis=-1)
    km = km.reshape(B, M, MEM_HEADS, MEM_HEAD_DIM)
    vm = vm.reshape(B, M, MEM_HEADS, MEM_HEAD_DIM)
    s = jnp.einsum('bthd,bmhd->bhtm', q, km).astype(jnp.float32) * (MEM_HEAD_DIM ** -0.5)
    pr = jax.nn.softmax(s, axis=-1)
    o = jnp.einsum('bhtm,bmhd->bthd', pr.astype(vm.dtype), vm).reshape(B, S, MEM_WIDTH)
    return o @ w_o


def setup_inputs(seed: int = 0) -> dict:
    key = jax.random.key(seed)
    ks = jax.random.split(key, 32)
    L = DEPTH

    def nrm(k, shape, scale):
        return jax.random.normal(k, shape, jnp.float32) * scale

    def gain(k, n):
        return 1.0 + nrm(k, (L, n), 0.02)

    pos = jnp.arange(RWKV_WIDTH, dtype=jnp.float32) / (RWKV_WIDTH - 1)
    w0_base = -6.5 + 5.0 * pos ** 0.85
    return {
        'x': nrm(ks[0], (BATCH, SEQ, D_MODEL), 1.0),
        'mem': nrm(ks[1], (BATCH, MEM_LEN, D_MODEL), 1.0),
        'ffn1_pre': gain(ks[2], D_MODEL),
        'ffn1_post': gain(ks[3], D_MODEL),
        'ffn1_w_in': nrm(ks[4], (L, D_MODEL, 2 * D_FF), D_MODEL ** -0.5),
        'ffn1_w_out': nrm(ks[5], (L, D_FF, D_MODEL), D_FF ** -0.5),
        'mix_pre': gain(ks[6], D_MODEL),
        'mix_post': gain(ks[7], D_MODEL),
        'mix_w_in': nrm(ks[8], (L, D_MODEL, MIX_IN), D_MODEL ** -0.5),
        'rwkv_mu': jax.random.uniform(ks[9], (L, RWKV_IN), jnp.float32),
        'rwkv_w0': w0_base[None, :] + nrm(ks[10], (L, RWKV_WIDTH), 0.01),
        'rwkv_w2': nrm(ks[11], (L, DECAY_LORA, RWKV_WIDTH), 0.1 * DECAY_LORA ** -0.5),
        'rwkv_a0': nrm(ks[12], (L, RWKV_WIDTH), 0.1),
        'rwkv_a2': nrm(ks[13], (L, AAA_LORA, RWKV_WIDTH), 0.1 * AAA_LORA ** -0.5),
        'rwkv_g2': nrm(ks[14], (L, GATE_LORA, RWKV_WIDTH), GATE_LORA ** -0.5),
        'rwkv_k_k': 0.85 + nrm(ks[15], (L, RWKV_WIDTH), 0.02),
        'rwkv_k_a': 1.0 + nrm(ks[16], (L, RWKV_WIDTH), 0.02),
        'rwkv_r_k': nrm(ks[17], (L, RWKV_HEADS, HEAD_DIM), 0.1),
        'rwkv_lnx_g': gain(ks[18], RWKV_WIDTH),
        'rwkv_lnx_b': nrm(ks[19], (L, RWKV_WIDTH), 0.02),
        'sb_out_g': gain(ks[20], SB_WIDTH),
        'mix_w_out': nrm(ks[21], (L, MIX_WIDTH, D_MODEL), MIX_WIDTH ** -0.5),
        'mem_pre': gain(ks[22], D_MODEL),
        'mem_post': gain(ks[23], D_MODEL),
        'mem_kv_g': gain(ks[24], D_MODEL),
        'mem_w_q': nrm(ks[25], (L, D_MODEL, MEM_WIDTH), D_MODEL ** -0.5),
        'mem_w_kv': nrm(ks[26], (L, D_MODEL, 2 * MEM_WIDTH), D_MODEL ** -0.5),
        'mem_w_o': nrm(ks[27], (L, MEM_WIDTH, D_MODEL), MEM_WIDTH ** -0.5),
        'ffn2_pre': gain(ks[28], D_MODEL),
        'ffn2_post': gain(ks[29], D_MODEL),
        'ffn2_w_in': nrm(ks[30], (L, D_MODEL, 2 * D_FF), D_MODEL ** -0.5),
        'ffn2_w_out': nrm(ks[31], (L, D_FF, D_MODEL), D_FF ** -0.5),
    }


def reference(x, mem, ffn1_pre, ffn1_post, ffn1_w_in, ffn1_w_out, mix_pre, mix_post, mix_w_in,
              rwkv_mu, rwkv_w0, rwkv_w2, rwkv_a0, rwkv_a2, rwkv_g2, rwkv_k_k, rwkv_k_a, rwkv_r_k,
              rwkv_lnx_g, rwkv_lnx_b, sb_out_g, mix_w_out, mem_pre, mem_post, mem_kv_g,
              mem_w_q, mem_w_kv, mem_w_o, ffn2_pre, ffn2_post, ffn2_w_in, ffn2_w_out):
    B, S, _ = x.shape
    h = x
    for l in range(DEPTH):
        h = h + 0.5 * rms_norm(swiglu(rms_norm(h, ffn1_pre[l]), ffn1_w_in[l], ffn1_w_out[l]), ffn1_post[l])

        u = rms_norm(h, mix_pre[l]) @ mix_w_in[l]
        sb_part, rw_part = jnp.split(u, [3 * SB_WIDTH], axis=-1)
        q, k, v = [t.reshape(B, S, SB_HEADS, HEAD_DIM) for t in jnp.split(sb_part, 3, axis=-1)]
        sb_o = stick_breaking_attention(q, k, v)
        sb_o = rms_norm(sb_o, sb_out_g[l].reshape(SB_HEADS, HEAD_DIM)).reshape(B, S, SB_WIDTH)
        rw_o = rwkv7_time_mix(rw_part, rwkv_mu[l], rwkv_w0[l], rwkv_w2[l], rwkv_a0[l], rwkv_a2[l],
                              rwkv_g2[l], rwkv_k_k[l], rwkv_k_a[l], rwkv_r_k[l],
                              rwkv_lnx_g[l], rwkv_lnx_b[l])
        mixed = jnp.concatenate([sb_o, rw_o], axis=-1) @ mix_w_out[l]
        h = h + rms_norm(mixed, mix_post[l])

        mem_n = rms_norm(mem, mem_kv_g[l])
        m_o = memory_cross_attention(rms_norm(h, mem_pre[l]), mem_n, mem_w_q[l], mem_w_kv[l], mem_w_o[l])
        h = h + rms_norm(m_o, mem_post[l])

        h = h + 0.5 * rms_norm(swiglu(rms_norm(h, ffn2_pre[l]), ffn2_w_in[l], ffn2_w_out[l]), ffn2_post[l])
    return h
```

```python
import functools

import jax
import jax.numpy as jnp
from jax import lax
from jax.experimental import pallas as pl
from jax.experimental.pallas import tpu as pltpu

D_MODEL = 1024
HEAD_DIM = 64
SB_HEADS = 8
RWKV_HEADS = 8
SB_WIDTH = SB_HEADS * HEAD_DIM
RWKV_WIDTH = RWKV_HEADS * HEAD_DIM
MIX_WIDTH = SB_WIDTH + RWKV_WIDTH
DECAY_LORA = 64
AAA_LORA = 64
GATE_LORA = 160
RWKV_IN = 3 * RWKV_WIDTH + DECAY_LORA + AAA_LORA + GATE_LORA
MEM_HEADS = 4
MEM_HEAD_DIM = 128
MEM_WIDTH = MEM_HEADS * MEM_HEAD_DIM
D_FF = 2816
NORM_EPS = 1e-6
LNX_EPS = 64e-5

LANES = 128
MXU_DIM = 256
V7X_VMEM_BYTES = 64 * 1024 * 1024
VMEM_LIMIT_BYTES = V7X_VMEM_BYTES - 8 * 1024 * 1024

RWKV_PAD = 15 * LANES
GATE_PAD = RWKV_PAD - (3 * RWKV_WIDTH + LANES)
SB_TILE = 128
RWKV_CHUNK = 64
RWKV_ROWS = 256
TOKEN_TILE = 512
FF_CHUNK = 256

BF16 = jnp.bfloat16
F32 = jnp.float32


def _const_spec(shape):
    nd = len(shape)
    return pl.BlockSpec(shape, lambda *_: (0,) * nd, pipeline_mode=pl.Buffered(1))


def _params(*sem):
    return pltpu.CompilerParams(dimension_semantics=sem, vmem_limit_bytes=VMEM_LIMIT_BYTES)


def _rms(x, g):
    ms = jnp.mean(x * x, axis=-1, keepdims=True)
    return x * lax.rsqrt(ms + NORM_EPS) * g


def _dot(a, b):
    return jnp.dot(a, b, preferred_element_type=F32)


def _dot_nt(a, b):
    return lax.dot_general(a, b, (((1,), (1,)), ((), ())), preferred_element_type=F32)


def _dot_tn(a, b):
    return lax.dot_general(a, b, (((0,), (0,)), ((), ())), preferred_element_type=F32)


def _split2(x):
    hi = x.astype(BF16)
    lo = (x - hi.astype(F32)).astype(BF16)
    return hi, lo


def _split3(x):
    hi = x.astype(BF16)
    r1 = x - hi.astype(F32)
    mid = r1.astype(BF16)
    lo = (r1 - mid.astype(F32)).astype(BF16)
    return hi, mid, lo


def _ffn_kernel(x_ref, pre_ref, post_ref, win_ref, wout_ref, o_ref, act_ref):
    x = x_ref[...]
    xn = _rms(x, pre_ref[...]).astype(BF16)
    for c in range(D_FF // FF_CHUNK):
        lo = c * FF_CHUNK
        gate = _dot(xn, win_ref[:, lo:lo + FF_CHUNK])
        up = _dot(xn, win_ref[:, D_FF + lo:D_FF + lo + FF_CHUNK])
        act_ref[:, lo:lo + FF_CHUNK] = (gate * jax.nn.sigmoid(gate) * up).astype(BF16)
    y = _dot(act_ref[...], wout_ref[...])
    o_ref[...] = x + 0.5 * _rms(y, post_ref[...])


def _ffn(h, pre, post, w_in, w_out):
    t = h.shape[0]
    tm = min(TOKEN_TILE, t)
    row = lambda i: (i, 0)
    return pl.pallas_call(
        _ffn_kernel,
        grid=(t // tm,),
        in_specs=[
            pl.BlockSpec((tm, D_MODEL), row),
            _const_spec((1, D_MODEL)),
            _const_spec((1, D_MODEL)),
            _const_spec((D_MODEL, 2 * D_FF)),
            _const_spec((D_FF, D_MODEL)),
        ],
        out_specs=pl.BlockSpec((tm, D_MODEL), row),
        out_shape=jax.ShapeDtypeStruct((t, D_MODEL), F32),
        scratch_shapes=[pltpu.VMEM((tm, D_FF), BF16)],
        compiler_params=_params("parallel"),
        name="ffn",
    )(h, pre, post, w_in, w_out)


def _mix_in_kernel(h_ref, pre_ref, w_ref, qkv_ref, rw_ref):
    xn = _rms(h_ref[...], pre_ref[...]).astype(BF16)
    qkv_ref[...] = _dot(xn, w_ref[:, :3 * SB_WIDTH]).astype(BF16)
    rw_ref[...] = _dot(xn, w_ref[:, 3 * SB_WIDTH:])


def _mix_in(h, pre, w):
    t = h.shape[0]
    tm = min(TOKEN_TILE, t)
    row = lambda i: (i, 0)
    return pl.pallas_call(
        _mix_in_kernel,
        grid=(t // tm,),
        in_specs=[
            pl.BlockSpec((tm, D_MODEL), row),
            _const_spec((1, D_MODEL)),
            _const_spec((D_MODEL, 3 * SB_WIDTH + RWKV_PAD)),
        ],
        out_specs=[pl.BlockSpec((tm, 3 * SB_WIDTH), row), pl.BlockSpec((tm, RWKV_PAD), row)],
        out_shape=[jax.ShapeDtypeStruct((t, 3 * SB_WIDTH), BF16),
                   jax.ShapeDtypeStruct((t, RWKV_PAD), F32)],
        compiler_params=_params("parallel"),
        name="mix_in",
    )(h, pre, w)


def _sb_kernel(q_ref, k_ref, v_ref, g_ref, cum_ref, o_ref, k2_ref, v2_ref, *, seq):
    i = pl.program_id(2)
    nk = seq // SB_TILE
    tq = SB_TILE
    lane = lax.broadcasted_iota(jnp.int32, (SB_TILE, LANES), 1)
    head0 = lane < HEAD_DIM

    @pl.when(i == 0)
    def _():
        for j in range(nk):
            kb = k_ref[j * SB_TILE:(j + 1) * SB_TILE, :]
            vb = v_ref[j * SB_TILE:(j + 1) * SB_TILE, :]
            zero = jnp.zeros_like(kb)
            k2_ref[j, :SB_TILE, :] = jnp.where(head0, kb, zero)
            k2_ref[j, SB_TILE:, :] = jnp.where(head0, zero, kb)
            v2_ref[j, :SB_TILE, :] = jnp.where(head0, vb, zero)
            v2_ref[j, SB_TILE:, :] = jnp.where(head0, zero, vb)

    q = q_ref[...] * jnp.asarray(HEAD_DIM ** -0.5, BF16)
    cum = cum_ref[...]

    def tile(j, carry, acc, causal):
        z = _dot_nt(q, k2_ref[j])
        sp = jnp.maximum(z, 0.0) + jnp.log1p(jnp.exp(-jnp.abs(z)))
        l = -sp
        if causal is not None:
            l = jnp.where(causal, l, 0.0)
        hi, lo = _split2(l)
        r = _dot(jnp.concatenate([hi, lo], axis=0), cum)
        r = r[:tq] + r[tq:]
        tail = r[:, :2 * SB_TILE] + carry
        a = jnp.exp(z - sp + tail)
        if causal is not None:
            a = jnp.where(causal, a, 0.0)
        acc = acc + _dot(a.astype(BF16), v2_ref[j])
        return carry + r[:, 2 * SB_TILE:], acc

    row = lax.broadcasted_iota(jnp.int32, (tq, 2 * SB_TILE), 0)
    col = lax.broadcasted_iota(jnp.int32, (tq, 2 * SB_TILE), 1) & (SB_TILE - 1)
    carry = jnp.zeros((tq, 2 * SB_TILE), F32)
    acc = jnp.zeros((tq, LANES), F32)
    carry, acc = tile(i, carry, acc, col < row)

    def body(jj, state):
        return tile(i - 1 - jj, state[0], state[1], None)

    carry, acc = lax.fori_loop(0, i, body, (carry, acc))

    sq = acc * acc
    ms0 = jnp.sum(jnp.where(head0, sq, 0.0), axis=-1, keepdims=True)
    ms1 = jnp.sum(jnp.where(head0, 0.0, sq), axis=-1, keepdims=True)
    ms = jnp.where(head0, ms0, ms1) * (1.0 / HEAD_DIM)
    o_ref[...] = (acc * lax.rsqrt(ms + NORM_EPS) * g_ref[...]).astype(BF16)


def _sb_cum_matrix():
    n = 2 * SB_TILE
    j = jnp.arange(n)[:, None]
    s = jnp.arange(n)[None, :]
    same = (j // SB_TILE) == (s // SB_TILE)
    tri = same & (j > s)
    return jnp.concatenate([tri, same], axis=1).astype(BF16)


def _sb_attention(qkv, out_g, batch, seq):
    qkv = qkv.reshape(batch, seq, 3 * SB_WIDTH)
    pairs = SB_WIDTH // LANES
    nq = seq // SB_TILE
    kern = functools.partial(_sb_kernel, seq=seq)
    return pl.pallas_call(
        kern,
        grid=(batch, pairs, nq),
        in_specs=[
            pl.BlockSpec((None, SB_TILE, LANES), lambda b, p, i: (b, i, p)),
            pl.BlockSpec((None, seq, LANES), lambda b, p, i: (b, 0, pairs + p)),
            pl.BlockSpec((None, seq, LANES), lambda b, p, i: (b, 0, 2 * pairs + p)),
            pl.BlockSpec((1, LANES), lambda b, p, i: (0, p)),
            _const_spec((2 * SB_TILE, 4 * SB_TILE)),
        ],
        out_specs=pl.BlockSpec((None, SB_TILE, LANES), lambda b, p, i: (b, i, p)),
        out_shape=jax.ShapeDtypeStruct((batch, seq, SB_WIDTH), BF16),
        scratch_shapes=[pltpu.VMEM((seq // SB_TILE, 2 * SB_TILE, LANES), BF16),
                        pltpu.VMEM((seq // SB_TILE, 2 * SB_TILE, LANES), BF16)],
        compiler_params=_params("parallel", "parallel", "arbitrary"),
        name="sb_attention",
    )(qkv, qkv, qkv, out_g, _sb_cum_matrix())


def _rwkv_kernel(rw_ref, prev_ref, mu_ref, w0_ref, w2_ref, a0_ref, a2_ref, g2_ref, kk_ref, ka_ref,
                 rk_ref, lg_ref, lb_ref, cum_ref, grp_ref, o_ref,
                 at_s, rt_s, bt_s, kt_s, v_s, bh_s, kh_s, gam_s, y_s, state_s):
    c = pl.program_id(1)
    rows = RWKV_ROWS
    ch = RWKV_CHUNK
    width = RWKV_WIDTH
    pairs = width // LANES

    @pl.when(c == 0)
    def _():
        state_s[...] = jnp.zeros_like(state_s)

    def gsum(x):
        hi, lo = _split2(x)
        r = _dot(jnp.concatenate([hi, lo], axis=0), grp_ref[...])
        return r[:rows] + r[rows:]

    p = rw_ref[...]
    prev_last = jnp.where(c == 0, 0.0, prev_ref[7:8, :])
    rowi = lax.broadcasted_iota(jnp.int32, p.shape, 0)
    shifted = jnp.where(rowi == 0, prev_last, pltpu.roll(p, 1, axis=0))
    p = p + (shifted - p) * mu_ref[...]
    r = p[:, :width]
    k = p[:, width:2 * width]
    v = p[:, 2 * width:3 * width]
    xwa = p[:, 3 * width:3 * width + LANES]
    xg = p[:, 3 * width + LANES:]
    lane = lax.broadcasted_iota(jnp.int32, xwa.shape, 1)
    xw = jnp.where(lane < DECAY_LORA, jnp.tanh(xwa), 0.0).astype(BF16)
    xa = jnp.where(lane < DECAY_LORA, 0.0, xwa).astype(BF16)
    wlin = w0_ref[...] + _dot(xw, w2_ref[...])
    wlog = -jax.nn.softplus(-wlin) - 0.5
    lw = -jnp.exp(wlog)
    lr = jax.nn.sigmoid(a0_ref[...] + _dot(xa, a2_ref[...]))
    gate = _dot(jax.nn.sigmoid(xg).astype(BF16), g2_ref[...])
    kk = k * kk_ref[...]
    kk = kk / jnp.maximum(jnp.sqrt(gsum(kk * kk)), 1e-12)
    k = k * (1.0 + (lr - 1.0) * ka_ref[...])
    a_vec = -kk
    b_vec = kk * lr

    cs = _dot(cum_ref[...], jnp.concatenate(_split3(lw), axis=1))
    cs = cs[:, :width] + cs[:, width:2 * width] + cs[:, 2 * width:]
    g_in = cs[:rows]
    g_tot = cs[rows:]
    e_in = jnp.exp(g_in)
    e_inv = jnp.exp(-g_in)
    e_end = jnp.exp(g_tot - g_in)
    pieces = (
        (at_s, a_vec * jnp.exp(g_in - lw)),
        (rt_s, r * e_in),
        (bt_s, b_vec * e_inv),
        (kt_s, k * e_inv),
        (v_s, v),
        (bh_s, b_vec * e_end),
        (kh_s, k * e_end),
    )
    for ref, val in pieces:
        val = val.astype(BF16)
        for pr in range(pairs):
            ref[pr] = val[:, pr * LANES:(pr + 1) * LANES]
    gam = jnp.exp(g_tot)
    for pr in range(pairs):
        gam_s[pr] = gam[:, pr * LANES:(pr + 1) * LANES]

    t_i = lax.broadcasted_iota(jnp.int32, (ch, LANES), 0)
    i_i = lax.broadcasted_iota(jnp.int32, (ch, LANES), 1) & (HEAD_DIM - 1)
    strict = i_i < t_i
    incl = i_i <= t_i
    eye = (i_i == t_i).astype(F32)
    h0 = lax.broadcasted_iota(jnp.int32, (ch, LANES), 1) < HEAD_DIM
    rr = lax.broadcasted_iota(jnp.int32, (LANES, LANES), 0) // HEAD_DIM
    cc = lax.broadcasted_iota(jnp.int32, (LANES, LANES), 1) // HEAD_DIM
    same_head = rr == cc

    def bd(y):
        y = y.astype(BF16)
        zero = jnp.zeros_like(y)
        return jnp.concatenate([jnp.where(h0, y, zero), jnp.where(h0, zero, y)], axis=0)

    def pp(x, y):
        return _dot(x.astype(BF16), bd(y))

    def chunk(ci, _):
        c0 = pl.multiple_of(ci * ch, ch)
        for pr in range(pairs):
            sl = (pr, pl.ds(c0, ch), slice(None))
            at, rt, bt, kt = at_s[sl], rt_s[sl], bt_s[sl], kt_s[sl]
            vv, bh, kh = v_s[sl], bh_s[sl], kh_s[sl]
            aa = _dot_nt(jnp.concatenate([at, rt], axis=0),
                         jnp.concatenate([bd(bt), bd(kt)], axis=0))
            n_ab = jnp.where(strict, aa[:ch, :LANES], 0.0)
            n_ak = jnp.where(strict, aa[:ch, LANES:], 0.0)
            a_rb = jnp.where(incl, aa[ch:, :LANES], 0.0)
            a_rk = jnp.where(incl, aa[ch:, LANES:], 0.0)

            blk = 8
            nd = jnp.where((t_i // blk) == (i_i // blk), n_ab, 0.0)
            n2 = pp(nd, nd)
            n4 = pp(n2, n2)
            tm = eye + nd
            tm = tm + pp(tm, n2)
            tm = tm + pp(tm, n4)
            while blk < ch:
                off = ((t_i // (2 * blk)) == (i_i // (2 * blk))) & ((t_i // blk) != (i_i // blk))
                tm = tm + pp(pp(tm, jnp.where(off, n_ab, 0.0)), tm)
                blk *= 2

            akv = pp(n_ak, vv)
            wa = pp(tm, at)
            uv = pp(tm, akv)
            rhat = rt.astype(F32) + pp(a_rb, wa)
            yv = pp(a_rb, uv) + pp(a_rk, vv)
            s0 = state_s[pr]
            s0b = s0.astype(BF16)
            y_s[sl] = _dot_nt(rhat.astype(BF16), s0b) + yv
            gc = jnp.where(same_head, _dot_tn(wa.astype(BF16), bh), 0.0)
            sv = _dot_tn(jnp.concatenate([uv.astype(BF16), vv], axis=0),
                         jnp.concatenate([bh, kh], axis=0))
            gam_row = gam_s[pr, pl.ds(c0, 1), :]
            state_s[pr] = s0 * gam_row + _dot(s0b, gc.astype(BF16)) + jnp.where(same_head, sv, 0.0)
        return 0

    lax.fori_loop(0, rows // ch, chunk, 0)

    y = jnp.concatenate([y_s[pr] for pr in range(pairs)], axis=1)
    mean = gsum(y) * (1.0 / HEAD_DIM)
    d = y - mean
    var = gsum(d * d) * (1.0 / HEAD_DIM)
    y = d * lax.rsqrt(var + LNX_EPS) * lg_ref[...] + lb_ref[...]
    bonus = gsum(r * k * rk_ref[...]) * v
    o_ref[...] = ((y + bonus) * gate).astype(BF16)


def _rwkv_cum_matrix():
    t = jnp.arange(RWKV_ROWS)[:, None]
    i = jnp.arange(RWKV_ROWS)[None, :]
    same = (t // RWKV_CHUNK) == (i // RWKV_CHUNK)
    return jnp.concatenate([same & (i <= t), same], axis=0).astype(BF16)


def _group_matrix():
    a = jnp.arange(RWKV_WIDTH)
    return (a[:, None] // HEAD_DIM == a[None, :] // HEAD_DIM).astype(BF16)


def _rwkv(rw, batch, seq, mu, w0, w2, a0, a2, g2, k_k, k_a, r_k, lnx_g, lnx_b):
    rw = rw.reshape(batch, seq, RWKV_PAD)
    rows = RWKV_ROWS
    pairs = RWKV_WIDTH // LANES
    vec = lambda n: _const_spec((1, n))
    packed = lambda dt: pltpu.VMEM((pairs, rows, LANES), dt)
    return pl.pallas_call(
        _rwkv_kernel,
        grid=(batch, seq // rows),
        in_specs=[
            pl.BlockSpec((None, rows, RWKV_PAD), lambda b, c: (b, c, 0)),
            pl.BlockSpec((None, 8, RWKV_PAD), lambda b, c: (b, jnp.maximum(c * (rows // 8) - 1, 0), 0)),
            vec(RWKV_PAD), vec(RWKV_WIDTH), _const_spec((LANES, RWKV_WIDTH)), vec(RWKV_WIDTH),
            _const_spec((LANES, RWKV_WIDTH)), _const_spec((GATE_PAD, RWKV_WIDTH)),
            vec(RWKV_WIDTH), vec(RWKV_WIDTH), vec(RWKV_WIDTH), vec(RWKV_WIDTH), vec(RWKV_WIDTH),
            _const_spec((2 * rows, rows)), _const_spec((RWKV_WIDTH, RWKV_WIDTH)),
        ],
        out_specs=pl.BlockSpec((None, rows, RWKV_WIDTH), lambda b, c: (b, c, 0)),
        out_shape=jax.ShapeDtypeStruct((batch, seq, RWKV_WIDTH), BF16),
        scratch_shapes=[packed(BF16)] * 7 + [packed(F32), packed(F32),
                                             pltpu.VMEM((pairs, LANES, LANES), F32)],
        compiler_params=_params("parallel", "arbitrary"),
        name="rwkv7",
    )(rw, rw, mu, w0, w2, a0, a2, g2, k_k, k_a, r_k, lnx_g, lnx_b,
      _rwkv_cum_matrix(), _group_matrix())


def _mix_out_kernel(sb_ref, rw_ref, h_ref, w_ref, post_ref, o_ref):
    mixed = _dot(sb_ref[...], w_ref[:SB_WIDTH, :]) + _dot(rw_ref[...], w_ref[SB_WIDTH:, :])
    o_ref[...] = h_ref[...] + _rms(mixed, post_ref[...])


def _mix_out(sb_o, rw_o, h, w, post):
    t = h.shape[0]
    tm = min(TOKEN_TILE, t)
    row = lambda i: (i, 0)
    return pl.pallas_call(
        _mix_out_kernel,
        grid=(t // tm,),
        in_specs=[
            pl.BlockSpec((tm, SB_WIDTH), row),
            pl.BlockSpec((tm, RWKV_WIDTH), row),
            pl.BlockSpec((tm, D_MODEL), row),
            _const_spec((MIX_WIDTH, D_MODEL)),
            _const_spec((1, D_MODEL)),
        ],
        out_specs=pl.BlockSpec((tm, D_MODEL), row),
        out_shape=jax.ShapeDtypeStruct((t, D_MODEL), F32),
        compiler_params=_params("parallel"),
        name="mix_out",
    )(sb_o, rw_o, h, w, post)


def _mem_kv_kernel(m_ref, g_ref, w_ref, o_ref):
    o_ref[...] = _dot(_rms(m_ref[...], g_ref[...]).astype(BF16), w_ref[...]).astype(BF16)


def _mem_kv(mem, g, w):
    t = mem.shape[0]
    tm = min(TOKEN_TILE, t)
    row = lambda i: (i, 0)
    return pl.pallas_call(
        _mem_kv_kernel,
        grid=(t // tm,),
        in_specs=[pl.BlockSpec((tm, D_MODEL), row), _const_spec((1, D_MODEL)),
                  _const_spec((D_MODEL, 2 * MEM_WIDTH))],
        out_specs=pl.BlockSpec((tm, 2 * MEM_WIDTH), row),
        out_shape=jax.ShapeDtypeStruct((t, 2 * MEM_WIDTH), BF16),
        compiler_params=_params("parallel"),
        name="mem_kv",
    )(mem, g, w)


def _mem_attn_kernel(h_ref, kv_ref, pre_ref, wq_ref, wo_ref, post_ref, o_ref):
    h = h_ref[...]
    q = _dot(_rms(h, pre_ref[...]).astype(BF16), wq_ref[...])
    q = (q * (MEM_HEAD_DIM ** -0.5)).astype(BF16)
    outs = []
    for hd in range(MEM_HEADS):
        lo = hd * MEM_HEAD_DIM
        s = _dot_nt(q[:, lo:lo + MEM_HEAD_DIM], kv_ref[:, lo:lo + MEM_HEAD_DIM])
        e = jnp.exp(s - jnp.max(s, axis=-1, keepdims=True))
        pr = e / jnp.sum(e, axis=-1, keepdims=True)
        outs.append(_dot(pr.astype(BF16), kv_ref[:, MEM_WIDTH + lo:MEM_WIDTH + lo + MEM_HEAD_DIM]))
    o = jnp.concatenate(outs, axis=1).astype(BF16)
    o_ref[...] = h + _rms(_dot(o, wo_ref[...]), post_ref[...])


def _mem_attn(h, kv, pre, w_q, w_o, post, batch, seq):
    mem_len = kv.shape[0] // batch
    kv = kv.reshape(batch, mem_len, 2 * MEM_WIDTH)
    h = h.reshape(batch, seq, D_MODEL)
    tm = min(TOKEN_TILE, seq)
    out = pl.pallas_call(
        _mem_attn_kernel,
        grid=(batch, seq // tm),
        in_specs=[
            pl.BlockSpec((None, tm, D_MODEL), lambda b, i: (b, i, 0)),
            pl.BlockSpec((None, mem_len, 2 * MEM_WIDTH), lambda b, i: (b, 0, 0)),
            _const_spec((1, D_MODEL)),
            _const_spec((D_MODEL, MEM_WIDTH)),
            _const_spec((MEM_WIDTH, D_MODEL)),
            _const_spec((1, D_MODEL)),
        ],
        out_specs=pl.BlockSpec((None, tm, D_MODEL), lambda b, i: (b, i, 0)),
        out_shape=jax.ShapeDtypeStruct((batch, seq, D_MODEL), F32),
        compiler_params=_params("parallel", "parallel"),
        name="mem_attn",
    )(h, kv, pre, w_q, w_o, post)
    return out.reshape(batch * seq, D_MODEL)


def _row(a):
    return a.reshape(1, -1).astype(F32)


def _pad_rows(w, n):
    return jnp.pad(w, ((0, n - w.shape[0]), (0, 0)))


def kernel(x, mem, ffn1_pre, ffn1_post, ffn1_w_in, ffn1_w_out, mix_pre, mix_post, mix_w_in, rwkv_mu, rwkv_w0, rwkv_w2, rwkv_a0, rwkv_a2, rwkv_g2, rwkv_k_k, rwkv_k_a, rwkv_r_k, rwkv_lnx_g, rwkv_lnx_b, sb_out_g, mix_w_out, mem_pre, mem_post, mem_kv_g, mem_w_q, mem_w_kv, mem_w_o, ffn2_pre, ffn2_post, ffn2_w_in, ffn2_w_out):
    batch, seq, _ = x.shape
    depth = ffn1_pre.shape[0]
    h = x.reshape(batch * seq, D_MODEL)
    mem2 = mem.reshape(-1, D_MODEL)
    pad_cols = RWKV_PAD - RWKV_IN
    for l in range(depth):
        h = _ffn(h, _row(ffn1_pre[l]), _row(ffn1_post[l]),
                 ffn1_w_in[l].astype(BF16), ffn1_w_out[l].astype(BF16))

        w_in = jnp.pad(mix_w_in[l], ((0, 0), (0, pad_cols))).astype(BF16)
        qkv, rw = _mix_in(h, _row(mix_pre[l]), w_in)
        sb_o = _sb_attention(qkv, _row(sb_out_g[l]), batch, seq)
        w2 = _pad_rows(rwkv_w2[l], LANES).astype(BF16)
        a2 = jnp.pad(rwkv_a2[l], ((DECAY_LORA, 0), (0, 0))).astype(BF16)
        g2 = _pad_rows(rwkv_g2[l], GATE_PAD).astype(BF16)
        mu = jnp.pad(rwkv_mu[l], (0, pad_cols))
        rw_o = _rwkv(rw, batch, seq, _row(mu), _row(rwkv_w0[l]), w2, _row(rwkv_a0[l]), a2, g2,
                     _row(rwkv_k_k[l]), _row(rwkv_k_a[l]), _row(rwkv_r_k[l]),
                     _row(rwkv_lnx_g[l]), _row(rwkv_lnx_b[l]))
        h = _mix_out(sb_o.reshape(batch * seq, SB_WIDTH), rw_o.reshape(batch * seq, RWKV_WIDTH), h,
                     mix_w_out[l].astype(BF16), _row(mix_post[l]))

        kv = _mem_kv(mem2, _row(mem_kv_g[l]), mem_w_kv[l].astype(BF16))
        h = _mem_attn(h, kv, _row(mem_pre[l]), mem_w_q[l].astype(BF16), mem_w_o[l].astype(BF16),
                      _row(mem_post[l]), batch, seq)

        h = _ffn(h, _row(ffn2_pre[l]), _row(ffn2_post[l]),
                 ffn2_w_in[l].astype(BF16), ffn2_w_out[l].astype(BF16))
    return h.reshape(batch, seq, D_MODEL)
```

```python
import functools

import jax
import jax.numpy as jnp
from jax import lax
from jax.experimental import pallas as pl
from jax.experimental.pallas import tpu as pltpu

D_MODEL = 1024
HEAD_DIM = 64
SB_HEADS = 8
RWKV_HEADS = 8
SB_WIDTH = SB_HEADS * HEAD_DIM
RWKV_WIDTH = RWKV_HEADS * HEAD_DIM
MIX_WIDTH = SB_WIDTH + RWKV_WIDTH
DECAY_LORA = 64
AAA_LORA = 64
GATE_LORA = 160
RWKV_IN = 3 * RWKV_WIDTH + DECAY_LORA + AAA_LORA + GATE_LORA
MEM_HEADS = 4
MEM_HEAD_DIM = 128
MEM_WIDTH = MEM_HEADS * MEM_HEAD_DIM
D_FF = 2816
NORM_EPS = 1e-6
LNX_EPS = 64e-5

LANES = 128
MXU_DIM = 256
V7X_VMEM_BYTES = 64 * 1024 * 1024
VMEM_LIMIT_BYTES = V7X_VMEM_BYTES - 8 * 1024 * 1024

RWKV_PAD = 15 * LANES
GATE_PAD = RWKV_PAD - (3 * RWKV_WIDTH + LANES)
SB_TILE = 128
SB_QROWS = 512
RWKV_CHUNK = 64
RWKV_ROWS = 256
TOKEN_TILE = 512
FF_CHUNK = 256

BF16 = jnp.bfloat16
F32 = jnp.float32


def _const_spec(shape):
    nd = len(shape)
    return pl.BlockSpec(shape, lambda *_: (0,) * nd, pipeline_mode=pl.Buffered(1))


def _params(*sem):
    return pltpu.CompilerParams(dimension_semantics=sem, vmem_limit_bytes=VMEM_LIMIT_BYTES)


def _rms(x, g):
    ms = jnp.mean(x * x, axis=-1, keepdims=True)
    return x * lax.rsqrt(ms + NORM_EPS) * g


def _dot(a, b):
    return jnp.dot(a, b, preferred_element_type=F32)


def _dot_nt(a, b):
    return lax.dot_general(a, b, (((1,), (1,)), ((), ())), preferred_element_type=F32)


def _dot_tn(a, b):
    return lax.dot_general(a, b, (((0,), (0,)), ((), ())), preferred_element_type=F32)


def _split2(x):
    hi = x.astype(BF16)
    lo = (x - hi.astype(F32)).astype(BF16)
    return hi, lo


def _split3(x):
    hi = x.astype(BF16)
    r1 = x - hi.astype(F32)
    mid = r1.astype(BF16)
    lo = (r1 - mid.astype(F32)).astype(BF16)
    return hi, mid, lo


def _ffn_kernel(x_ref, pre_ref, post_ref, win_ref, wout_ref, o_ref, act_ref):
    x = x_ref[...]
    xn = _rms(x, pre_ref[...]).astype(BF16)
    for c in range(D_FF // FF_CHUNK):
        lo = c * FF_CHUNK
        gate = _dot(xn, win_ref[:, lo:lo + FF_CHUNK])
        up = _dot(xn, win_ref[:, D_FF + lo:D_FF + lo + FF_CHUNK])
        act_ref[:, lo:lo + FF_CHUNK] = (gate * jax.nn.sigmoid(gate) * up).astype(BF16)
    y = _dot(act_ref[...], wout_ref[...])
    o_ref[...] = x + 0.5 * _rms(y, post_ref[...])


def _ffn(h, pre, post, w_in, w_out):
    t = h.shape[0]
    tm = min(TOKEN_TILE, t)
    row = lambda i: (i, 0)
    return pl.pallas_call(
        _ffn_kernel,
        grid=(t // tm,),
        in_specs=[
            pl.BlockSpec((tm, D_MODEL), row),
            _const_spec((1, D_MODEL)),
            _const_spec((1, D_MODEL)),
            _const_spec((D_MODEL, 2 * D_FF)),
            _const_spec((D_FF, D_MODEL)),
        ],
        out_specs=pl.BlockSpec((tm, D_MODEL), row),
        out_shape=jax.ShapeDtypeStruct((t, D_MODEL), F32),
        scratch_shapes=[pltpu.VMEM((tm, D_FF), BF16)],
        compiler_params=_params("parallel"),
        name="ffn",
    )(h, pre, post, w_in, w_out)


def _mix_in_kernel(h_ref, pre_ref, w_ref, qkv_ref, rw_ref):
    xn = _rms(h_ref[...], pre_ref[...]).astype(BF16)
    qkv_ref[...] = _dot(xn, w_ref[:, :3 * SB_WIDTH]).astype(BF16)
    rw_ref[...] = _dot(xn, w_ref[:, 3 * SB_WIDTH:])


def _mix_in(h, pre, w):
    t = h.shape[0]
    tm = min(TOKEN_TILE, t)
    row = lambda i: (i, 0)
    return pl.pallas_call(
        _mix_in_kernel,
        grid=(t // tm,),
        in_specs=[
            pl.BlockSpec((tm, D_MODEL), row),
            _const_spec((1, D_MODEL)),
            _const_spec((D_MODEL, 3 * SB_WIDTH + RWKV_PAD)),
        ],
        out_specs=[pl.BlockSpec((tm, 3 * SB_WIDTH), row), pl.BlockSpec((tm, RWKV_PAD), row)],
        out_shape=[jax.ShapeDtypeStruct((t, 3 * SB_WIDTH), BF16),
                   jax.ShapeDtypeStruct((t, RWKV_PAD), F32)],
        compiler_params=_params("parallel"),
        name="mix_in",
    )(h, pre, w)


def _sb_kernel(q_ref, k_ref, v_ref, g_ref, cum_ref, o_ref, k2_ref, v2_ref, carry_ref, acc_ref, *, seq):
    i = pl.program_id(2)
    nk = seq // SB_TILE
    tq = SB_QROWS
    sub = tq // SB_TILE
    lane = lax.broadcasted_iota(jnp.int32, (SB_TILE, LANES), 1)
    head0 = lane < HEAD_DIM

    @pl.when(i == 0)
    def _():
        for j in range(nk):
            kb = k_ref[j * SB_TILE:(j + 1) * SB_TILE, :] * jnp.asarray(HEAD_DIM ** -0.5, BF16)
            vb = v_ref[j * SB_TILE:(j + 1) * SB_TILE, :]
            zero = jnp.zeros_like(kb)
            k2_ref[j, :SB_TILE, :] = jnp.where(head0, kb, zero)
            k2_ref[j, SB_TILE:, :] = jnp.where(head0, zero, kb)
            v2_ref[j, :SB_TILE, :] = jnp.where(head0, vb, zero)
            v2_ref[j, SB_TILE:, :] = jnp.where(head0, zero, vb)

    cum = cum_ref[...]

    def tile(q, j, carry, causal):
        z = _dot_nt(q, k2_ref[j])
        sp = jnp.maximum(z, 0.0) + jnp.log(1.0 + jnp.exp(-jnp.abs(z)))
        lsig = z - sp
        if causal is not None:
            sp = jnp.where(causal, sp, 0.0)
        r = _dot(sp.astype(BF16), cum)
        a = jnp.exp(lsig - r[:, :2 * SB_TILE] - carry)
        if causal is not None:
            a = jnp.where(causal, a, 0.0)
        return carry + r[:, 2 * SB_TILE:], _dot(a.astype(BF16), v2_ref[j])

    carry_ref[...] = jnp.zeros_like(carry_ref)
    acc_ref[...] = jnp.zeros_like(acc_ref)
    for m in reversed(range(sub)):
        lo = m * SB_TILE
        n = tq - lo
        row = lax.broadcasted_iota(jnp.int32, (n, 2 * SB_TILE), 0)
        col = lax.broadcasted_iota(jnp.int32, (n, 2 * SB_TILE), 1) & (SB_TILE - 1)
        carry, out = tile(q_ref[lo:, :], sub * i + m, carry_ref[lo:, :], col < row)
        carry_ref[lo:, :] = carry
        acc_ref[lo:, :] += out

    q = q_ref[...]

    def body(jj, _):
        j = sub * i - 1 - 2 * jj
        carry, out0 = tile(q, j, carry_ref[...], None)
        carry, out1 = tile(q, j - 1, carry, None)
        carry_ref[...] = carry
        acc_ref[...] += out0 + out1
        return 0

    lax.fori_loop(0, (sub // 2) * i, body, 0)

    acc = acc_ref[...]
    h0 = lax.broadcasted_iota(jnp.int32, acc.shape, 1) < HEAD_DIM
    sq = acc * acc
    ms0 = jnp.sum(jnp.where(h0, sq, 0.0), axis=-1, keepdims=True)
    ms1 = jnp.sum(jnp.where(h0, 0.0, sq), axis=-1, keepdims=True)
    ms = jnp.where(h0, ms0, ms1) * (1.0 / HEAD_DIM)
    o_ref[...] = (acc * lax.rsqrt(ms + NORM_EPS) * g_ref[...]).astype(BF16)


def _sb_cum_matrix():
    n = 2 * SB_TILE
    j = jnp.arange(n)[:, None]
    s = jnp.arange(n)[None, :]
    same = (j // SB_TILE) == (s // SB_TILE)
    tri = same & (j > s)
    return jnp.concatenate([tri, same], axis=1).astype(BF16)


def _sb_attention(qkv, out_g, batch, seq):
    assert seq % SB_QROWS == 0 and (SB_QROWS // SB_TILE) % 2 == 0
    qkv = qkv.reshape(batch, seq, 3 * SB_WIDTH)
    pairs = SB_WIDTH // LANES
    kern = functools.partial(_sb_kernel, seq=seq)
    return pl.pallas_call(
        kern,
        grid=(batch, pairs, seq // SB_QROWS),
        in_specs=[
            pl.BlockSpec((None, SB_QROWS, LANES), lambda b, p, i: (b, i, p)),
            pl.BlockSpec((None, seq, LANES), lambda b, p, i: (b, 0, pairs + p)),
            pl.BlockSpec((None, seq, LANES), lambda b, p, i: (b, 0, 2 * pairs + p)),
            pl.BlockSpec((1, LANES), lambda b, p, i: (0, p)),
            _const_spec((2 * SB_TILE, 4 * SB_TILE)),
        ],
        out_specs=pl.BlockSpec((None, SB_QROWS, LANES), lambda b, p, i: (b, i, p)),
        out_shape=jax.ShapeDtypeStruct((batch, seq, SB_WIDTH), BF16),
        scratch_shapes=[pltpu.VMEM((seq // SB_TILE, 2 * SB_TILE, LANES), BF16),
                        pltpu.VMEM((seq // SB_TILE, 2 * SB_TILE, LANES), BF16),
                        pltpu.VMEM((SB_QROWS, 2 * SB_TILE), F32),
                        pltpu.VMEM((SB_QROWS, LANES), F32)],
        compiler_params=_params("parallel", "parallel", "arbitrary"),
        name="sb_attention",
    )(qkv, qkv, qkv, out_g, _sb_cum_matrix())


def _rwkv_kernel(rw_ref, prev_ref, mu_ref, w0_ref, w2_ref, a0_ref, a2_ref, g2_ref, kk_ref, ka_ref,
                 rk_ref, lg_ref, lb_ref, cum_ref, grp_ref, o_ref,
                 at_s, rt_s, bt_s, kt_s, v_s, bh_s, kh_s, gam_s, y_s, state_s):
    c = pl.program_id(1)
    rows = RWKV_ROWS
    ch = RWKV_CHUNK
    width = RWKV_WIDTH
    pairs = width // LANES

    @pl.when(c == 0)
    def _():
        state_s[...] = jnp.zeros_like(state_s)

    def gsum(x):
        hi, lo = _split2(x)
        r = _dot(jnp.concatenate([hi, lo], axis=0), grp_ref[...])
        return r[:rows] + r[rows:]

    p = rw_ref[...]
    prev_last = jnp.where(c == 0, 0.0, prev_ref[7:8, :])
    rowi = lax.broadcasted_iota(jnp.int32, p.shape, 0)
    shifted = jnp.where(rowi == 0, prev_last, pltpu.roll(p, 1, axis=0))
    p = p + (shifted - p) * mu_ref[...]
    r = p[:, :width]
    k = p[:, width:2 * width]
    v = p[:, 2 * width:3 * width]
    xwa = p[:, 3 * width:3 * width + LANES]
    xg = p[:, 3 * width + LANES:]
    lane = lax.broadcasted_iota(jnp.int32, xwa.shape, 1)
    xw = jnp.where(lane < DECAY_LORA, jnp.tanh(xwa), 0.0).astype(BF16)
    xa = jnp.where(lane < DECAY_LORA, 0.0, xwa).astype(BF16)
    wlin = w0_ref[...] + _dot(xw, w2_ref[...])
    wlog = -jax.nn.softplus(-wlin) - 0.5
    lw = -jnp.exp(wlog)
    lr = jax.nn.sigmoid(a0_ref[...] + _dot(xa, a2_ref[...]))
    gate = _dot(jax.nn.sigmoid(xg).astype(BF16), g2_ref[...])
    kk = k * kk_ref[...]
    kk = kk / jnp.maximum(jnp.sqrt(gsum(kk * kk)), 1e-12)
    k = k * (1.0 + (lr - 1.0) * ka_ref[...])
    a_vec = -kk
    b_vec = kk * lr

    cs = _dot(cum_ref[...], jnp.concatenate(_split3(lw), axis=1))
    cs = cs[:, :width] + cs[:, width:2 * width] + cs[:, 2 * width:]
    g_in = cs[:rows]
    g_tot = cs[rows:]
    e_in = jnp.exp(g_in)
    e_inv = jnp.exp(-g_in)
    e_end = jnp.exp(g_tot - g_in)
    pieces = (
        (at_s, a_vec * jnp.exp(g_in - lw)),
        (rt_s, r * e_in),
        (bt_s, b_vec * e_inv),
        (kt_s, k * e_inv),
        (v_s, v),
        (bh_s, b_vec * e_end),
        (kh_s, k * e_end),
    )
    for ref, val in pieces:
        val = val.astype(BF16)
        for pr in range(pairs):
            ref[pr] = val[:, pr * LANES:(pr + 1) * LANES]
    gam = jnp.exp(g_tot)
    for pr in range(pairs):
        gam_s[pr] = gam[:, pr * LANES:(pr + 1) * LANES]

    t_i = lax.broadcasted_iota(jnp.int32, (ch, LANES), 0)
    i_i = lax.broadcasted_iota(jnp.int32, (ch, LANES), 1) & (HEAD_DIM - 1)
    strict = i_i < t_i
    incl = i_i <= t_i
    eye = (i_i == t_i).astype(F32)
    h0 = lax.broadcasted_iota(jnp.int32, (ch, LANES), 1) < HEAD_DIM
    rr = lax.broadcasted_iota(jnp.int32, (LANES, LANES), 0) // HEAD_DIM
    cc = lax.broadcasted_iota(jnp.int32, (LANES, LANES), 1) // HEAD_DIM
    same_head = rr == cc

    def bd(y):
        y = y.astype(BF16)
        zero = jnp.zeros_like(y)
        return jnp.concatenate([jnp.where(h0, y, zero), jnp.where(h0, zero, y)], axis=0)

    def pp(x, y):
        return _dot(x.astype(BF16), bd(y))

    def chunk(ci, _):
        c0 = pl.multiple_of(ci * ch, ch)
        for pr in range(pairs):
            sl = (pr, pl.ds(c0, ch), slice(None))
            at, rt, bt, kt = at_s[sl], rt_s[sl], bt_s[sl], kt_s[sl]
            vv, bh, kh = v_s[sl], bh_s[sl], kh_s[sl]
            aa = _dot_nt(jnp.concatenate([at, rt], axis=0),
                         jnp.concatenate([bd(bt), bd(kt)], axis=0))
            n_ab = jnp.where(strict, aa[:ch, :LANES], 0.0)
            n_ak = jnp.where(strict, aa[:ch, LANES:], 0.0)
            a_rb = jnp.where(incl, aa[ch:, :LANES], 0.0)
            a_rk = jnp.where(incl, aa[ch:, LANES:], 0.0)

            blk = 8
            nd = jnp.where((t_i // blk) == (i_i // blk), n_ab, 0.0)
            n2 = pp(nd, nd)
            n4 = pp(n2, n2)
            tm = eye + nd
            tm = tm + pp(tm, n2)
            tm = tm + pp(tm, n4)
            while blk < ch:
                off = ((t_i // (2 * blk)) == (i_i // (2 * blk))) & ((t_i // blk) != (i_i // blk))
                tm = tm + pp(pp(tm, jnp.where(off, n_ab, 0.0)), tm)
                blk *= 2

            akv = pp(n_ak, vv)
            wa = pp(tm, at)
            uv = pp(tm, akv)
            rhat = rt.astype(F32) + pp(a_rb, wa)
            yv = pp(a_rb, uv) + pp(a_rk, vv)
            s0 = state_s[pr]
            s0b = s0.astype(BF16)
            y_s[sl] = _dot_nt(rhat.astype(BF16), s0b) + yv
            gc = jnp.where(same_head, _dot_tn(wa.astype(BF16), bh), 0.0)
            sv = _dot_tn(jnp.concatenate([uv.astype(BF16), vv], axis=0),
                         jnp.concatenate([bh, kh], axis=0))
            gam_row = gam_s[pr, pl.ds(c0, 1), :]
            state_s[pr] = s0 * gam_row + _dot(s0b, gc.astype(BF16)) + jnp.where(same_head, sv, 0.0)
        return 0

    lax.fori_loop(0, rows // ch, chunk, 0)

    y = jnp.concatenate([y_s[pr] for pr in range(pairs)], axis=1)
    mean = gsum(y) * (1.0 / HEAD_DIM)
    d = y - mean
    var = gsum(d * d) * (1.0 / HEAD_DIM)
    y = d * lax.rsqrt(var + LNX_EPS) * lg_ref[...] + lb_ref[...]
    bonus = gsum(r * k * rk_ref[...]) * v
    o_ref[...] = ((y + bonus) * gate).astype(BF16)


def _rwkv_cum_matrix():
    t = jnp.arange(RWKV_ROWS)[:, None]
    i = jnp.arange(RWKV_ROWS)[None, :]
    same = (t // RWKV_CHUNK) == (i // RWKV_CHUNK)
    return jnp.concatenate([same & (i <= t), same], axis=0).astype(BF16)


def _group_matrix():
    a = jnp.arange(RWKV_WIDTH)
    return (a[:, None] // HEAD_DIM == a[None, :] // HEAD_DIM).astype(BF16)


def _rwkv(rw, batch, seq, mu, w0, w2, a0, a2, g2, k_k, k_a, r_k, lnx_g, lnx_b):
    rw = rw.reshape(batch, seq, RWKV_PAD)
    rows = RWKV_ROWS
    pairs = RWKV_WIDTH // LANES
    vec = lambda n: _const_spec((1, n))
    packed = lambda dt: pltpu.VMEM((pairs, rows, LANES), dt)
    return pl.pallas_call(
        _rwkv_kernel,
        grid=(batch, seq // rows),
        in_specs=[
            pl.BlockSpec((None, rows, RWKV_PAD), lambda b, c: (b, c, 0)),
            pl.BlockSpec((None, 8, RWKV_PAD), lambda b, c: (b, jnp.maximum(c * (rows // 8) - 1, 0), 0)),
            vec(RWKV_PAD), vec(RWKV_WIDTH), _const_spec((LANES, RWKV_WIDTH)), vec(RWKV_WIDTH),
            _const_spec((LANES, RWKV_WIDTH)), _const_spec((GATE_PAD, RWKV_WIDTH)),
            vec(RWKV_WIDTH), vec(RWKV_WIDTH), vec(RWKV_WIDTH), vec(RWKV_WIDTH), vec(RWKV_WIDTH),
            _const_spec((2 * rows, rows)), _const_spec((RWKV_WIDTH, RWKV_WIDTH)),
        ],
        out_specs=pl.BlockSpec((None, rows, RWKV_WIDTH), lambda b, c: (b, c, 0)),
        out_shape=jax.ShapeDtypeStruct((batch, seq, RWKV_WIDTH), BF16),
        scratch_shapes=[packed(BF16)] * 7 + [packed(F32), packed(F32),
                                             pltpu.VMEM((pairs, LANES, LANES), F32)],
        compiler_params=_params("parallel", "arbitrary"),
        name="rwkv7",
    )(rw, rw, mu, w0, w2, a0, a2, g2, k_k, k_a, r_k, lnx_g, lnx_b,
      _rwkv_cum_matrix(), _group_matrix())


def _mix_out_kernel(sb_ref, rw_ref, h_ref, w_ref, post_ref, o_ref):
    mixed = _dot(sb_ref[...], w_ref[:SB_WIDTH, :]) + _dot(rw_ref[...], w_ref[SB_WIDTH:, :])
    o_ref[...] = h_ref[...] + _rms(mixed, post_ref[...])


def _mix_out(sb_o, rw_o, h, w, post):
    t = h.shape[0]
    tm = min(TOKEN_TILE, t)
    row = lambda i: (i, 0)
    return pl.pallas_call(
        _mix_out_kernel,
        grid=(t // tm,),
        in_specs=[
            pl.BlockSpec((tm, SB_WIDTH), row),
            pl.BlockSpec((tm, RWKV_WIDTH), row),
            pl.BlockSpec((tm, D_MODEL), row),
            _const_spec((MIX_WIDTH, D_MODEL)),
            _const_spec((1, D_MODEL)),
        ],
        out_specs=pl.BlockSpec((tm, D_MODEL), row),
        out_shape=jax.ShapeDtypeStruct((t, D_MODEL), F32),
        compiler_params=_params("parallel"),
        name="mix_out",
    )(sb_o, rw_o, h, w, post)


def _mem_kv_kernel(m_ref, g_ref, w_ref, o_ref):
    o_ref[...] = _dot(_rms(m_ref[...], g_ref[...]).astype(BF16), w_ref[...]).astype(BF16)


def _mem_kv(mem, g, w):
    t = mem.shape[0]
    tm = min(TOKEN_TILE, t)
    row = lambda i: (i, 0)
    return pl.pallas_call(
        _mem_kv_kernel,
        grid=(t // tm,),
        in_specs=[pl.BlockSpec((tm, D_MODEL), row), _const_spec((1, D_MODEL)),
                  _const_spec((D_MODEL, 2 * MEM_WIDTH))],
        out_specs=pl.BlockSpec((tm, 2 * MEM_WIDTH), row),
        out_shape=jax.ShapeDtypeStruct((t, 2 * MEM_WIDTH), BF16),
        compiler_params=_params("parallel"),
        name="mem_kv",
    )(mem, g, w)


def _mem_attn_kernel(h_ref, kv_ref, pre_ref, wq_ref, wo_ref, post_ref, o_ref):
    h = h_ref[...]
    q = _dot(_rms(h, pre_ref[...]).astype(BF16), wq_ref[...])
    q = (q * (MEM_HEAD_DIM ** -0.5)).astype(BF16)
    outs = []
    for hd in range(MEM_HEADS):
        lo = hd * MEM_HEAD_DIM
        s = _dot_nt(q[:, lo:lo + MEM_HEAD_DIM], kv_ref[:, lo:lo + MEM_HEAD_DIM])
        e = jnp.exp(s - jnp.max(s, axis=-1, keepdims=True))
        pr = e / jnp.sum(e, axis=-1, keepdims=True)
        outs.append(_dot(pr.astype(BF16), kv_ref[:, MEM_WIDTH + lo:MEM_WIDTH + lo + MEM_HEAD_DIM]))
    o = jnp.concatenate(outs, axis=1).astype(BF16)
    o_ref[...] = h + _rms(_dot(o, wo_ref[...]), post_ref[...])


def _mem_attn(h, kv, pre, w_q, w_o, post, batch, seq):
    mem_len = kv.shape[0] // batch
    kv = kv.reshape(batch, mem_len, 2 * MEM_WIDTH)
    h = h.reshape(batch, seq, D_MODEL)
    tm = min(TOKEN_TILE, seq)
    out = pl.pallas_call(
        _mem_attn_kernel,
        grid=(batch, seq // tm),
        in_specs=[
            pl.BlockSpec((None, tm, D_MODEL), lambda b, i: (b, i, 0)),
            pl.BlockSpec((None, mem_len, 2 * MEM_WIDTH), lambda b, i: (b, 0, 0)),
            _const_spec((1, D_MODEL)),
            _const_spec((D_MODEL, MEM_WIDTH)),
            _const_spec((MEM_WIDTH, D_MODEL)),
            _const_spec((1, D_MODEL)),
        ],
        out_specs=pl.BlockSpec((None, tm, D_MODEL), lambda b, i: (b, i, 0)),
        out_shape=jax.ShapeDtypeStruct((batch, seq, D_MODEL), F32),
        compiler_params=_params("parallel", "parallel"),
        name="mem_attn",
    )(h, kv, pre, w_q, w_o, post)
    return out.reshape(batch * seq, D_MODEL)


def _row(a):
    return a.reshape(1, -1).astype(F32)


def _pad_rows(w, n):
    return jnp.pad(w, ((0, n - w.shape[0]), (0, 0)))


def kernel(x, mem, ffn1_pre, ffn1_post, ffn1_w_in, ffn1_w_out, mix_pre, mix_post, mix_w_in, rwkv_mu, rwkv_w0, rwkv_w2, rwkv_a0, rwkv_a2, rwkv_g2, rwkv_k_k, rwkv_k_a, rwkv_r_k, rwkv_lnx_g, rwkv_lnx_b, sb_out_g, mix_w_out, mem_pre, mem_post, mem_kv_g, mem_w_q, mem_w_kv, mem_w_o, ffn2_pre, ffn2_post, ffn2_w_in, ffn2_w_out):
    batch, seq, _ = x.shape
    depth = ffn1_pre.shape[0]
    h = x.reshape(batch * seq, D_MODEL)
    mem2 = mem.reshape(-1, D_MODEL)
    pad_cols = RWKV_PAD - RWKV_IN
    for l in range(depth):
        h = _ffn(h, _row(ffn1_pre[l]), _row(ffn1_post[l]),
                 ffn1_w_in[l].astype(BF16), ffn1_w_out[l].astype(BF16))

        w_in = jnp.pad(mix_w_in[l], ((0, 0), (0, pad_cols))).astype(BF16)
        qkv, rw = _mix_in(h, _row(mix_pre[l]), w_in)
        sb_o = _sb_attention(qkv, _row(sb_out_g[l]), batch, seq)
        w2 = _pad_rows(rwkv_w2[l], LANES).astype(BF16)
        a2 = jnp.pad(rwkv_a2[l], ((DECAY_LORA, 0), (0, 0))).astype(BF16)
        g2 = _pad_rows(rwkv_g2[l], GATE_PAD).astype(BF16)
        mu = jnp.pad(rwkv_mu[l], (0, pad_cols))
        rw_o = _rwkv(rw, batch, seq, _row(mu), _row(rwkv_w0[l]), w2, _row(rwkv_a0[l]), a2, g2,
                     _row(rwkv_k_k[l]), _row(rwkv_k_a[l]), _row(rwkv_r_k[l]),
                     _row(rwkv_lnx_g[l]), _row(rwkv_lnx_b[l]))
        h = _mix_out(sb_o.reshape(batch * seq, SB_WIDTH), rw_o.reshape(batch * seq, RWKV_WIDTH), h,
                     mix_w_out[l].astype(BF16), _row(mix_post[l]))

        kv = _mem_kv(mem2, _row(mem_kv_g[l]), mem_w_kv[l].astype(BF16))
        h = _mem_attn(h, kv, _row(mem_pre[l]), mem_w_q[l].astype(BF16), mem_w_o[l].astype(BF16),
                      _row(mem_post[l]), batch, seq)

        h = _ffn(h, _row(ffn2_pre[l]), _row(ffn2_post[l]),
                 ffn2_w_in[l].astype(BF16), ffn2_w_out[l].astype(BF16))
    return h.reshape(batch, seq, D_MODEL)
```

```python
import functools

import jax
import jax.numpy as jnp
from jax import lax
from jax.experimental import pallas as pl
from jax.experimental.pallas import tpu as pltpu

D_MODEL = 1024
HEAD_DIM = 64
SB_HEADS = 8
RWKV_HEADS = 8
SB_WIDTH = SB_HEADS * HEAD_DIM
RWKV_WIDTH = RWKV_HEADS * HEAD_DIM
MIX_WIDTH = SB_WIDTH + RWKV_WIDTH
DECAY_LORA = 64
AAA_LORA = 64
GATE_LORA = 160
RWKV_IN = 3 * RWKV_WIDTH + DECAY_LORA + AAA_LORA + GATE_LORA
MEM_HEADS = 4
MEM_HEAD_DIM = 128
MEM_WIDTH = MEM_HEADS * MEM_HEAD_DIM
D_FF = 2816
NORM_EPS = 1e-6
LNX_EPS = 64e-5

LANES = 128
MXU_DIM = 256
V7X_VMEM_BYTES = 64 * 1024 * 1024
VMEM_LIMIT_BYTES = V7X_VMEM_BYTES - 8 * 1024 * 1024

RWKV_PAD = 15 * LANES
GATE_PAD = RWKV_PAD - (3 * RWKV_WIDTH + LANES)
SB_TILE = 128
SB_QROWS = 512
RWKV_CHUNK = 64
RWKV_ROWS = 256
RWKV_QUAD = MXU_DIM
TOKEN_TILE = 512
FF_CHUNK = 256

BF16 = jnp.bfloat16
F32 = jnp.float32


def _const_spec(shape):
    nd = len(shape)
    return pl.BlockSpec(shape, lambda *_: (0,) * nd, pipeline_mode=pl.Buffered(1))


def _params(*sem):
    return pltpu.CompilerParams(dimension_semantics=sem, vmem_limit_bytes=VMEM_LIMIT_BYTES)


def _rms(x, g):
    ms = jnp.mean(x * x, axis=-1, keepdims=True)
    return x * lax.rsqrt(ms + NORM_EPS) * g


def _dot(a, b):
    return jnp.dot(a, b, preferred_element_type=F32)


def _dot_nt(a, b):
    return lax.dot_general(a, b, (((1,), (1,)), ((), ())), preferred_element_type=F32)


def _dot_tn(a, b):
    return lax.dot_general(a, b, (((0,), (0,)), ((), ())), preferred_element_type=F32)


def _split2(x):
    hi = x.astype(BF16)
    lo = (x - hi.astype(F32)).astype(BF16)
    return hi, lo


def _split3(x):
    hi = x.astype(BF16)
    r1 = x - hi.astype(F32)
    mid = r1.astype(BF16)
    lo = (r1 - mid.astype(F32)).astype(BF16)
    return hi, mid, lo


def _ffn_kernel(x_ref, pre_ref, post_ref, win_ref, wout_ref, o_ref, act_ref):
    x = x_ref[...]
    xn = _rms(x, pre_ref[...]).astype(BF16)
    for c in range(D_FF // FF_CHUNK):
        lo = c * FF_CHUNK
        gate = _dot(xn, win_ref[:, lo:lo + FF_CHUNK])
        up = _dot(xn, win_ref[:, D_FF + lo:D_FF + lo + FF_CHUNK])
        act_ref[:, lo:lo + FF_CHUNK] = (gate * jax.nn.sigmoid(gate) * up).astype(BF16)
    y = _dot(act_ref[...], wout_ref[...])
    o_ref[...] = x + 0.5 * _rms(y, post_ref[...])


def _ffn(h, pre, post, w_in, w_out):
    t = h.shape[0]
    tm = min(TOKEN_TILE, t)
    row = lambda i: (i, 0)
    return pl.pallas_call(
        _ffn_kernel,
        grid=(t // tm,),
        in_specs=[
            pl.BlockSpec((tm, D_MODEL), row),
            _const_spec((1, D_MODEL)),
            _const_spec((1, D_MODEL)),
            _const_spec((D_MODEL, 2 * D_FF)),
            _const_spec((D_FF, D_MODEL)),
        ],
        out_specs=pl.BlockSpec((tm, D_MODEL), row),
        out_shape=jax.ShapeDtypeStruct((t, D_MODEL), F32),
        scratch_shapes=[pltpu.VMEM((tm, D_FF), BF16)],
        compiler_params=_params("parallel"),
        name="ffn",
    )(h, pre, post, w_in, w_out)


def _mix_in_kernel(h_ref, pre_ref, w_ref, qkv_ref, rw_ref):
    xn = _rms(h_ref[...], pre_ref[...]).astype(BF16)
    qkv_ref[...] = _dot(xn, w_ref[:, :3 * SB_WIDTH]).astype(BF16)
    rw_ref[...] = _dot(xn, w_ref[:, 3 * SB_WIDTH:])


def _mix_in(h, pre, w):
    t = h.shape[0]
    tm = min(TOKEN_TILE, t)
    row = lambda i: (i, 0)
    return pl.pallas_call(
        _mix_in_kernel,
        grid=(t // tm,),
        in_specs=[
            pl.BlockSpec((tm, D_MODEL), row),
            _const_spec((1, D_MODEL)),
            _const_spec((D_MODEL, 3 * SB_WIDTH + RWKV_PAD)),
        ],
        out_specs=[pl.BlockSpec((tm, 3 * SB_WIDTH), row), pl.BlockSpec((tm, RWKV_PAD), row)],
        out_shape=[jax.ShapeDtypeStruct((t, 3 * SB_WIDTH), BF16),
                   jax.ShapeDtypeStruct((t, RWKV_PAD), F32)],
        compiler_params=_params("parallel"),
        name="mix_in",
    )(h, pre, w)


def _sb_kernel(q_ref, k_ref, v_ref, g_ref, cum_ref, o_ref, k2_ref, v2_ref, carry_ref, acc_ref, *, seq):
    i = pl.program_id(2)
    nk = seq // SB_TILE
    tq = SB_QROWS
    sub = tq // SB_TILE
    lane = lax.broadcasted_iota(jnp.int32, (SB_TILE, LANES), 1)
    head0 = lane < HEAD_DIM

    @pl.when(i == 0)
    def _():
        for j in range(nk):
            kb = k_ref[j * SB_TILE:(j + 1) * SB_TILE, :] * jnp.asarray(HEAD_DIM ** -0.5, BF16)
            vb = v_ref[j * SB_TILE:(j + 1) * SB_TILE, :]
            zero = jnp.zeros_like(kb)
            k2_ref[j, :SB_TILE, :] = jnp.where(head0, kb, zero)
            k2_ref[j, SB_TILE:, :] = jnp.where(head0, zero, kb)
            v2_ref[j, :SB_TILE, :] = jnp.where(head0, vb, zero)
            v2_ref[j, SB_TILE:, :] = jnp.where(head0, zero, vb)

    cum = cum_ref[...]

    def tile(q, j, carry, causal):
        z = _dot_nt(q, k2_ref[j])
        sp = jnp.maximum(z, 0.0) + jnp.log(1.0 + jnp.exp(-jnp.abs(z)))
        lsig = z - sp
        if causal is not None:
            sp = jnp.where(causal, sp, 0.0)
        r = _dot(sp.astype(BF16), cum)
        a = jnp.exp(lsig - r[:, :2 * SB_TILE] - carry)
        if causal is not None:
            a = jnp.where(causal, a, 0.0)
        return carry + r[:, 2 * SB_TILE:], _dot(a.astype(BF16), v2_ref[j])

    carry_ref[...] = jnp.zeros_like(carry_ref)
    acc_ref[...] = jnp.zeros_like(acc_ref)
    for m in reversed(range(sub)):
        lo = m * SB_TILE
        n = tq - lo
        row = lax.broadcasted_iota(jnp.int32, (n, 2 * SB_TILE), 0)
        col = lax.broadcasted_iota(jnp.int32, (n, 2 * SB_TILE), 1) & (SB_TILE - 1)
        carry, out = tile(q_ref[lo:, :], sub * i + m, carry_ref[lo:, :], col < row)
        carry_ref[lo:, :] = carry
        acc_ref[lo:, :] += out

    q = q_ref[...]

    def body(jj, _):
        j = sub * i - 1 - 2 * jj
        carry, out0 = tile(q, j, carry_ref[...], None)
        carry, out1 = tile(q, j - 1, carry, None)
        carry_ref[...] = carry
        acc_ref[...] += out0 + out1
        return 0

    lax.fori_loop(0, (sub // 2) * i, body, 0)

    acc = acc_ref[...]
    h0 = lax.broadcasted_iota(jnp.int32, acc.shape, 1) < HEAD_DIM
    sq = acc * acc
    ms0 = jnp.sum(jnp.where(h0, sq, 0.0), axis=-1, keepdims=True)
    ms1 = jnp.sum(jnp.where(h0, 0.0, sq), axis=-1, keepdims=True)
    ms = jnp.where(h0, ms0, ms1) * (1.0 / HEAD_DIM)
    o_ref[...] = (acc * lax.rsqrt(ms + NORM_EPS) * g_ref[...]).astype(BF16)


def _sb_cum_matrix():
    n = 2 * SB_TILE
    j = jnp.arange(n)[:, None]
    s = jnp.arange(n)[None, :]
    same = (j // SB_TILE) == (s // SB_TILE)
    tri = same & (j > s)
    return jnp.concatenate([tri, same], axis=1).astype(BF16)


def _sb_attention(qkv, out_g, batch, seq):
    assert seq % SB_QROWS == 0 and (SB_QROWS // SB_TILE) % 2 == 0
    qkv = qkv.reshape(batch, seq, 3 * SB_WIDTH)
    pairs = SB_WIDTH // LANES
    kern = functools.partial(_sb_kernel, seq=seq)
    return pl.pallas_call(
        kern,
        grid=(batch, pairs, seq // SB_QROWS),
        in_specs=[
            pl.BlockSpec((None, SB_QROWS, LANES), lambda b, p, i: (b, i, p)),
            pl.BlockSpec((None, seq, LANES), lambda b, p, i: (b, 0, pairs + p)),
            pl.BlockSpec((None, seq, LANES), lambda b, p, i: (b, 0, 2 * pairs + p)),
            pl.BlockSpec((1, LANES), lambda b, p, i: (0, p)),
            _const_spec((2 * SB_TILE, 4 * SB_TILE)),
        ],
        out_specs=pl.BlockSpec((None, SB_QROWS, LANES), lambda b, p, i: (b, i, p)),
        out_shape=jax.ShapeDtypeStruct((batch, seq, SB_WIDTH), BF16),
        scratch_shapes=[pltpu.VMEM((seq // SB_TILE, 2 * SB_TILE, LANES), BF16),
                        pltpu.VMEM((seq // SB_TILE, 2 * SB_TILE, LANES), BF16),
                        pltpu.VMEM((SB_QROWS, 2 * SB_TILE), F32),
                        pltpu.VMEM((SB_QROWS, LANES), F32)],
        compiler_params=_params("parallel", "parallel", "arbitrary"),
        name="sb_attention",
    )(qkv, qkv, qkv, out_g, _sb_cum_matrix())


def _rwkv_kernel(rw_ref, prev_ref, mu_ref, w0_ref, w2_ref, a0_ref, a2_ref, g2_ref, kk_ref, ka_ref,
                 rk_ref, lg_ref, lb_ref, cum_ref, grp_ref, o_ref, state_s):
    c = pl.program_id(1)
    rows = RWKV_ROWS
    ch = RWKV_CHUNK
    width = RWKV_WIDTH
    quad = RWKV_QUAD
    heads = quad // HEAD_DIM

    @pl.when(c == 0)
    def _():
        state_s[...] = jnp.zeros_like(state_s)

    def gsum(x):
        hi, lo = _split2(x)
        r = _dot(jnp.concatenate([hi, lo], axis=0), grp_ref[...])
        return r[:rows] + r[rows:]

    p = rw_ref[...]
    prev_last = jnp.where(c == 0, 0.0, prev_ref[7:8, :])
    rowi = lax.broadcasted_iota(jnp.int32, p.shape, 0)
    shifted = jnp.where(rowi == 0, prev_last, pltpu.roll(p, 1, axis=0))
    p = p + (shifted - p) * mu_ref[...]
    r = p[:, :width]
    k = p[:, width:2 * width]
    v = p[:, 2 * width:3 * width]
    xwa = p[:, 3 * width:3 * width + LANES]
    xg = p[:, 3 * width + LANES:]
    lane = lax.broadcasted_iota(jnp.int32, xwa.shape, 1)
    xw = jnp.where(lane < DECAY_LORA, jnp.tanh(xwa), 0.0).astype(BF16)
    xa = jnp.where(lane < DECAY_LORA, 0.0, xwa).astype(BF16)
    wlin = w0_ref[...] + _dot(xw, w2_ref[...])
    wlog = -jax.nn.softplus(-wlin) - 0.5
    lw = -jnp.exp(wlog)
    lr = jax.nn.sigmoid(a0_ref[...] + _dot(xa, a2_ref[...]))
    gate = _dot(jax.nn.sigmoid(xg).astype(BF16), g2_ref[...])
    kk = k * kk_ref[...]
    kk = kk / jnp.maximum(jnp.sqrt(gsum(kk * kk)), 1e-12)
    k = k * (1.0 + (lr - 1.0) * ka_ref[...])
    a_vec = -kk
    b_vec = kk * lr

    cs = _dot(cum_ref[...], jnp.concatenate(_split3(lw), axis=1))
    cs = cs[:, :width] + cs[:, width:2 * width] + cs[:, 2 * width:]
    g_in = cs[:rows]
    g_tot = cs[rows:]
    e_in = jnp.exp(g_in)
    e_inv = jnp.exp(-g_in)
    e_end = jnp.exp(g_tot - g_in)
    at_all = (a_vec * jnp.exp(g_in - lw)).astype(BF16)
    rt_all = (r * e_in).astype(BF16)
    bt_all = (b_vec * e_inv).astype(BF16)
    kt_all = (k * e_inv).astype(BF16)
    v_all = v.astype(BF16)
    bh_all = (b_vec * e_end).astype(BF16)
    kh_all = (k * e_end).astype(BF16)
    gam_all = jnp.exp(g_tot)

    t_i = lax.broadcasted_iota(jnp.int32, (ch, quad), 0)
    l_i = lax.broadcasted_iota(jnp.int32, (ch, quad), 1)
    i_i = l_i & (HEAD_DIM - 1)
    strict = i_i < t_i
    incl = i_i <= t_i
    eye = (i_i == t_i).astype(F32)
    in_head = [(l_i // HEAD_DIM) == h for h in range(heads)]
    rr = lax.broadcasted_iota(jnp.int32, (quad, quad), 0) // HEAD_DIM
    cc = lax.broadcasted_iota(jnp.int32, (quad, quad), 1) // HEAD_DIM
    same_head = rr == cc

    def bd(y):
        y = y.astype(BF16)
        zero = jnp.zeros_like(y)
        return jnp.concatenate([jnp.where(m, y, zero) for m in in_head], axis=0)

    def pp(x, y):
        return _dot(x.astype(BF16), bd(y))

    def unpack_diag(full):
        out = jnp.where(in_head[0], full[:ch], 0.0)
        for h in range(1, heads):
            out = out + jnp.where(in_head[h], full[h * ch:(h + 1) * ch], 0.0)
        return out

    nq = width // quad
    combos = [(ci, qd) for ci in range(rows // ch) for qd in range(nq)]

    def cut(x, ci, qd):
        return x[ci * ch:(ci + 1) * ch, qd * quad:(qd + 1) * quad]

    def each(fn, *lists):
        return [fn(*args) for args in zip(*lists)]

    at = [cut(at_all, *cq) for cq in combos]
    rt = [cut(rt_all, *cq) for cq in combos]
    vv = [cut(v_all, *cq) for cq in combos]
    bh = [cut(bh_all, *cq) for cq in combos]
    kh = [cut(kh_all, *cq) for cq in combos]
    aa = [_dot_nt(jnp.concatenate([a, r_], axis=0),
                  jnp.concatenate([bd(cut(bt_all, *cq)), bd(cut(kt_all, *cq))], axis=0))
          for a, r_, cq in zip(at, rt, combos)]
    n_ab = [jnp.where(strict, x[:ch, :quad], 0.0) for x in aa]
    n_ak = [jnp.where(strict, x[:ch, quad:], 0.0) for x in aa]
    a_rb = [jnp.where(incl, x[ch:, :quad], 0.0) for x in aa]
    a_rk = [jnp.where(incl, x[ch:, quad:], 0.0) for x in aa]

    blk = 8
    diag = (t_i // blk) == (i_i // blk)
    nd = [jnp.where(diag, x, 0.0) for x in n_ab]
    n2 = each(pp, nd, nd)
    n4 = each(pp, n2, n2)
    tm = [eye + x for x in nd]
    tm = each(lambda t, n: t + pp(t, n), tm, n2)
    tm = each(lambda t, n: t + pp(t, n), tm, n4)
    while blk < ch:
        off = ((t_i // (2 * blk)) == (i_i // (2 * blk))) & ((t_i // blk) != (i_i // blk))
        half = each(lambda t, n: pp(t, jnp.where(off, n, 0.0)), tm, n_ab)
        tm = each(lambda t, h: t + pp(h, t), tm, half)
        blk *= 2

    akv = each(pp, n_ak, vv)
    wu = each(lambda t, a, b: _dot(t.astype(BF16), jnp.concatenate([bd(a), bd(b)], axis=1)),
              tm, at, akv)
    wa = [x[:, :quad] for x in wu]
    uv = [x[:, quad:] for x in wu]
    ry = each(lambda a, w, u: _dot(a.astype(BF16), jnp.concatenate([bd(w), bd(u)], axis=1)),
              a_rb, wa, uv)
    rhat = each(lambda r_, x: r_.astype(F32) + x[:, :quad], rt, ry)
    yv = each(lambda x, a, v_: x[:, quad:] + pp(a, v_), ry, a_rk, vv)
    gc = each(lambda w, b: jnp.where(same_head, _dot_tn(w.astype(BF16), b), 0.0).astype(BF16), wa, bh)
    sv = each(lambda u, v_, b, k_: unpack_diag(_dot_tn(jnp.concatenate([u.astype(BF16), v_], axis=0),
                                                       jnp.concatenate([b, k_], axis=0))),
              uv, vv, bh, kh)

    states = [state_s[qd] for qd in range(nq)]
    y_out = {}
    for n, (ci, qd) in enumerate(combos):
        s0 = states[qd]
        y_out[ci, qd] = _dot_nt(rhat[n].astype(BF16), bd(s0)) + yv[n]
        gam_row = gam_all[ci * ch:ci * ch + 1, qd * quad:(qd + 1) * quad]
        states[qd] = s0 * gam_row + _dot(s0.astype(BF16), gc[n]) + sv[n]
    for qd in range(nq):
        state_s[qd] = states[qd]
    y_rows = [jnp.concatenate([y_out[ci, qd] for qd in range(nq)], axis=1)
              for ci in range(rows // ch)]

    y = jnp.concatenate(y_rows, axis=0)
    mean = gsum(y) * (1.0 / HEAD_DIM)
    d = y - mean
    var = gsum(d * d) * (1.0 / HEAD_DIM)
    y = d * lax.rsqrt(var + LNX_EPS) * lg_ref[...] + lb_ref[...]
    bonus = gsum(r * k * rk_ref[...]) * v
    o_ref[...] = ((y + bonus) * gate).astype(BF16)


def _rwkv_cum_matrix():
    t = jnp.arange(RWKV_ROWS)[:, None]
    i = jnp.arange(RWKV_ROWS)[None, :]
    same = (t // RWKV_CHUNK) == (i // RWKV_CHUNK)
    return jnp.concatenate([same & (i <= t), same], axis=0).astype(BF16)


def _group_matrix():
    a = jnp.arange(RWKV_WIDTH)
    return (a[:, None] // HEAD_DIM == a[None, :] // HEAD_DIM).astype(BF16)


def _rwkv(rw, batch, seq, mu, w0, w2, a0, a2, g2, k_k, k_a, r_k, lnx_g, lnx_b):
    rw = rw.reshape(batch, seq, RWKV_PAD)
    rows = RWKV_ROWS
    vec = lambda n: _const_spec((1, n))
    return pl.pallas_call(
        _rwkv_kernel,
        grid=(batch, seq // rows),
        in_specs=[
            pl.BlockSpec((None, rows, RWKV_PAD), lambda b, c: (b, c, 0)),
            pl.BlockSpec((None, 8, RWKV_PAD), lambda b, c: (b, jnp.maximum(c * (rows // 8) - 1, 0), 0)),
            vec(RWKV_PAD), vec(RWKV_WIDTH), _const_spec((LANES, RWKV_WIDTH)), vec(RWKV_WIDTH),
            _const_spec((LANES, RWKV_WIDTH)), _const_spec((GATE_PAD, RWKV_WIDTH)),
            vec(RWKV_WIDTH), vec(RWKV_WIDTH), vec(RWKV_WIDTH), vec(RWKV_WIDTH), vec(RWKV_WIDTH),
            _const_spec((2 * rows, rows)), _const_spec((RWKV_WIDTH, RWKV_WIDTH)),
        ],
        out_specs=pl.BlockSpec((None, rows, RWKV_WIDTH), lambda b, c: (b, c, 0)),
        out_shape=jax.ShapeDtypeStruct((batch, seq, RWKV_WIDTH), BF16),
        scratch_shapes=[pltpu.VMEM((RWKV_WIDTH // RWKV_QUAD, RWKV_CHUNK, RWKV_QUAD), F32)],
        compiler_params=_params("parallel", "arbitrary"),
        name="rwkv7",
    )(rw, rw, mu, w0, w2, a0, a2, g2, k_k, k_a, r_k, lnx_g, lnx_b,
      _rwkv_cum_matrix(), _group_matrix())


def _mix_out_kernel(sb_ref, rw_ref, h_ref, w_ref, post_ref, o_ref):
    mixed = _dot(sb_ref[...], w_ref[:SB_WIDTH, :]) + _dot(rw_ref[...], w_ref[SB_WIDTH:, :])
    o_ref[...] = h_ref[...] + _rms(mixed, post_ref[...])


def _mix_out(sb_o, rw_o, h, w, post):
    t = h.shape[0]
    tm = min(TOKEN_TILE, t)
    row = lambda i: (i, 0)
    return pl.pallas_call(
        _mix_out_kernel,
        grid=(t // tm,),
        in_specs=[
            pl.BlockSpec((tm, SB_WIDTH), row),
            pl.BlockSpec((tm, RWKV_WIDTH), row),
            pl.BlockSpec((tm, D_MODEL), row),
            _const_spec((MIX_WIDTH, D_MODEL)),
            _const_spec((1, D_MODEL)),
        ],
        out_specs=pl.BlockSpec((tm, D_MODEL), row),
        out_shape=jax.ShapeDtypeStruct((t, D_MODEL), F32),
        compiler_params=_params("parallel"),
        name="mix_out",
    )(sb_o, rw_o, h, w, post)


def _mem_kv_kernel(m_ref, g_ref, w_ref, o_ref):
    o_ref[...] = _dot(_rms(m_ref[...], g_ref[...]).astype(BF16), w_ref[...]).astype(BF16)


def _mem_kv(mem, g, w):
    t = mem.shape[0]
    tm = min(TOKEN_TILE, t)
    row = lambda i: (i, 0)
    return pl.pallas_call(
        _mem_kv_kernel,
        grid=(t // tm,),
        in_specs=[pl.BlockSpec((tm, D_MODEL), row), _const_spec((1, D_MODEL)),
                  _const_spec((D_MODEL, 2 * MEM_WIDTH))],
        out_specs=pl.BlockSpec((tm, 2 * MEM_WIDTH), row),
        out_shape=jax.ShapeDtypeStruct((t, 2 * MEM_WIDTH), BF16),
        compiler_params=_params("parallel"),
        name="mem_kv",
    )(mem, g, w)


def _mem_attn_kernel(h_ref, kv_ref, pre_ref, wq_ref, wo_ref, post_ref, o_ref):
    h = h_ref[...]
    q = _dot(_rms(h, pre_ref[...]).astype(BF16), wq_ref[...])
    q = (q * (MEM_HEAD_DIM ** -0.5)).astype(BF16)
    outs = []
    for hd in range(MEM_HEADS):
        lo = hd * MEM_HEAD_DIM
        s = _dot_nt(q[:, lo:lo + MEM_HEAD_DIM], kv_ref[:, lo:lo + MEM_HEAD_DIM])
        e = jnp.exp(s - jnp.max(s, axis=-1, keepdims=True))
        pr = e / jnp.sum(e, axis=-1, keepdims=True)
        outs.append(_dot(pr.astype(BF16), kv_ref[:, MEM_WIDTH + lo:MEM_WIDTH + lo + MEM_HEAD_DIM]))
    o = jnp.concatenate(outs, axis=1).astype(BF16)
    o_ref[...] = h + _rms(_dot(o, wo_ref[...]), post_ref[...])


def _mem_attn(h, kv, pre, w_q, w_o, post, batch, seq):
    mem_len = kv.shape[0] // batch
    kv = kv.reshape(batch, mem_len, 2 * MEM_WIDTH)
    h = h.reshape(batch, seq, D_MODEL)
    tm = min(TOKEN_TILE, seq)
    out = pl.pallas_call(
        _mem_attn_kernel,
        grid=(batch, seq // tm),
        in_specs=[
            pl.BlockSpec((None, tm, D_MODEL), lambda b, i: (b, i, 0)),
            pl.BlockSpec((None, mem_len, 2 * MEM_WIDTH), lambda b, i: (b, 0, 0)),
            _const_spec((1, D_MODEL)),
            _const_spec((D_MODEL, MEM_WIDTH)),
            _const_spec((MEM_WIDTH, D_MODEL)),
            _const_spec((1, D_MODEL)),
        ],
        out_specs=pl.BlockSpec((None, tm, D_MODEL), lambda b, i: (b, i, 0)),
        out_shape=jax.ShapeDtypeStruct((batch, seq, D_MODEL), F32),
        compiler_params=_params("parallel", "parallel"),
        name="mem_attn",
    )(h, kv, pre, w_q, w_o, post)
    return out.reshape(batch * seq, D_MODEL)


def _row(a):
    return a.reshape(1, -1).astype(F32)


def _pad_rows(w, n):
    return jnp.pad(w, ((0, n - w.shape[0]), (0, 0)))


def kernel(x, mem, ffn1_pre, ffn1_post, ffn1_w_in, ffn1_w_out, mix_pre, mix_post, mix_w_in, rwkv_mu, rwkv_w0, rwkv_w2, rwkv_a0, rwkv_a2, rwkv_g2, rwkv_k_k, rwkv_k_a, rwkv_r_k, rwkv_lnx_g, rwkv_lnx_b, sb_out_g, mix_w_out, mem_pre, mem_post, mem_kv_g, mem_w_q, mem_w_kv, mem_w_o, ffn2_pre, ffn2_post, ffn2_w_in, ffn2_w_out):
    batch, seq, _ = x.shape
    depth = ffn1_pre.shape[0]
    h = x.reshape(batch * seq, D_MODEL)
    mem2 = mem.reshape(-1, D_MODEL)
    pad_cols = RWKV_PAD - RWKV_IN
    for l in range(depth):
        h = _ffn(h, _row(ffn1_pre[l]), _row(ffn1_post[l]),
                 ffn1_w_in[l].astype(BF16), ffn1_w_out[l].astype(BF16))

        w_in = jnp.pad(mix_w_in[l], ((0, 0), (0, pad_cols))).astype(BF16)
        qkv, rw = _mix_in(h, _row(mix_pre[l]), w_in)
        sb_o = _sb_attention(qkv, _row(sb_out_g[l]), batch, seq)
        w2 = _pad_rows(rwkv_w2[l], LANES).astype(BF16)
        a2 = jnp.pad(rwkv_a2[l], ((DECAY_LORA, 0), (0, 0))).astype(BF16)
        g2 = _pad_rows(rwkv_g2[l], GATE_PAD).astype(BF16)
        mu = jnp.pad(rwkv_mu[l], (0, pad_cols))
        rw_o = _rwkv(rw, batch, seq, _row(mu), _row(rwkv_w0[l]), w2, _row(rwkv_a0[l]), a2, g2,
                     _row(rwkv_k_k[l]), _row(rwkv_k_a[l]), _row(rwkv_r_k[l]),
                     _row(rwkv_lnx_g[l]), _row(rwkv_lnx_b[l]))
        h = _mix_out(sb_o.reshape(batch * seq, SB_WIDTH), rw_o.reshape(batch * seq, RWKV_WIDTH), h,
                     mix_w_out[l].astype(BF16), _row(mix_post[l]))

        kv = _mem_kv(mem2, _row(mem_kv_g[l]), mem_w_kv[l].astype(BF16))
        h = _mem_attn(h, kv, _row(mem_pre[l]), mem_w_q[l].astype(BF16), mem_w_o[l].astype(BF16),
                      _row(mem_post[l]), batch, seq)

        h = _ffn(h, _row(ffn2_pre[l]), _row(ffn2_post[l]),
                 ffn2_w_in[l].astype(BF16), ffn2_w_out[l].astype(BF16))
    return h.reshape(batch, seq, D_MODEL)
```

```python
import functools

import jax
import jax.numpy as jnp
from jax import lax
from jax.experimental import pallas as pl
from jax.experimental.pallas import tpu as pltpu

D_MODEL = 1024
HEAD_DIM = 64
SB_HEADS = 8
RWKV_HEADS = 8
SB_WIDTH = SB_HEADS * HEAD_DIM
RWKV_WIDTH = RWKV_HEADS * HEAD_DIM
MIX_WIDTH = SB_WIDTH + RWKV_WIDTH
DECAY_LORA = 64
AAA_LORA = 64
GATE_LORA = 160
RWKV_IN = 3 * RWKV_WIDTH + DECAY_LORA + AAA_LORA + GATE_LORA
MEM_HEADS = 4
MEM_HEAD_DIM = 128
MEM_WIDTH = MEM_HEADS * MEM_HEAD_DIM
D_FF = 2816
NORM_EPS = 1e-6
LNX_EPS = 64e-5

LANES = 128
MXU_DIM = 256
V7X_VMEM_BYTES = 64 * 1024 * 1024
VMEM_LIMIT_BYTES = V7X_VMEM_BYTES - 8 * 1024 * 1024

RWKV_PAD = 15 * LANES
GATE_PAD = RWKV_PAD - (3 * RWKV_WIDTH + LANES)
SB_TILE = 128
SB_QROWS = 512
SB_CARRY_STOP = 112.0
LOG2E = 1.4426950408889634
RWKV_CHUNK = 64
RWKV_ROWS = 256
RWKV_QUAD = MXU_DIM
TOKEN_TILE = 512
FF_CHUNK = 256

BF16 = jnp.bfloat16
F32 = jnp.float32


def _const_spec(shape):
    nd = len(shape)
    return pl.BlockSpec(shape, lambda *_: (0,) * nd, pipeline_mode=pl.Buffered(1))


def _params(*sem):
    return pltpu.CompilerParams(dimension_semantics=sem, vmem_limit_bytes=VMEM_LIMIT_BYTES)


def _rms(x, g):
    ms = jnp.mean(x * x, axis=-1, keepdims=True)
    return x * lax.rsqrt(ms + NORM_EPS) * g


def _dot(a, b):
    return jnp.dot(a, b, preferred_element_type=F32)


def _dot_nt(a, b):
    return lax.dot_general(a, b, (((1,), (1,)), ((), ())), preferred_element_type=F32)


def _dot_tn(a, b):
    return lax.dot_general(a, b, (((0,), (0,)), ((), ())), preferred_element_type=F32)


def _split2(x):
    hi = x.astype(BF16)
    lo = (x - hi.astype(F32)).astype(BF16)
    return hi, lo


def _split3(x):
    hi = x.astype(BF16)
    r1 = x - hi.astype(F32)
    mid = r1.astype(BF16)
    lo = (r1 - mid.astype(F32)).astype(BF16)
    return hi, mid, lo


def _ffn_kernel(x_ref, pre_ref, post_ref, win_ref, wout_ref, o_ref, act_ref):
    x = x_ref[...]
    xn = _rms(x, pre_ref[...]).astype(BF16)
    for c in range(D_FF // FF_CHUNK):
        lo = c * FF_CHUNK
        gate = _dot(xn, win_ref[:, lo:lo + FF_CHUNK])
        up = _dot(xn, win_ref[:, D_FF + lo:D_FF + lo + FF_CHUNK])
        act_ref[:, lo:lo + FF_CHUNK] = (gate * jax.nn.sigmoid(gate) * up).astype(BF16)
    y = _dot(act_ref[...], wout_ref[...])
    o_ref[...] = x + 0.5 * _rms(y, post_ref[...])


def _ffn(h, pre, post, w_in, w_out):
    t = h.shape[0]
    tm = min(TOKEN_TILE, t)
    row = lambda i: (i, 0)
    return pl.pallas_call(
        _ffn_kernel,
        grid=(t // tm,),
        in_specs=[
            pl.BlockSpec((tm, D_MODEL), row),
            _const_spec((1, D_MODEL)),
            _const_spec((1, D_MODEL)),
            _const_spec((D_MODEL, 2 * D_FF)),
            _const_spec((D_FF, D_MODEL)),
        ],
        out_specs=pl.BlockSpec((tm, D_MODEL), row),
        out_shape=jax.ShapeDtypeStruct((t, D_MODEL), F32),
        scratch_shapes=[pltpu.VMEM((tm, D_FF), BF16)],
        compiler_params=_params("parallel"),
        name="ffn",
    )(h, pre, post, w_in, w_out)


def _mix_in_kernel(h_ref, pre_ref, w_ref, qkv_ref, rw_ref):
    xn = _rms(h_ref[...], pre_ref[...]).astype(BF16)
    qkv_ref[...] = _dot(xn, w_ref[:, :3 * SB_WIDTH]).astype(BF16)
    rw_ref[...] = _dot(xn, w_ref[:, 3 * SB_WIDTH:])


def _mix_in(h, pre, w):
    t = h.shape[0]
    tm = min(TOKEN_TILE, t)
    row = lambda i: (i, 0)
    return pl.pallas_call(
        _mix_in_kernel,
        grid=(t // tm,),
        in_specs=[
            pl.BlockSpec((tm, D_MODEL), row),
            _const_spec((1, D_MODEL)),
            _const_spec((D_MODEL, 3 * SB_WIDTH + RWKV_PAD)),
        ],
        out_specs=[pl.BlockSpec((tm, 3 * SB_WIDTH), row), pl.BlockSpec((tm, RWKV_PAD), row)],
        out_shape=[jax.ShapeDtypeStruct((t, 3 * SB_WIDTH), BF16),
                   jax.ShapeDtypeStruct((t, RWKV_PAD), F32)],
        compiler_params=_params("parallel"),
        name="mix_in",
    )(h, pre, w)


def _sb_kernel(q_ref, k_ref, v_ref, g_ref, cum_ref, o_ref, k2_ref, v2_ref, carry_ref, acc_ref, *, seq):
    i = pl.program_id(2)
    nk = seq // SB_TILE
    tq = SB_QROWS
    sub = tq // SB_TILE
    lane = lax.broadcasted_iota(jnp.int32, (SB_TILE, LANES), 1)
    head0 = lane < HEAD_DIM

    @pl.when(i == 0)
    def _():
        for j in range(nk):
            kb = k_ref[j * SB_TILE:(j + 1) * SB_TILE, :] * jnp.asarray(HEAD_DIM ** -0.5, BF16)
            vb = v_ref[j * SB_TILE:(j + 1) * SB_TILE, :]
            zero = jnp.zeros_like(kb)
            k2_ref[j, :SB_TILE, :] = jnp.where(head0, kb, zero)
            k2_ref[j, SB_TILE:, :] = jnp.where(head0, zero, kb)
            v2_ref[j, :SB_TILE, :] = jnp.where(head0, vb, zero)
            v2_ref[j, SB_TILE:, :] = jnp.where(head0, zero, vb)

    cum = cum_ref[...]

    def tile(q, j, carry, causal):
        z = _dot_nt(q, k2_ref[j])
        e = jnp.exp2(jnp.abs(z) * (-LOG2E))
        sp = jnp.maximum(z, 0.0) + jnp.log(1.0 + e)
        lsig = z - sp
        if causal is not None:
            sp = jnp.where(causal, sp, 0.0)
        r = _dot(sp.astype(BF16), cum)
        a = jnp.exp(lsig - r[:, :2 * SB_TILE] - carry)
        if causal is not None:
            a = jnp.where(causal, a, 0.0)
        return carry + r[:, 2 * SB_TILE:], _dot(a.astype(BF16), v2_ref[j])

    carry_ref[...] = jnp.zeros_like(carry_ref)
    acc_ref[...] = jnp.zeros_like(acc_ref)
    for m in reversed(range(sub)):
        lo = m * SB_TILE
        n = tq - lo
        row = lax.broadcasted_iota(jnp.int32, (n, 2 * SB_TILE), 0)
        col = lax.broadcasted_iota(jnp.int32, (n, 2 * SB_TILE), 1) & (SB_TILE - 1)
        carry, out = tile(q_ref[lo:, :], sub * i + m, carry_ref[lo:, :], col < row)
        carry_ref[lo:, :] = carry
        acc_ref[lo:, :] += out

    q = q_ref[...]

    def cond(state):
        jj, smallest = state
        return jnp.logical_and(jj < (sub // 2) * i, smallest < SB_CARRY_STOP)

    def body(state):
        jj, _ = state
        j = sub * i - 1 - 2 * jj
        carry, out0 = tile(q, j, carry_ref[...], None)
        carry, out1 = tile(q, j - 1, carry, None)
        carry_ref[...] = carry
        acc_ref[...] += out0 + out1
        return jj + 1, jnp.min(carry)

    lax.while_loop(cond, body, (jnp.int32(0), jnp.min(carry_ref[...])))

    acc = acc_ref[...]
    h0 = lax.broadcasted_iota(jnp.int32, acc.shape, 1) < HEAD_DIM
    sq = acc * acc
    ms0 = jnp.sum(jnp.where(h0, sq, 0.0), axis=-1, keepdims=True)
    ms1 = jnp.sum(jnp.where(h0, 0.0, sq), axis=-1, keepdims=True)
    ms = jnp.where(h0, ms0, ms1) * (1.0 / HEAD_DIM)
    o_ref[...] = (acc * lax.rsqrt(ms + NORM_EPS) * g_ref[...]).astype(BF16)


def _sb_cum_matrix():
    n = 2 * SB_TILE
    j = jnp.arange(n)[:, None]
    s = jnp.arange(n)[None, :]
    same = (j // SB_TILE) == (s // SB_TILE)
    tri = same & (j > s)
    return jnp.concatenate([tri, same], axis=1).astype(BF16)


def _sb_attention(qkv, out_g, batch, seq):
    assert seq % SB_QROWS == 0 and (SB_QROWS // SB_TILE) % 2 == 0
    qkv = qkv.reshape(batch, seq, 3 * SB_WIDTH)
    pairs = SB_WIDTH // LANES
    kern = functools.partial(_sb_kernel, seq=seq)
    return pl.pallas_call(
        kern,
        grid=(batch, pairs, seq // SB_QROWS),
        in_specs=[
            pl.BlockSpec((None, SB_QROWS, LANES), lambda b, p, i: (b, i, p)),
            pl.BlockSpec((None, seq, LANES), lambda b, p, i: (b, 0, pairs + p)),
            pl.BlockSpec((None, seq, LANES), lambda b, p, i: (b, 0, 2 * pairs + p)),
            pl.BlockSpec((1, LANES), lambda b, p, i: (0, p)),
            _const_spec((2 * SB_TILE, 4 * SB_TILE)),
        ],
        out_specs=pl.BlockSpec((None, SB_QROWS, LANES), lambda b, p, i: (b, i, p)),
        out_shape=jax.ShapeDtypeStruct((batch, seq, SB_WIDTH), BF16),
        scratch_shapes=[pltpu.VMEM((seq // SB_TILE, 2 * SB_TILE, LANES), BF16),
                        pltpu.VMEM((seq // SB_TILE, 2 * SB_TILE, LANES), BF16),
                        pltpu.VMEM((SB_QROWS, 2 * SB_TILE), F32),
                        pltpu.VMEM((SB_QROWS, LANES), F32)],
        compiler_params=_params("parallel", "parallel", "arbitrary"),
        name="sb_attention",
    )(qkv, qkv, qkv, out_g, _sb_cum_matrix())


def _rwkv_kernel(rw_ref, prev_ref, mu_ref, w0_ref, w2_ref, a0_ref, a2_ref, g2_ref, kk_ref, ka_ref,
                 rk_ref, lg_ref, lb_ref, cum_ref, grp_ref, o_ref, state_s):
    c = pl.program_id(1)
    rows = RWKV_ROWS
    ch = RWKV_CHUNK
    width = RWKV_WIDTH
    quad = RWKV_QUAD
    heads = quad // HEAD_DIM

    @pl.when(c == 0)
    def _():
        state_s[...] = jnp.zeros_like(state_s)

    def gsum(x):
        hi, lo = _split2(x)
        hl = jnp.concatenate([hi, lo], axis=0)
        r = jnp.concatenate([_dot(hl[:, q0:q0 + quad], grp_ref[...]) for q0 in range(0, width, quad)],
                            axis=1)
        return r[:rows] + r[rows:]

    p = rw_ref[...]
    prev_last = jnp.where(c == 0, 0.0, prev_ref[7:8, :])
    rowi = lax.broadcasted_iota(jnp.int32, p.shape, 0)
    shifted = jnp.where(rowi == 0, prev_last, pltpu.roll(p, 1, axis=0))
    p = p + (shifted - p) * mu_ref[...]
    r = p[:, :width]
    k = p[:, width:2 * width]
    v = p[:, 2 * width:3 * width]
    xwa = p[:, 3 * width:3 * width + LANES]
    xg = p[:, 3 * width + LANES:]
    lane = lax.broadcasted_iota(jnp.int32, xwa.shape, 1)
    xw = jnp.where(lane < DECAY_LORA, jnp.tanh(xwa), 0.0).astype(BF16)
    xa = jnp.where(lane < DECAY_LORA, 0.0, xwa).astype(BF16)
    wlin = w0_ref[...] + _dot(xw, w2_ref[...])
    wlog = -jax.nn.softplus(-wlin) - 0.5
    lw = -jnp.exp(wlog)
    lr = jax.nn.sigmoid(a0_ref[...] + _dot(xa, a2_ref[...]))
    gate = _dot(jax.nn.sigmoid(xg).astype(BF16), g2_ref[...])
    kk = k * kk_ref[...]
    kk = kk / jnp.maximum(jnp.sqrt(gsum(kk * kk)), 1e-12)
    k = k * (1.0 + (lr - 1.0) * ka_ref[...])
    a_vec = -kk
    b_vec = kk * lr

    cs = _dot(cum_ref[...], jnp.concatenate(_split3(lw), axis=1))
    cs = cs[:, :width] + cs[:, width:2 * width] + cs[:, 2 * width:]
    g_in = cs[:rows]
    g_tot = cs[rows:]
    e_in = jnp.exp(g_in)
    e_inv = jnp.exp(-g_in)
    e_end = jnp.exp(g_tot - g_in)
    at_all = (a_vec * jnp.exp(g_in - lw)).astype(BF16)
    rt_all = (r * e_in).astype(BF16)
    bt_all = (b_vec * e_inv).astype(BF16)
    kt_all = (k * e_inv).astype(BF16)
    v_all = v.astype(BF16)
    bh_all = (b_vec * e_end).astype(BF16)
    kh_all = (k * e_end).astype(BF16)
    gam_all = jnp.exp(g_tot)

    t_i = lax.broadcasted_iota(jnp.int32, (ch, quad), 0)
    l_i = lax.broadcasted_iota(jnp.int32, (ch, quad), 1)
    i_i = l_i & (HEAD_DIM - 1)
    strict = i_i < t_i
    incl = i_i <= t_i
    eye = (i_i == t_i).astype(F32)
    in_head = [(l_i // HEAD_DIM) == h for h in range(heads)]
    rr = lax.broadcasted_iota(jnp.int32, (quad, quad), 0) // HEAD_DIM
    cc = lax.broadcasted_iota(jnp.int32, (quad, quad), 1) // HEAD_DIM
    same_head = rr == cc

    def bd(y):
        y = y.astype(BF16)
        zero = jnp.zeros_like(y)
        return jnp.concatenate([jnp.where(m, y, zero) for m in in_head], axis=0)

    def pp(x, y):
        return _dot(x.astype(BF16), bd(y))

    def unpack_diag(full):
        out = jnp.where(in_head[0], full[:ch], 0.0)
        for h in range(1, heads):
            out = out + jnp.where(in_head[h], full[h * ch:(h + 1) * ch], 0.0)
        return out

    nq = width // quad
    combos = [(ci, qd) for ci in range(rows // ch) for qd in range(nq)]

    def cut(x, ci, qd):
        return x[ci * ch:(ci + 1) * ch, qd * quad:(qd + 1) * quad]

    def each(fn, *lists):
        return [fn(*args) for args in zip(*lists)]

    at = [cut(at_all, *cq) for cq in combos]
    rt = [cut(rt_all, *cq) for cq in combos]
    vv = [cut(v_all, *cq) for cq in combos]
    bh = [cut(bh_all, *cq) for cq in combos]
    kh = [cut(kh_all, *cq) for cq in combos]
    aa = [_dot_nt(jnp.concatenate([a, r_], axis=0),
                  jnp.concatenate([bd(cut(bt_all, *cq)), bd(cut(kt_all, *cq))], axis=0))
          for a, r_, cq in zip(at, rt, combos)]
    n_ab = [jnp.where(strict, x[:ch, :quad], 0.0) for x in aa]
    n_ak = [jnp.where(strict, x[:ch, quad:], 0.0) for x in aa]
    a_rb = [jnp.where(incl, x[ch:, :quad], 0.0) for x in aa]
    a_rk = [jnp.where(incl, x[ch:, quad:], 0.0) for x in aa]

    blk = 8
    diag = (t_i // blk) == (i_i // blk)
    nd = [jnp.where(diag, x, 0.0) for x in n_ab]
    n2 = each(pp, nd, nd)
    n4 = each(pp, n2, n2)
    tm = [eye + x for x in nd]
    tm = each(lambda t, n: t + pp(t, n), tm, n2)
    tm = each(lambda t, n: t + pp(t, n), tm, n4)
    while blk < ch:
        off = ((t_i // (2 * blk)) == (i_i // (2 * blk))) & ((t_i // blk) != (i_i // blk))
        half = each(lambda t, n: pp(t, jnp.where(off, n, 0.0)), tm, n_ab)
        tm = each(lambda t, h: t + pp(h, t), tm, half)
        blk *= 2

    akv = each(pp, n_ak, vv)
    wu = each(lambda t, a, b: _dot(t.astype(BF16), jnp.concatenate([bd(a), bd(b)], axis=1)),
              tm, at, akv)
    wa = [x[:, :quad] for x in wu]
    uv = [x[:, quad:] for x in wu]
    ry = each(lambda a, w, u: _dot(a.astype(BF16), jnp.concatenate([bd(w), bd(u)], axis=1)),
              a_rb, wa, uv)
    rhat = each(lambda r_, x: r_.astype(F32) + x[:, :quad], rt, ry)
    yv = each(lambda x, a, v_: x[:, quad:] + pp(a, v_), ry, a_rk, vv)
    gc = each(lambda w, b: jnp.where(same_head, _dot_tn(w.astype(BF16), b), 0.0).astype(BF16), wa, bh)
    sv = each(lambda u, v_, b, k_: unpack_diag(_dot_tn(jnp.concatenate([u.astype(BF16), v_], axis=0),
                                                       jnp.concatenate([b, k_], axis=0))),
              uv, vv, bh, kh)

    states = [state_s[qd] for qd in range(nq)]
    y_out = {}
    for n, (ci, qd) in enumerate(combos):
        s0 = states[qd]
        y_out[ci, qd] = _dot_nt(rhat[n].astype(BF16), bd(s0)) + yv[n]
        gam_row = gam_all[ci * ch:ci * ch + 1, qd * quad:(qd + 1) * quad]
        states[qd] = s0 * gam_row + _dot(s0.astype(BF16), gc[n]) + sv[n]
    for qd in range(nq):
        state_s[qd] = states[qd]
    y_rows = [jnp.concatenate([y_out[ci, qd] for qd in range(nq)], axis=1)
              for ci in range(rows // ch)]

    y = jnp.concatenate(y_rows, axis=0)
    mean = gsum(y) * (1.0 / HEAD_DIM)
    d = y - mean
    var = gsum(d * d) * (1.0 / HEAD_DIM)
    y = d * lax.rsqrt(var + LNX_EPS) * lg_ref[...] + lb_ref[...]
    bonus = gsum(r * k * rk_ref[...]) * v
    o_ref[...] = ((y + bonus) * gate).astype(BF16)


def _rwkv_cum_matrix():
    t = jnp.arange(RWKV_ROWS)[:, None]
    i = jnp.arange(RWKV_ROWS)[None, :]
    same = (t // RWKV_CHUNK) == (i // RWKV_CHUNK)
    return jnp.concatenate([same & (i <= t), same], axis=0).astype(BF16)


def _group_matrix():
    a = jnp.arange(RWKV_QUAD)
    return (a[:, None] // HEAD_DIM == a[None, :] // HEAD_DIM).astype(BF16)


def _rwkv(rw, batch, seq, mu, w0, w2, a0, a2, g2, k_k, k_a, r_k, lnx_g, lnx_b):
    rw = rw.reshape(batch, seq, RWKV_PAD)
    rows = RWKV_ROWS
    vec = lambda n: _const_spec((1, n))
    return pl.pallas_call(
        _rwkv_kernel,
        grid=(batch, seq // rows),
        in_specs=[
            pl.BlockSpec((None, rows, RWKV_PAD), lambda b, c: (b, c, 0)),
            pl.BlockSpec((None, 8, RWKV_PAD), lambda b, c: (b, jnp.maximum(c * (rows // 8) - 1, 0), 0)),
            vec(RWKV_PAD), vec(RWKV_WIDTH), _const_spec((LANES, RWKV_WIDTH)), vec(RWKV_WIDTH),
            _const_spec((LANES, RWKV_WIDTH)), _const_spec((GATE_PAD, RWKV_WIDTH)),
            vec(RWKV_WIDTH), vec(RWKV_WIDTH), vec(RWKV_WIDTH), vec(RWKV_WIDTH), vec(RWKV_WIDTH),
            _const_spec((2 * rows, rows)), _const_spec((RWKV_QUAD, RWKV_QUAD)),
        ],
        out_specs=pl.BlockSpec((None, rows, RWKV_WIDTH), lambda b, c: (b, c, 0)),
        out_shape=jax.ShapeDtypeStruct((batch, seq, RWKV_WIDTH), BF16),
        scratch_shapes=[pltpu.VMEM((RWKV_WIDTH // RWKV_QUAD, RWKV_CHUNK, RWKV_QUAD), F32)],
        compiler_params=_params("parallel", "arbitrary"),
        name="rwkv7",
    )(rw, rw, mu, w0, w2, a0, a2, g2, k_k, k_a, r_k, lnx_g, lnx_b,
      _rwkv_cum_matrix(), _group_matrix())


def _mix_out_kernel(sb_ref, rw_ref, h_ref, w_ref, post_ref, o_ref):
    mixed = _dot(sb_ref[...], w_ref[:SB_WIDTH, :]) + _dot(rw_ref[...], w_ref[SB_WIDTH:, :])
    o_ref[...] = h_ref[...] + _rms(mixed, post_ref[...])


def _mix_out(sb_o, rw_o, h, w, post):
    t = h.shape[0]
    tm = min(TOKEN_TILE, t)
    row = lambda i: (i, 0)
    return pl.pallas_call(
        _mix_out_kernel,
        grid=(t // tm,),
        in_specs=[
            pl.BlockSpec((tm, SB_WIDTH), row),
            pl.BlockSpec((tm, RWKV_WIDTH), row),
            pl.BlockSpec((tm, D_MODEL), row),
            _const_spec((MIX_WIDTH, D_MODEL)),
            _const_spec((1, D_MODEL)),
        ],
        out_specs=pl.BlockSpec((tm, D_MODEL), row),
        out_shape=jax.ShapeDtypeStruct((t, D_MODEL), F32),
        compiler_params=_params("parallel"),
        name="mix_out",
    )(sb_o, rw_o, h, w, post)


def _mem_kv_kernel(m_ref, g_ref, w_ref, o_ref):
    o_ref[...] = _dot(_rms(m_ref[...], g_ref[...]).astype(BF16), w_ref[...]).astype(BF16)


def _mem_kv(mem, g, w):
    t = mem.shape[0]
    tm = min(TOKEN_TILE, t)
    row = lambda i: (i, 0)
    return pl.pallas_call(
        _mem_kv_kernel,
        grid=(t // tm,),
        in_specs=[pl.BlockSpec((tm, D_MODEL), row), _const_spec((1, D_MODEL)),
                  _const_spec((D_MODEL, 2 * MEM_WIDTH))],
        out_specs=pl.BlockSpec((tm, 2 * MEM_WIDTH), row),
        out_shape=jax.ShapeDtypeStruct((t, 2 * MEM_WIDTH), BF16),
        compiler_params=_params("parallel"),
        name="mem_kv",
    )(mem, g, w)


def _mem_attn_kernel(h_ref, kv_ref, pre_ref, wq_ref, wo_ref, post_ref, o_ref):
    h = h_ref[...]
    q = _dot(_rms(h, pre_ref[...]).astype(BF16), wq_ref[...])
    q = (q * (MEM_HEAD_DIM ** -0.5)).astype(BF16)
    outs = []
    for hd in range(MEM_HEADS):
        lo = hd * MEM_HEAD_DIM
        s = _dot_nt(q[:, lo:lo + MEM_HEAD_DIM], kv_ref[:, lo:lo + MEM_HEAD_DIM])
        e = jnp.exp(s - jnp.max(s, axis=-1, keepdims=True))
        pr = e / jnp.sum(e, axis=-1, keepdims=True)
        outs.append(_dot(pr.astype(BF16), kv_ref[:, MEM_WIDTH + lo:MEM_WIDTH + lo + MEM_HEAD_DIM]))
    o = jnp.concatenate(outs, axis=1).astype(BF16)
    o_ref[...] = h + _rms(_dot(o, wo_ref[...]), post_ref[...])


def _mem_attn(h, kv, pre, w_q, w_o, post, batch, seq):
    mem_len = kv.shape[0] // batch
    kv = kv.reshape(batch, mem_len, 2 * MEM_WIDTH)
    h = h.reshape(batch, seq, D_MODEL)
    tm = min(TOKEN_TILE, seq)
    out = pl.pallas_call(
        _mem_attn_kernel,
        grid=(batch, seq // tm),
        in_specs=[
            pl.BlockSpec((None, tm, D_MODEL), lambda b, i: (b, i, 0)),
            pl.BlockSpec((None, mem_len, 2 * MEM_WIDTH), lambda b, i: (b, 0, 0)),
            _const_spec((1, D_MODEL)),
            _const_spec((D_MODEL, MEM_WIDTH)),
            _const_spec((MEM_WIDTH, D_MODEL)),
            _const_spec((1, D_MODEL)),
        ],
        out_specs=pl.BlockSpec((None, tm, D_MODEL), lambda b, i: (b, i, 0)),
        out_shape=jax.ShapeDtypeStruct((batch, seq, D_MODEL), F32),
        compiler_params=_params("parallel", "parallel"),
        name="mem_attn",
    )(h, kv, pre, w_q, w_o, post)
    return out.reshape(batch * seq, D_MODEL)


def _row(a):
    return a.reshape(1, -1).astype(F32)


def _pad_rows(w, n):
    return jnp.pad(w, ((0, n - w.shape[0]), (0, 0)))


def kernel(x, mem, ffn1_pre, ffn1_post, ffn1_w_in, ffn1_w_out, mix_pre, mix_post, mix_w_in, rwkv_mu, rwkv_w0, rwkv_w2, rwkv_a0, rwkv_a2, rwkv_g2, rwkv_k_k, rwkv_k_a, rwkv_r_k, rwkv_lnx_g, rwkv_lnx_b, sb_out_g, mix_w_out, mem_pre, mem_post, mem_kv_g, mem_w_q, mem_w_kv, mem_w_o, ffn2_pre, ffn2_post, ffn2_w_in, ffn2_w_out):
    batch, seq, _ = x.shape
    depth = ffn1_pre.shape[0]
    h = x.reshape(batch * seq, D_MODEL)
    mem2 = mem.reshape(-1, D_MODEL)
    pad_cols = RWKV_PAD - RWKV_IN
    for l in range(depth):
        h = _ffn(h, _row(ffn1_pre[l]), _row(ffn1_post[l]),
                 ffn1_w_in[l].astype(BF16), ffn1_w_out[l].astype(BF16))

        w_in = jnp.pad(mix_w_in[l], ((0, 0), (0, pad_cols))).astype(BF16)
        qkv, rw = _mix_in(h, _row(mix_pre[l]), w_in)
        sb_o = _sb_attention(qkv, _row(sb_out_g[l]), batch, seq)
        w2 = _pad_rows(rwkv_w2[l], LANES).astype(BF16)
        a2 = jnp.pad(rwkv_a2[l], ((DECAY_LORA, 0), (0, 0))).astype(BF16)
        g2 = _pad_rows(rwkv_g2[l], GATE_PAD).astype(BF16)
        mu = jnp.pad(rwkv_mu[l], (0, pad_cols))
        rw_o = _rwkv(rw, batch, seq, _row(mu), _row(rwkv_w0[l]), w2, _row(rwkv_a0[l]), a2, g2,
                     _row(rwkv_k_k[l]), _row(rwkv_k_a[l]), _row(rwkv_r_k[l]),
                     _row(rwkv_lnx_g[l]), _row(rwkv_lnx_b[l]))
        h = _mix_out(sb_o.reshape(batch * seq, SB_WIDTH), rw_o.reshape(batch * seq, RWKV_WIDTH), h,
                     mix_w_out[l].astype(BF16), _row(mix_post[l]))

        kv = _mem_kv(mem2, _row(mem_kv_g[l]), mem_w_kv[l].astype(BF16))
        h = _mem_attn(h, kv, _row(mem_pre[l]), mem_w_q[l].astype(BF16), mem_w_o[l].astype(BF16),
                      _row(mem_post[l]), batch, seq)

        h = _ffn(h, _row(ffn2_pre[l]), _row(ffn2_post[l]),
                 ffn2_w_in[l].astype(BF16), ffn2_w_out[l].astype(BF16))
    return h.reshape(batch, seq, D_MODEL)
```

```python
import functools

import jax
import jax.numpy as jnp
from jax import lax
from jax.experimental import pallas as pl
from jax.experimental.pallas import tpu as pltpu

D_MODEL = 1024
HEAD_DIM = 64
SB_HEADS = 8
RWKV_HEADS = 8
SB_WIDTH = SB_HEADS * HEAD_DIM
RWKV_WIDTH = RWKV_HEADS * HEAD_DIM
MIX_WIDTH = SB_WIDTH + RWKV_WIDTH
DECAY_LORA = 64
AAA_LORA = 64
GATE_LORA = 160
RWKV_IN = 3 * RWKV_WIDTH + DECAY_LORA + AAA_LORA + GATE_LORA
MEM_HEADS = 4
MEM_HEAD_DIM = 128
MEM_WIDTH = MEM_HEADS * MEM_HEAD_DIM
D_FF = 2816
NORM_EPS = 1e-6
LNX_EPS = 64e-5

LANES = 128
MXU_DIM = 256
V7X_VMEM_BYTES = 64 * 1024 * 1024
VMEM_LIMIT_BYTES = V7X_VMEM_BYTES - 8 * 1024 * 1024

RWKV_PAD = 15 * LANES
GATE_PAD = RWKV_PAD - (3 * RWKV_WIDTH + LANES)
SB_TILE = 128
SB_QROWS = 512
SB_CARRY_STOP = 112.0
LOG2E = 1.4426950408889634
RWKV_CHUNK = 64
RWKV_ROWS = 256
RWKV_QUAD = MXU_DIM
TOKEN_TILE = 512
FF_CHUNK = 256

BF16 = jnp.bfloat16
F32 = jnp.float32


def _const_spec(shape):
    nd = len(shape)
    return pl.BlockSpec(shape, lambda *_: (0,) * nd, pipeline_mode=pl.Buffered(1))


def _params(*sem):
    return pltpu.CompilerParams(dimension_semantics=sem, vmem_limit_bytes=VMEM_LIMIT_BYTES)


def _rms(x, g):
    ms = jnp.mean(x * x, axis=-1, keepdims=True)
    return x * lax.rsqrt(ms + NORM_EPS) * g


def _dot(a, b):
    return jnp.dot(a, b, preferred_element_type=F32)


def _dot_nt(a, b):
    return lax.dot_general(a, b, (((1,), (1,)), ((), ())), preferred_element_type=F32)


def _dot_tn(a, b):
    return lax.dot_general(a, b, (((0,), (0,)), ((), ())), preferred_element_type=F32)


def _split2(x):
    hi = x.astype(BF16)
    lo = (x - hi.astype(F32)).astype(BF16)
    return hi, lo


def _ffn_body(x, pre_ref, post_ref, win_ref, wout_ref, act_ref):
    xn = _rms(x, pre_ref[...]).astype(BF16)
    for c in range(D_FF // FF_CHUNK):
        lo = c * FF_CHUNK
        gate = _dot(xn, win_ref[:, lo:lo + FF_CHUNK])
        up = _dot(xn, win_ref[:, D_FF + lo:D_FF + lo + FF_CHUNK])
        act_ref[:, lo:lo + FF_CHUNK] = (gate * jax.nn.sigmoid(gate) * up).astype(BF16)
    y = _dot(act_ref[...], wout_ref[...])
    return x + 0.5 * _rms(y, post_ref[...])


def _pre_mix_kernel(x_ref, fpre_ref, fpost_ref, win_ref, wout_ref, mpre_ref, wmix_ref,
                    h_ref, qkv_ref, rw_ref, act_ref):
    h = _ffn_body(x_ref[...], fpre_ref, fpost_ref, win_ref, wout_ref, act_ref)
    h_ref[...] = h
    hn = _rms(h, mpre_ref[...]).astype(BF16)
    qkv_ref[...] = _dot(hn, wmix_ref[:, :3 * SB_WIDTH]).astype(BF16)
    rw_ref[...] = _dot(hn, wmix_ref[:, 3 * SB_WIDTH:])


def _pre_mix(x, fpre, fpost, w_in, w_out, mpre, w_mix):
    t = x.shape[0]
    tm = min(TOKEN_TILE, t)
    row = lambda i: (i, 0)
    vec = _const_spec((1, D_MODEL))
    return pl.pallas_call(
        _pre_mix_kernel,
        grid=(t // tm,),
        in_specs=[
            pl.BlockSpec((tm, D_MODEL), row), vec, vec,
            _const_spec((D_MODEL, 2 * D_FF)), _const_spec((D_FF, D_MODEL)),
            vec, _const_spec((D_MODEL, 3 * SB_WIDTH + RWKV_PAD)),
        ],
        out_specs=[pl.BlockSpec((tm, D_MODEL), row), pl.BlockSpec((tm, 3 * SB_WIDTH), row),
                   pl.BlockSpec((tm, RWKV_PAD), row)],
        out_shape=[jax.ShapeDtypeStruct((t, D_MODEL), F32),
                   jax.ShapeDtypeStruct((t, 3 * SB_WIDTH), BF16),
                   jax.ShapeDtypeStruct((t, RWKV_PAD), F32)],
        scratch_shapes=[pltpu.VMEM((tm, D_FF), BF16)],
        compiler_params=_params("parallel"),
        name="ffn1_mix_in",
    )(x, fpre, fpost, w_in, w_out, mpre, w_mix)


def _post_mix_kernel(sb_ref, rw_ref, h_ref, kv_ref, wmix_ref, mixpost_ref, mempre_ref, wq_ref, wo_ref,
                     mempost_ref, fpre_ref, fpost_ref, win_ref, wout_ref, o_ref, act_ref):
    mixed = _dot(sb_ref[...], wmix_ref[:SB_WIDTH, :]) + _dot(rw_ref[...], wmix_ref[SB_WIDTH:, :])
    h = h_ref[...] + _rms(mixed, mixpost_ref[...])

    q = _dot(_rms(h, mempre_ref[...]).astype(BF16), wq_ref[...])
    q = (q * (MEM_HEAD_DIM ** -0.5)).astype(BF16)
    outs = []
    for hd in range(MEM_HEADS):
        lo = hd * MEM_HEAD_DIM
        s = _dot_nt(q[:, lo:lo + MEM_HEAD_DIM], kv_ref[:, lo:lo + MEM_HEAD_DIM])
        e = jnp.exp(s - jnp.max(s, axis=-1, keepdims=True))
        pr = e / jnp.sum(e, axis=-1, keepdims=True)
        outs.append(_dot(pr.astype(BF16), kv_ref[:, MEM_WIDTH + lo:MEM_WIDTH + lo + MEM_HEAD_DIM]))
    o = jnp.concatenate(outs, axis=1).astype(BF16)
    h = h + _rms(_dot(o, wo_ref[...]), mempost_ref[...])

    o_ref[...] = _ffn_body(h, fpre_ref, fpost_ref, win_ref, wout_ref, act_ref)


def _post_mix(sb_o, rw_o, h, kv, w_mix, mix_post, mem_pre, w_q, w_o, mem_post, fpre, fpost, w_in, w_out,
              batch, seq):
    mem_len = kv.shape[0] // batch
    kv = kv.reshape(batch, mem_len, 2 * MEM_WIDTH)
    h = h.reshape(batch, seq, D_MODEL)
    tm = min(TOKEN_TILE, seq)
    tile = lambda width: pl.BlockSpec((None, tm, width), lambda b, i: (b, i, 0))
    vec = _const_spec((1, D_MODEL))
    out = pl.pallas_call(
        _post_mix_kernel,
        grid=(batch, seq // tm),
        in_specs=[
            tile(SB_WIDTH), tile(RWKV_WIDTH), tile(D_MODEL),
            pl.BlockSpec((None, mem_len, 2 * MEM_WIDTH), lambda b, i: (b, 0, 0)),
            _const_spec((MIX_WIDTH, D_MODEL)), vec, vec,
            _const_spec((D_MODEL, MEM_WIDTH)), _const_spec((MEM_WIDTH, D_MODEL)), vec,
            vec, vec, _const_spec((D_MODEL, 2 * D_FF)), _const_spec((D_FF, D_MODEL)),
        ],
        out_specs=tile(D_MODEL),
        out_shape=jax.ShapeDtypeStruct((batch, seq, D_MODEL), F32),
        scratch_shapes=[pltpu.VMEM((tm, D_FF), BF16)],
        compiler_params=_params("parallel", "parallel"),
        name="mix_out_mem_ffn2",
    )(sb_o, rw_o, h, kv, w_mix, mix_post, mem_pre, w_q, w_o, mem_post, fpre, fpost, w_in, w_out)
    return out.reshape(batch * seq, D_MODEL)


def _mem_kv_kernel(m_ref, g_ref, w_ref, o_ref):
    o_ref[...] = _dot(_rms(m_ref[...], g_ref[...]).astype(BF16), w_ref[...]).astype(BF16)


def _mem_kv(mem, g, w):
    t = mem.shape[0]
    tm = min(TOKEN_TILE, t)
    row = lambda i: (i, 0)
    return pl.pallas_call(
        _mem_kv_kernel,
        grid=(t // tm,),
        in_specs=[pl.BlockSpec((tm, D_MODEL), row), _const_spec((1, D_MODEL)),
                  _const_spec((D_MODEL, 2 * MEM_WIDTH))],
        out_specs=pl.BlockSpec((tm, 2 * MEM_WIDTH), row),
        out_shape=jax.ShapeDtypeStruct((t, 2 * MEM_WIDTH), BF16),
        compiler_params=_params("parallel"),
        name="mem_kv",
    )(mem, g, w)


def _sb_kernel(q_ref, k_ref, v_ref, g_ref, cum_ref, o_ref, k2_ref, v2_ref, carry_ref, acc_ref, *, seq):
    i = pl.program_id(2)
    nk = seq // SB_TILE
    tq = SB_QROWS
    sub = tq // SB_TILE
    lane = lax.broadcasted_iota(jnp.int32, (SB_TILE, LANES), 1)
    head0 = lane < HEAD_DIM

    @pl.when(i == 0)
    def _():
        for j in range(nk):
            kb = k_ref[j * SB_TILE:(j + 1) * SB_TILE, :] * jnp.asarray(HEAD_DIM ** -0.5, BF16)
            vb = v_ref[j * SB_TILE:(j + 1) * SB_TILE, :]
            zero = jnp.zeros_like(kb)
            k2_ref[j, :SB_TILE, :] = jnp.where(head0, kb, zero)
            k2_ref[j, SB_TILE:, :] = jnp.where(head0, zero, kb)
            v2_ref[j, :SB_TILE, :] = jnp.where(head0, vb, zero)
            v2_ref[j, SB_TILE:, :] = jnp.where(head0, zero, vb)

    cum = cum_ref[...]

    def tile(q, j, carry, causal):
        z = _dot_nt(q, k2_ref[j])
        e = jnp.exp2(jnp.abs(z) * (-LOG2E))
        sp = jnp.maximum(z, 0.0) + jnp.log(1.0 + e)
        lsig = z - sp
        if causal is not None:
            sp = jnp.where(causal, sp, 0.0)
        r = _dot(sp.astype(BF16), cum)
        a = jnp.exp(lsig - r[:, :2 * SB_TILE] - carry)
        if causal is not None:
            a = jnp.where(causal, a, 0.0)
        return carry + r[:, 2 * SB_TILE:], _dot(a.astype(BF16), v2_ref[j])

    carry_ref[...] = jnp.zeros_like(carry_ref)
    acc_ref[...] = jnp.zeros_like(acc_ref)
    for m in reversed(range(sub)):
        lo = m * SB_TILE
        n = tq - lo
        row = lax.broadcasted_iota(jnp.int32, (n, 2 * SB_TILE), 0)
        col = lax.broadcasted_iota(jnp.int32, (n, 2 * SB_TILE), 1) & (SB_TILE - 1)
        carry, out = tile(q_ref[lo:, :], sub * i + m, carry_ref[lo:, :], col < row)
        carry_ref[lo:, :] = carry
        acc_ref[lo:, :] += out

    q = q_ref[...]

    def cond(state):
        jj, smallest = state
        return jnp.logical_and(jj < (sub // 2) * i, smallest < SB_CARRY_STOP)

    def body(state):
        jj, _ = state
        j = sub * i - 1 - 2 * jj
        carry, out0 = tile(q, j, carry_ref[...], None)
        carry, out1 = tile(q, j - 1, carry, None)
        carry_ref[...] = carry
        acc_ref[...] += out0 + out1
        return jj + 1, jnp.min(carry)

    lax.while_loop(cond, body, (jnp.int32(0), jnp.min(carry_ref[...])))

    acc = acc_ref[...]
    h0 = lax.broadcasted_iota(jnp.int32, acc.shape, 1) < HEAD_DIM
    sq = acc * acc
    ms0 = jnp.sum(jnp.where(h0, sq, 0.0), axis=-1, keepdims=True)
    ms1 = jnp.sum(jnp.where(h0, 0.0, sq), axis=-1, keepdims=True)
    ms = jnp.where(h0, ms0, ms1) * (1.0 / HEAD_DIM)
    o_ref[...] = (acc * lax.rsqrt(ms + NORM_EPS) * g_ref[...]).astype(BF16)


def _sb_cum_matrix():
    n = 2 * SB_TILE
    j = jnp.arange(n)[:, None]
    s = jnp.arange(n)[None, :]
    same = (j // SB_TILE) == (s // SB_TILE)
    tri = same & (j > s)
    return jnp.concatenate([tri, same], axis=1).astype(BF16)


def _sb_attention(qkv, out_g, batch, seq):
    assert seq % SB_QROWS == 0 and (SB_QROWS // SB_TILE) % 2 == 0
    qkv = qkv.reshape(batch, seq, 3 * SB_WIDTH)
    pairs = SB_WIDTH // LANES
    kern = functools.partial(_sb_kernel, seq=seq)
    return pl.pallas_call(
        kern,
        grid=(batch, pairs, seq // SB_QROWS),
        in_specs=[
            pl.BlockSpec((None, SB_QROWS, LANES), lambda b, p, i: (b, i, p)),
            pl.BlockSpec((None, seq, LANES), lambda b, p, i: (b, 0, pairs + p)),
            pl.BlockSpec((None, seq, LANES), lambda b, p, i: (b, 0, 2 * pairs + p)),
            pl.BlockSpec((1, LANES), lambda b, p, i: (0, p)),
            _const_spec((2 * SB_TILE, 4 * SB_TILE)),
        ],
        out_specs=pl.BlockSpec((None, SB_QROWS, LANES), lambda b, p, i: (b, i, p)),
        out_shape=jax.ShapeDtypeStruct((batch, seq, SB_WIDTH), BF16),
        scratch_shapes=[pltpu.VMEM((seq // SB_TILE, 2 * SB_TILE, LANES), BF16),
                        pltpu.VMEM((seq // SB_TILE, 2 * SB_TILE, LANES), BF16),
                        pltpu.VMEM((SB_QROWS, 2 * SB_TILE), F32),
                        pltpu.VMEM((SB_QROWS, LANES), F32)],
        compiler_params=_params("parallel", "parallel", "arbitrary"),
        name="sb_attention",
    )(qkv, qkv, qkv, out_g, _sb_cum_matrix())


def _rwkv_kernel(rw_ref, prev_ref, mu_ref, w0_ref, w2_ref, a0_ref, a2_ref, g2_ref, kk_ref, ka_ref,
                 rk_ref, lg_ref, lb_ref, cum_ref, grp_ref, o_ref, state_s):
    c = pl.program_id(1)
    rows = RWKV_ROWS
    ch = RWKV_CHUNK
    width = RWKV_WIDTH
    quad = RWKV_QUAD
    heads = quad // HEAD_DIM

    @pl.when(c == 0)
    def _():
        state_s[...] = jnp.zeros_like(state_s)

    def gsum(x):
        hi, lo = _split2(x)
        hl = jnp.concatenate([hi, lo], axis=0)
        r = jnp.concatenate([_dot(hl[:, q0:q0 + quad], grp_ref[...]) for q0 in range(0, width, quad)],
                            axis=1)
        return r[:rows] + r[rows:]

    p = rw_ref[...]
    prev_last = jnp.where(c == 0, 0.0, prev_ref[7:8, :])
    rowi = lax.broadcasted_iota(jnp.int32, p.shape, 0)
    shifted = jnp.where(rowi == 0, prev_last, pltpu.roll(p, 1, axis=0))
    p = p + (shifted - p) * mu_ref[...]
    r = p[:, :width]
    k = p[:, width:2 * width]
    v = p[:, 2 * width:3 * width]
    xwa = p[:, 3 * width:3 * width + LANES]
    xg = p[:, 3 * width + LANES:]
    lane = lax.broadcasted_iota(jnp.int32, xwa.shape, 1)
    xw = jnp.where(lane < DECAY_LORA, jnp.tanh(xwa), 0.0).astype(BF16)
    xa = jnp.where(lane < DECAY_LORA, 0.0, xwa).astype(BF16)
    wlin = w0_ref[...] + _dot(xw, w2_ref[...])
    wlog = -jax.nn.softplus(-wlin) - 0.5
    lw = -jnp.exp(wlog)
    lr = jax.nn.sigmoid(a0_ref[...] + _dot(xa, a2_ref[...]))
    gate = _dot(jax.nn.sigmoid(xg).astype(BF16), g2_ref[...])
    kk = k * kk_ref[...]
    kk = kk * lax.rsqrt(jnp.maximum(gsum(kk * kk), 1e-24))
    k = k * (1.0 + (lr - 1.0) * ka_ref[...])
    a_vec = -kk
    b_vec = kk * lr

    cs = _dot(cum_ref[...], jnp.concatenate(_split2(lw), axis=1))
    cs = cs[:, :width] + cs[:, width:]
    g_in = cs[:rows]
    g_tot = cs[rows:]
    e_in = jnp.exp(g_in)
    e_inv = jnp.exp(-g_in)
    e_end = jnp.exp(g_tot - g_in)
    at_all = (a_vec * jnp.exp(g_in - lw)).astype(BF16)
    rt_all = (r * e_in).astype(BF16)
    bt_all = (b_vec * e_inv).astype(BF16)
    kt_all = (k * e_inv).astype(BF16)
    v_all = v.astype(BF16)
    bh_all = (b_vec * e_end).astype(BF16)
    kh_all = (k * e_end).astype(BF16)
    gam_all = jnp.exp(g_tot)

    t_i = lax.broadcasted_iota(jnp.int32, (ch, quad), 0)
    l_i = lax.broadcasted_iota(jnp.int32, (ch, quad), 1)
    i_i = l_i & (HEAD_DIM - 1)
    strict = i_i < t_i
    incl = i_i <= t_i
    eye = (i_i == t_i).astype(F32)
    in_head = [(l_i // HEAD_DIM) == h for h in range(heads)]
    rr = lax.broadcasted_iota(jnp.int32, (quad, quad), 0) // HEAD_DIM
    cc = lax.broadcasted_iota(jnp.int32, (quad, quad), 1) // HEAD_DIM
    same_head = rr == cc

    def bd(y):
        y = y.astype(BF16)
        zero = jnp.zeros_like(y)
        return jnp.concatenate([jnp.where(m, y, zero) for m in in_head], axis=0)

    def pp(x, y):
        return _dot(x.astype(BF16), bd(y))

    def unpack_diag(full):
        out = jnp.where(in_head[0], full[:ch], 0.0)
        for h in range(1, heads):
            out = out + jnp.where(in_head[h], full[h * ch:(h + 1) * ch], 0.0)
        return out

    nq = width // quad
    combos = [(ci, qd) for ci in range(rows // ch) for qd in range(nq)]

    def cut(x, ci, qd):
        return x[ci * ch:(ci + 1) * ch, qd * quad:(qd + 1) * quad]

    def each(fn, *lists):
        return [fn(*args) for args in zip(*lists)]

    at = [cut(at_all, *cq) for cq in combos]
    rt = [cut(rt_all, *cq) for cq in combos]
    vv = [cut(v_all, *cq) for cq in combos]
    bh = [cut(bh_all, *cq) for cq in combos]
    kh = [cut(kh_all, *cq) for cq in combos]
    aa = [_dot_nt(jnp.concatenate([a, r_], axis=0),
                  jnp.concatenate([bd(cut(bt_all, *cq)), bd(cut(kt_all, *cq))], axis=0))
          for a, r_, cq in zip(at, rt, combos)]
    n_ab = [jnp.where(strict, x[:ch, :quad], 0.0) for x in aa]
    n_ak = [jnp.where(strict, x[:ch, quad:], 0.0) for x in aa]
    a_rb = [jnp.where(incl, x[ch:, :quad], 0.0) for x in aa]
    a_rk = [jnp.where(incl, x[ch:, quad:], 0.0) for x in aa]

    blk = 8
    diag = (t_i // blk) == (i_i // blk)
    nd = [jnp.where(diag, x, 0.0) for x in n_ab]
    n2 = each(pp, nd, nd)
    n4 = each(pp, n2, n2)
    tm = [eye + x for x in nd]
    tm = each(lambda t, n: t + pp(t, n), tm, n2)
    tm = each(lambda t, n: t + pp(t, n), tm, n4)
    while blk < ch:
        off = ((t_i // (2 * blk)) == (i_i // (2 * blk))) & ((t_i // blk) != (i_i // blk))
        half = each(lambda t, n: pp(t, jnp.where(off, n, 0.0)), tm, n_ab)
        tm = each(lambda t, h: t + pp(h, t), tm, half)
        blk *= 2

    akv = each(pp, n_ak, vv)
    wu = each(lambda t, a, b: _dot(t.astype(BF16), jnp.concatenate([bd(a), bd(b)], axis=1)),
              tm, at, akv)
    wa = [x[:, :quad] for x in wu]
    uv = [x[:, quad:] for x in wu]
    ry = each(lambda a, w, u: _dot(a.astype(BF16), jnp.concatenate([bd(w), bd(u)], axis=1)),
              a_rb, wa, uv)
    rhat = each(lambda r_, x: r_.astype(F32) + x[:, :quad], rt, ry)
    yv = each(lambda x, a, v_: x[:, quad:] + pp(a, v_), ry, a_rk, vv)
    gc = each(lambda w, b: jnp.where(same_head, _dot_tn(w.astype(BF16), b), 0.0).astype(BF16), wa, bh)
    sv = each(lambda u, v_, b, k_: unpack_diag(_dot_tn(jnp.concatenate([u.astype(BF16), v_], axis=0),
                                                       jnp.concatenate([b, k_], axis=0))),
              uv, vv, bh, kh)

    states = [state_s[qd] for qd in range(nq)]
    y_out = {}
    for n, (ci, qd) in enumerate(combos):
        s0 = states[qd]
        y_out[ci, qd] = _dot_nt(rhat[n].astype(BF16), bd(s0)) + yv[n]
        gam_row = gam_all[ci * ch:ci * ch + 1, qd * quad:(qd + 1) * quad]
        states[qd] = s0 * gam_row + _dot(s0.astype(BF16), gc[n]) + sv[n]
    for qd in range(nq):
        state_s[qd] = states[qd]
    y_rows = [jnp.concatenate([y_out[ci, qd] for qd in range(nq)], axis=1)
              for ci in range(rows // ch)]

    y = jnp.concatenate(y_rows, axis=0)
    mean = gsum(y) * (1.0 / HEAD_DIM)
    d = y - mean
    var = gsum(d * d) * (1.0 / HEAD_DIM)
    y = d * lax.rsqrt(var + LNX_EPS) * lg_ref[...] + lb_ref[...]
    bonus = gsum(r * k * rk_ref[...]) * v
    o_ref[...] = ((y + bonus) * gate).astype(BF16)


def _rwkv_cum_matrix():
    t = jnp.arange(RWKV_ROWS)[:, None]
    i = jnp.arange(RWKV_ROWS)[None, :]
    same = (t // RWKV_CHUNK) == (i // RWKV_CHUNK)
    return jnp.concatenate([same & (i <= t), same], axis=0).astype(BF16)


def _group_matrix():
    a = jnp.arange(RWKV_QUAD)
    return (a[:, None] // HEAD_DIM == a[None, :] // HEAD_DIM).astype(BF16)


def _rwkv(rw, batch, seq, mu, w0, w2, a0, a2, g2, k_k, k_a, r_k, lnx_g, lnx_b):
    rw = rw.reshape(batch, seq, RWKV_PAD)
    rows = RWKV_ROWS
    vec = lambda n: _const_spec((1, n))
    return pl.pallas_call(
        _rwkv_kernel,
        grid=(batch, seq // rows),
        in_specs=[
            pl.BlockSpec((None, rows, RWKV_PAD), lambda b, c: (b, c, 0)),
            pl.BlockSpec((None, 8, RWKV_PAD), lambda b, c: (b, jnp.maximum(c * (rows // 8) - 1, 0), 0)),
            vec(RWKV_PAD), vec(RWKV_WIDTH), _const_spec((LANES, RWKV_WIDTH)), vec(RWKV_WIDTH),
            _const_spec((LANES, RWKV_WIDTH)), _const_spec((GATE_PAD, RWKV_WIDTH)),
            vec(RWKV_WIDTH), vec(RWKV_WIDTH), vec(RWKV_WIDTH), vec(RWKV_WIDTH), vec(RWKV_WIDTH),
            _const_spec((2 * rows, rows)), _const_spec((RWKV_QUAD, RWKV_QUAD)),
        ],
        out_specs=pl.BlockSpec((None, rows, RWKV_WIDTH), lambda b, c: (b, c, 0)),
        out_shape=jax.ShapeDtypeStruct((batch, seq, RWKV_WIDTH), BF16),
        scratch_shapes=[pltpu.VMEM((RWKV_WIDTH // RWKV_QUAD, RWKV_CHUNK, RWKV_QUAD), F32)],
        compiler_params=_params("parallel", "arbitrary"),
        name="rwkv7",
    )(rw, rw, mu, w0, w2, a0, a2, g2, k_k, k_a, r_k, lnx_g, lnx_b,
      _rwkv_cum_matrix(), _group_matrix())


def _row(a):
    return a.reshape(1, -1).astype(F32)


def _pad_rows(w, n):
    return jnp.pad(w, ((0, n - w.shape[0]), (0, 0)))


def kernel(x, mem, ffn1_pre, ffn1_post, ffn1_w_in, ffn1_w_out, mix_pre, mix_post, mix_w_in, rwkv_mu, rwkv_w0, rwkv_w2, rwkv_a0, rwkv_a2, rwkv_g2, rwkv_k_k, rwkv_k_a, rwkv_r_k, rwkv_lnx_g, rwkv_lnx_b, sb_out_g, mix_w_out, mem_pre, mem_post, mem_kv_g, mem_w_q, mem_w_kv, mem_w_o, ffn2_pre, ffn2_post, ffn2_w_in, ffn2_w_out):
    batch, seq, _ = x.shape
    depth = ffn1_pre.shape[0]
    h = x.reshape(batch * seq, D_MODEL)
    mem2 = mem.reshape(-1, D_MODEL)
    pad_cols = RWKV_PAD - RWKV_IN
    for l in range(depth):
        w_mix_in = jnp.pad(mix_w_in[l], ((0, 0), (0, pad_cols))).astype(BF16)
        h, qkv, rw = _pre_mix(h, _row(ffn1_pre[l]), _row(ffn1_post[l]), ffn1_w_in[l].astype(BF16),
                              ffn1_w_out[l].astype(BF16), _row(mix_pre[l]), w_mix_in)
        sb_o = _sb_attention(qkv, _row(sb_out_g[l]), batch, seq)
        w2 = _pad_rows(rwkv_w2[l], LANES).astype(BF16)
        a2 = jnp.pad(rwkv_a2[l], ((DECAY_LORA, 0), (0, 0))).astype(BF16)
        g2 = _pad_rows(rwkv_g2[l], GATE_PAD).astype(BF16)
        mu = jnp.pad(rwkv_mu[l], (0, pad_cols))
        rw_o = _rwkv(rw, batch, seq, _row(mu), _row(rwkv_w0[l]), w2, _row(rwkv_a0[l]), a2, g2,
                     _row(rwkv_k_k[l]), _row(rwkv_k_a[l]), _row(rwkv_r_k[l]),
                     _row(rwkv_lnx_g[l]), _row(rwkv_lnx_b[l]))
        kv = _mem_kv(mem2, _row(mem_kv_g[l]), mem_w_kv[l].astype(BF16))
        h = _post_mix(sb_o, rw_o, h, kv, mix_w_out[l].astype(BF16), _row(mix_post[l]),
                      _row(mem_pre[l]), mem_w_q[l].astype(BF16), mem_w_o[l].astype(BF16),
                      _row(mem_post[l]), _row(ffn2_pre[l]), _row(ffn2_post[l]),
                      ffn2_w_in[l].astype(BF16), ffn2_w_out[l].astype(BF16), batch, seq)
    return h.reshape(batch, seq, D_MODEL)
```

```python
import functools

import jax
import jax.numpy as jnp
from jax import lax
from jax.experimental import pallas as pl
from jax.experimental.pallas import tpu as pltpu

D_MODEL = 1024
HEAD_DIM = 64
SB_HEADS = 8
RWKV_HEADS = 8
SB_WIDTH = SB_HEADS * HEAD_DIM
RWKV_WIDTH = RWKV_HEADS * HEAD_DIM
MIX_WIDTH = SB_WIDTH + RWKV_WIDTH
DECAY_LORA = 64
AAA_LORA = 64
GATE_LORA = 160
RWKV_IN = 3 * RWKV_WIDTH + DECAY_LORA + AAA_LORA + GATE_LORA
MEM_HEADS = 4
MEM_HEAD_DIM = 128
MEM_WIDTH = MEM_HEADS * MEM_HEAD_DIM
D_FF = 2816
NORM_EPS = 1e-6
LNX_EPS = 64e-5

LANES = 128
MXU_DIM = 256
V7X_VMEM_BYTES = 64 * 1024 * 1024
VMEM_LIMIT_BYTES = V7X_VMEM_BYTES - 8 * 1024 * 1024

RWKV_PAD = 15 * LANES
GATE_PAD = RWKV_PAD - (3 * RWKV_WIDTH + LANES)
SB_TILE = 128
SB_QROWS = 512
SB_CARRY_STOP = 112.0
SB_FIXED_WAVES = 3
LOG2E = 1.4426950408889634
RWKV_CHUNK = 64
RWKV_ROWS = 256
RWKV_QUAD = MXU_DIM
TOKEN_TILE = 512
FF_CHUNK = 256
ROW_PARTS = 2

BF16 = jnp.bfloat16
F32 = jnp.float32


def _const_spec(shape):
    nd = len(shape)
    return pl.BlockSpec(shape, lambda *_: (0,) * nd, pipeline_mode=pl.Buffered(1))


def _params(*sem):
    return pltpu.CompilerParams(dimension_semantics=sem, vmem_limit_bytes=VMEM_LIMIT_BYTES)


def _rms(x, g):
    ms = jnp.mean(x * x, axis=-1, keepdims=True)
    return x * lax.rsqrt(ms + NORM_EPS) * g


def _dot(a, b):
    return jnp.dot(a, b, preferred_element_type=F32)


def _dot_nt(a, b):
    return lax.dot_general(a, b, (((1,), (1,)), ((), ())), preferred_element_type=F32)


def _dot_tn(a, b):
    return lax.dot_general(a, b, (((0,), (0,)), ((), ())), preferred_element_type=F32)


def _split2(x):
    hi = x.astype(BF16)
    lo = (x - hi.astype(F32)).astype(BF16)
    return hi, lo


def _ffn_body(x, pre_ref, post_ref, win_ref, wout_ref, act_ref):
    xn = _rms(x, pre_ref[...]).astype(BF16)
    for c in range(D_FF // FF_CHUNK):
        lo = c * FF_CHUNK
        gate = _dot(xn, win_ref[:, lo:lo + FF_CHUNK])
        up = _dot(xn, win_ref[:, D_FF + lo:D_FF + lo + FF_CHUNK])
        act_ref[:, lo:lo + FF_CHUNK] = (gate * jax.nn.sigmoid(gate) * up).astype(BF16)
    y = _dot(act_ref[...], wout_ref[...])
    return x + 0.5 * _rms(y, post_ref[...])


def _pre_mix_kernel(x_ref, fpre_ref, fpost_ref, win_ref, wout_ref, mpre_ref, wmix_ref,
                    h_ref, qkv_ref, rw_ref, act_ref):
    h = _ffn_body(x_ref[...], fpre_ref, fpost_ref, win_ref, wout_ref, act_ref)
    h_ref[...] = h
    hn = _rms(h, mpre_ref[...]).astype(BF16)
    qkv_ref[...] = _dot(hn, wmix_ref[:, :3 * SB_WIDTH]).astype(BF16)
    rw_ref[...] = _dot(hn, wmix_ref[:, 3 * SB_WIDTH:])


def _pre_mix(x, fpre, fpost, w_in, w_out, mpre, w_mix):
    t = x.shape[0]
    tm = min(TOKEN_TILE, t)
    row = lambda i: (i, 0)
    vec = _const_spec((1, D_MODEL))
    return pl.pallas_call(
        _pre_mix_kernel,
        grid=(t // tm,),
        in_specs=[
            pl.BlockSpec((tm, D_MODEL), row), vec, vec,
            _const_spec((D_MODEL, 2 * D_FF)), _const_spec((D_FF, D_MODEL)),
            vec, _const_spec((D_MODEL, 3 * SB_WIDTH + RWKV_PAD)),
        ],
        out_specs=[pl.BlockSpec((tm, D_MODEL), row), pl.BlockSpec((tm, 3 * SB_WIDTH), row),
                   pl.BlockSpec((tm, RWKV_PAD), row)],
        out_shape=[jax.ShapeDtypeStruct((t, D_MODEL), F32),
                   jax.ShapeDtypeStruct((t, 3 * SB_WIDTH), BF16),
                   jax.ShapeDtypeStruct((t, RWKV_PAD), F32)],
        scratch_shapes=[pltpu.VMEM((tm, D_FF), BF16)],
        compiler_params=_params("parallel"),
        name="ffn1_mix_in",
    )(x, fpre, fpost, w_in, w_out, mpre, w_mix)


def _interleave(*stages):
    gens = [g for g, _ in stages]
    counts = [n for _, n in stages]
    done = [0] * len(gens)
    live = [True] * len(gens)
    while any(live):
        which = min((n for n in range(len(gens)) if live[n]), key=lambda n: (done[n] + 1) / counts[n])
        try:
            next(gens[which])
            done[which] += 1
        except StopIteration:
            live[which] = False


def _post_mix_kernel(sb_ref, rw_ref, h_ref, kv_ref, wmix_ref, mixpost_ref, mempre_ref, wq_ref, wo_ref,
                     mempost_ref, fpre_ref, fpost_ref, win_ref, wout_ref, o_ref, act_ref, hs_ref, xn_ref):
    step = pl.program_id(0)

    @pl.when(step == 0)
    def _():
        hs_ref[...] = jnp.zeros_like(hs_ref)
        xn_ref[...] = jnp.zeros_like(xn_ref)

    result = {}

    def ffn():
        xn = xn_ref[...]
        for c in range(D_FF // FF_CHUNK):
            lo = c * FF_CHUNK
            gate = _dot(xn, win_ref[:, lo:lo + FF_CHUNK])
            up = _dot(xn, win_ref[:, D_FF + lo:D_FF + lo + FF_CHUNK])
            act_ref[:, lo:lo + FF_CHUNK] = (gate * jax.nn.sigmoid(gate) * up).astype(BF16)
            yield
        tm = hs_ref.shape[0]
        for lo in range(0, tm, tm // ROW_PARTS):
            rows = slice(lo, lo + tm // ROW_PARTS)
            y = _dot(act_ref[rows, :], wout_ref[...])
            o_ref[rows, :] = hs_ref[rows, :] + 0.5 * _rms(y, fpost_ref[...])
            yield

    def attn():
        mixed = _dot(sb_ref[...], wmix_ref[:SB_WIDTH, :]) + _dot(rw_ref[...], wmix_ref[SB_WIDTH:, :])
        yield
        h = h_ref[...] + _rms(mixed, mixpost_ref[...])
        q = _dot(_rms(h, mempre_ref[...]).astype(BF16), wq_ref[...])
        q = (q * (MEM_HEAD_DIM ** -0.5)).astype(BF16)
        yield
        outs = []
        for hd in range(MEM_HEADS):
            lo = hd * MEM_HEAD_DIM
            s = _dot_nt(q[:, lo:lo + MEM_HEAD_DIM], kv_ref[:, lo:lo + MEM_HEAD_DIM])
            e = jnp.exp(s - jnp.max(s, axis=-1, keepdims=True))
            pr = e / jnp.sum(e, axis=-1, keepdims=True)
            outs.append(_dot(pr.astype(BF16), kv_ref[:, MEM_WIDTH + lo:MEM_WIDTH + lo + MEM_HEAD_DIM]))
            yield
        o = jnp.concatenate(outs, axis=1).astype(BF16)
        h = h + _rms(_dot(o, wo_ref[...]), mempost_ref[...])
        yield
        result["h"] = h
        result["xn"] = _rms(h, fpre_ref[...]).astype(BF16)

    _interleave((ffn(), D_FF // FF_CHUNK + ROW_PARTS), (attn(), MEM_HEADS + 4))
    hs_ref[...] = result["h"]
    xn_ref[...] = result["xn"]


def _post_mix(sb_o, rw_o, h, kv, w_mix, mix_post, mem_pre, w_q, w_o, mem_post, fpre, fpost, w_in, w_out,
              batch, seq):
    mem_len = kv.shape[0] // batch
    kv = kv.reshape(batch, mem_len, 2 * MEM_WIDTH)
    t = batch * seq
    sb_o = sb_o.reshape(t, SB_WIDTH)
    rw_o = rw_o.reshape(t, RWKV_WIDTH)
    tm = min(TOKEN_TILE, seq)
    n = t // tm
    per_batch = seq // tm
    cur = lambda s: jnp.minimum(s, n - 1)
    tile = lambda width: pl.BlockSpec((tm, width), lambda s: (cur(s), 0))
    vec = _const_spec((1, D_MODEL))
    return pl.pallas_call(
        _post_mix_kernel,
        grid=(n + 1,),
        in_specs=[
            tile(SB_WIDTH), tile(RWKV_WIDTH), tile(D_MODEL),
            pl.BlockSpec((None, mem_len, 2 * MEM_WIDTH), lambda s: (cur(s) // per_batch, 0, 0)),
            _const_spec((MIX_WIDTH, D_MODEL)), vec, vec,
            _const_spec((D_MODEL, MEM_WIDTH)), _const_spec((MEM_WIDTH, D_MODEL)), vec,
            vec, vec, _const_spec((D_MODEL, 2 * D_FF)), _const_spec((D_FF, D_MODEL)),
        ],
        out_specs=pl.BlockSpec((tm, D_MODEL), lambda s: (jnp.maximum(s - 1, 0), 0)),
        out_shape=jax.ShapeDtypeStruct((t, D_MODEL), F32),
        scratch_shapes=[pltpu.VMEM((tm, D_FF), BF16), pltpu.VMEM((tm, D_MODEL), F32),
                        pltpu.VMEM((tm, D_MODEL), BF16)],
        compiler_params=_params("arbitrary"),
        name="mix_out_mem_ffn2",
    )(sb_o, rw_o, h, kv, w_mix, mix_post, mem_pre, w_q, w_o, mem_post, fpre, fpost, w_in, w_out)


def _mem_kv_kernel(m_ref, g_ref, w_ref, o_ref):
    o_ref[...] = _dot(_rms(m_ref[...], g_ref[...]).astype(BF16), w_ref[...]).astype(BF16)


def _mem_kv(mem, g, w):
    t = mem.shape[0]
    tm = min(TOKEN_TILE, t)
    row = lambda i: (i, 0)
    return pl.pallas_call(
        _mem_kv_kernel,
        grid=(t // tm,),
        in_specs=[pl.BlockSpec((tm, D_MODEL), row), _const_spec((1, D_MODEL)),
                  _const_spec((D_MODEL, 2 * MEM_WIDTH))],
        out_specs=pl.BlockSpec((tm, 2 * MEM_WIDTH), row),
        out_shape=jax.ShapeDtypeStruct((t, 2 * MEM_WIDTH), BF16),
        compiler_params=_params("parallel"),
        name="mem_kv",
    )(mem, g, w)


def _sb_kernel(q_ref, k_ref, v_ref, g_ref, cum_ref, o_ref, k2_ref, v2_ref, carry_ref, acc_ref, *, seq):
    i = pl.program_id(2)
    nk = seq // SB_TILE
    tq = SB_QROWS
    sub = tq // SB_TILE
    lane = lax.broadcasted_iota(jnp.int32, (SB_TILE, LANES), 1)
    head0 = lane < HEAD_DIM

    @pl.when(i == 0)
    def _():
        for j in range(nk):
            kb = k_ref[j * SB_TILE:(j + 1) * SB_TILE, :] * jnp.asarray(HEAD_DIM ** -0.5, BF16)
            vb = v_ref[j * SB_TILE:(j + 1) * SB_TILE, :]
            zero = jnp.zeros_like(kb)
            k2_ref[j, :SB_TILE, :] = jnp.where(head0, kb, zero)
            k2_ref[j, SB_TILE:, :] = jnp.where(head0, zero, kb)
            v2_ref[j, :SB_TILE, :] = jnp.where(head0, vb, zero)
            v2_ref[j, SB_TILE:, :] = jnp.where(head0, zero, vb)

    cum = cum_ref[...]
    row = lax.broadcasted_iota(jnp.int32, (SB_TILE, 2 * SB_TILE), 0)
    col = lax.broadcasted_iota(jnp.int32, (SB_TILE, 2 * SB_TILE), 1) & (SB_TILE - 1)
    causal = col < row

    def scores(blocks, tiles, diagonal, valid=None):
        z = jnp.concatenate([_dot_nt(q_ref[rb * SB_TILE:(rb + 1) * SB_TILE, :], k2_ref[j])
                             for rb, j in zip(blocks, tiles)], axis=0)
        e = jnp.exp2(jnp.abs(z) * (-LOG2E))
        sp = jnp.maximum(z, 0.0) + jnp.log(1.0 + e)
        lsig = z - sp
        keep = None
        if diagonal:
            keep = jnp.concatenate([causal] * len(blocks), axis=0)
        if valid is not None:
            keep = valid if keep is None else jnp.logical_and(keep, valid)
        if keep is not None:
            sp = jnp.where(keep, sp, 0.0)
        return lsig, _dot(sp.astype(BF16), cum), keep

    def weights(blocks, tiles, lsig, r, keep):
        lo, hi = blocks[0] * SB_TILE, (blocks[-1] + 1) * SB_TILE
        carry = carry_ref[lo:hi, :]
        a = jnp.exp(lsig - r[:, :2 * SB_TILE] - carry)
        if keep is not None:
            a = jnp.where(keep, a, 0.0)
        a = a.astype(BF16)
        out = jnp.concatenate([_dot(a[n * SB_TILE:(n + 1) * SB_TILE, :], v2_ref[j])
                               for n, j in enumerate(tiles)], axis=0)
        carry_ref[lo:hi, :] = carry + r[:, 2 * SB_TILE:]
        acc_ref[lo:hi, :] += out

    def waves(plan):
        sc = [scores(*w) for w in plan]
        for (blocks, tiles, _), (lsig, r, keep) in zip(plan, sc):
            weights(blocks, tiles, lsig, r, keep)

    carry_ref[...] = jnp.zeros_like(carry_ref)
    acc_ref[...] = jnp.zeros_like(acc_ref)
    first = sub * i
    everyone = list(range(sub))

    @pl.when(i == 0)
    def _():
        waves([(everyone[d:], [rb - d for rb in everyone[d:]], d == 0) for d in range(sub)])

    @pl.when(i > 0)
    def _():
        waves([(everyone, [first + rb - d for rb in everyone], d == 0) for d in range(SB_FIXED_WAVES)])

    rb_of_row = lax.broadcasted_iota(jnp.int32, (tq, 1), 0) // SB_TILE

    def cond(state):
        d, smallest = state
        return jnp.logical_and(jnp.logical_and(i > 0, d < first + sub), smallest < SB_CARRY_STOP)

    def body(state):
        d, _ = state
        tiles = [jnp.maximum(first + rb - d, 0) for rb in everyone]
        lsig, r, keep = scores(everyone, tiles, False, (first + rb_of_row - d) >= 0)
        weights(everyone, tiles, lsig, r, keep)
        return d + 1, jnp.min(carry_ref[...])

    lax.while_loop(cond, body, (jnp.int32(SB_FIXED_WAVES), jnp.min(carry_ref[...])))

    acc = acc_ref[...]
    h0 = lax.broadcasted_iota(jnp.int32, acc.shape, 1) < HEAD_DIM
    sq = acc * acc
    ms0 = jnp.sum(jnp.where(h0, sq, 0.0), axis=-1, keepdims=True)
    ms1 = jnp.sum(jnp.where(h0, 0.0, sq), axis=-1, keepdims=True)
    ms = jnp.where(h0, ms0, ms1) * (1.0 / HEAD_DIM)
    o_ref[...] = (acc * lax.rsqrt(ms + NORM_EPS) * g_ref[...]).astype(BF16)


def _sb_cum_matrix():
    n = 2 * SB_TILE
    j = jnp.arange(n)[:, None]
    s = jnp.arange(n)[None, :]
    same = (j // SB_TILE) == (s // SB_TILE)
    tri = same & (j > s)
    return jnp.concatenate([tri, same], axis=1).astype(BF16)


def _sb_attention(qkv, out_g, batch, seq):
    assert seq % SB_QROWS == 0 and SB_FIXED_WAVES <= SB_QROWS // SB_TILE
    qkv = qkv.reshape(batch, seq, 3 * SB_WIDTH)
    pairs = SB_WIDTH // LANES
    kern = functools.partial(_sb_kernel, seq=seq)
    return pl.pallas_call(
        kern,
        grid=(batch, pairs, seq // SB_QROWS),
        in_specs=[
            pl.BlockSpec((None, SB_QROWS, LANES), lambda b, p, i: (b, i, p)),
            pl.BlockSpec((None, seq, LANES), lambda b, p, i: (b, 0, pairs + p)),
            pl.BlockSpec((None, seq, LANES), lambda b, p, i: (b, 0, 2 * pairs + p)),
            pl.BlockSpec((1, LANES), lambda b, p, i: (0, p)),
            _const_spec((2 * SB_TILE, 4 * SB_TILE)),
        ],
        out_specs=pl.BlockSpec((None, SB_QROWS, LANES), lambda b, p, i: (b, i, p)),
        out_shape=jax.ShapeDtypeStruct((batch, seq, SB_WIDTH), BF16),
        scratch_shapes=[pltpu.VMEM((seq // SB_TILE, 2 * SB_TILE, LANES), BF16),
                        pltpu.VMEM((seq // SB_TILE, 2 * SB_TILE, LANES), BF16),
                        pltpu.VMEM((SB_QROWS, 2 * SB_TILE), F32),
                        pltpu.VMEM((SB_QROWS, LANES), F32)],
        compiler_params=_params("parallel", "parallel", "arbitrary"),
        name="sb_attention",
    )(qkv, qkv, qkv, out_g, _sb_cum_matrix())


def _rwkv_kernel(rw_ref, prev_ref, mu_ref, w0_ref, w2_ref, a0_ref, a2_ref, g2_ref, kk_ref, ka_ref,
                 rk_ref, lg_ref, lb_ref, cum_ref, grp_ref, o_ref, state_s):
    c = pl.program_id(1)
    rows = RWKV_ROWS
    ch = RWKV_CHUNK
    width = RWKV_WIDTH
    quad = RWKV_QUAD
    heads = quad // HEAD_DIM

    @pl.when(c == 0)
    def _():
        state_s[...] = jnp.zeros_like(state_s)

    def gsum(x):
        hi, lo = _split2(x)
        hl = jnp.concatenate([hi, lo], axis=0)
        r = jnp.concatenate([_dot(hl[:, q0:q0 + quad], grp_ref[...]) for q0 in range(0, width, quad)],
                            axis=1)
        return r[:rows] + r[rows:]

    p = rw_ref[...]
    prev_last = jnp.where(c == 0, 0.0, prev_ref[7:8, :])
    rowi = lax.broadcasted_iota(jnp.int32, p.shape, 0)
    shifted = jnp.where(rowi == 0, prev_last, pltpu.roll(p, 1, axis=0))
    p = p + (shifted - p) * mu_ref[...]
    r = p[:, :width]
    k = p[:, width:2 * width]
    v = p[:, 2 * width:3 * width]
    xwa = p[:, 3 * width:3 * width + LANES]
    xg = p[:, 3 * width + LANES:]
    lane = lax.broadcasted_iota(jnp.int32, xwa.shape, 1)
    xw = jnp.where(lane < DECAY_LORA, jnp.tanh(xwa), 0.0).astype(BF16)
    xa = jnp.where(lane < DECAY_LORA, 0.0, xwa).astype(BF16)
    wlin = w0_ref[...] + _dot(xw, w2_ref[...])
    wlog = -jax.nn.softplus(-wlin) - 0.5
    lw = -jnp.exp(wlog)
    lr = jax.nn.sigmoid(a0_ref[...] + _dot(xa, a2_ref[...]))
    gate = _dot(jax.nn.sigmoid(xg).astype(BF16), g2_ref[...])
    kk = k * kk_ref[...]
    kk = kk * lax.rsqrt(jnp.maximum(gsum(kk * kk), 1e-24))
    k = k * (1.0 + (lr - 1.0) * ka_ref[...])
    a_vec = -kk
    b_vec = kk * lr

    cs = _dot(cum_ref[...], jnp.concatenate(_split2(lw), axis=1))
    cs = cs[:, :width] + cs[:, width:]
    g_in = cs[:rows]
    g_tot = cs[rows:]
    e_in = jnp.exp(g_in)
    e_inv = jnp.exp(-g_in)
    e_end = jnp.exp(g_tot - g_in)
    at_all = (a_vec * jnp.exp(g_in - lw)).astype(BF16)
    rt_all = (r * e_in).astype(BF16)
    bt_all = (b_vec * e_inv).astype(BF16)
    kt_all = (k * e_inv).astype(BF16)
    v_all = v.astype(BF16)
    bh_all = (b_vec * e_end).astype(BF16)
    kh_all = (k * e_end).astype(BF16)
    gam_all = jnp.exp(g_tot)

    t_i = lax.broadcasted_iota(jnp.int32, (ch, quad), 0)
    l_i = lax.broadcasted_iota(jnp.int32, (ch, quad), 1)
    i_i = l_i & (HEAD_DIM - 1)
    strict = i_i < t_i
    incl = i_i <= t_i
    eye = (i_i == t_i).astype(F32)
    in_head = [(l_i // HEAD_DIM) == h for h in range(heads)]
    rr = lax.broadcasted_iota(jnp.int32, (quad, quad), 0) // HEAD_DIM
    cc = lax.broadcasted_iota(jnp.int32, (quad, quad), 1) // HEAD_DIM
    same_head = rr == cc

    def bd(y):
        y = y.astype(BF16)
        zero = jnp.zeros_like(y)
        return jnp.concatenate([jnp.where(m, y, zero) for m in in_head], axis=0)

    def pp(x, y):
        return _dot(x.astype(BF16), bd(y))

    def unpack_diag(full):
        out = jnp.where(in_head[0], full[:ch], 0.0)
        for h in range(1, heads):
            out = out + jnp.where(in_head[h], full[h * ch:(h + 1) * ch], 0.0)
        return out

    nq = width // quad
    combos = [(ci, qd) for ci in range(rows // ch) for qd in range(nq)]

    def cut(x, ci, qd):
        return x[ci * ch:(ci + 1) * ch, qd * quad:(qd + 1) * quad]

    def each(fn, *lists):
        return [fn(*args) for args in zip(*lists)]

    at = [cut(at_all, *cq) for cq in combos]
    rt = [cut(rt_all, *cq) for cq in combos]
    vv = [cut(v_all, *cq) for cq in combos]
    bh = [cut(bh_all, *cq) for cq in combos]
    kh = [cut(kh_all, *cq) for cq in combos]
    aa = [_dot_nt(jnp.concatenate([a, r_], axis=0),
                  jnp.concatenate([bd(cut(bt_all, *cq)), bd(cut(kt_all, *cq))], axis=0))
          for a, r_, cq in zip(at, rt, combos)]
    n_ab = [jnp.where(strict, x[:ch, :quad], 0.0) for x in aa]
    n_ak = [jnp.where(strict, x[:ch, quad:], 0.0) for x in aa]
    a_rb = [jnp.where(incl, x[ch:, :quad], 0.0) for x in aa]
    a_rk = [jnp.where(incl, x[ch:, quad:], 0.0) for x in aa]

    blk = 8
    diag = (t_i // blk) == (i_i // blk)
    nd = [jnp.where(diag, x, 0.0) for x in n_ab]
    n2 = each(pp, nd, nd)
    n4 = each(pp, n2, n2)
    tm = [eye + x for x in nd]
    tm = each(lambda t, n: t + pp(t, n), tm, n2)
    tm = each(lambda t, n: t + pp(t, n), tm, n4)
    while blk < ch:
        off = ((t_i // (2 * blk)) == (i_i // (2 * blk))) & ((t_i // blk) != (i_i // blk))
        half = each(lambda t, n: pp(t, jnp.where(off, n, 0.0)), tm, n_ab)
        tm = each(lambda t, h: t + pp(h, t), tm, half)
        blk *= 2

    akv = each(pp, n_ak, vv)
    wu = each(lambda t, a, b: _dot(t.astype(BF16), jnp.concatenate([bd(a), bd(b)], axis=1)),
              tm, at, akv)
    wa = [x[:, :quad] for x in wu]
    uv = [x[:, quad:] for x in wu]
    ry = each(lambda a, w, u: _dot(a.astype(BF16), jnp.concatenate([bd(w), bd(u)], axis=1)),
              a_rb, wa, uv)
    rhat = each(lambda r_, x: r_.astype(F32) + x[:, :quad], rt, ry)
    yv = each(lambda x, a, v_: x[:, quad:] + pp(a, v_), ry, a_rk, vv)
    gc = each(lambda w, b: jnp.where(same_head, _dot_tn(w.astype(BF16), b), 0.0).astype(BF16), wa, bh)
    sv = each(lambda u, v_, b, k_: unpack_diag(_dot_tn(jnp.concatenate([u.astype(BF16), v_], axis=0),
                                                       jnp.concatenate([b, k_], axis=0))),
              uv, vv, bh, kh)

    states = [state_s[qd] for qd in range(nq)]
    y_out = {}
    for n, (ci, qd) in enumerate(combos):
        s0 = states[qd]
        y_out[ci, qd] = _dot_nt(rhat[n].astype(BF16), bd(s0)) + yv[n]
        gam_row = gam_all[ci * ch:ci * ch + 1, qd * quad:(qd + 1) * quad]
        states[qd] = s0 * gam_row + _dot(s0.astype(BF16), gc[n]) + sv[n]
    for qd in range(nq):
        state_s[qd] = states[qd]
    y_rows = [jnp.concatenate([y_out[ci, qd] for qd in range(nq)], axis=1)
              for ci in range(rows // ch)]

    y = jnp.concatenate(y_rows, axis=0)
    mean = gsum(y) * (1.0 / HEAD_DIM)
    d = y - mean
    var = gsum(d * d) * (1.0 / HEAD_DIM)
    y = d * lax.rsqrt(var + LNX_EPS) * lg_ref[...] + lb_ref[...]
    bonus = gsum(r * k * rk_ref[...]) * v
    o_ref[...] = ((y + bonus) * gate).astype(BF16)


def _rwkv_cum_matrix():
    t = jnp.arange(RWKV_ROWS)[:, None]
    i = jnp.arange(RWKV_ROWS)[None, :]
    same = (t // RWKV_CHUNK) == (i // RWKV_CHUNK)
    return jnp.concatenate([same & (i <= t), same], axis=0).astype(BF16)


def _group_matrix():
    a = jnp.arange(RWKV_QUAD)
    return (a[:, None] // HEAD_DIM == a[None, :] // HEAD_DIM).astype(BF16)


def _rwkv(rw, batch, seq, mu, w0, w2, a0, a2, g2, k_k, k_a, r_k, lnx_g, lnx_b):
    rw = rw.reshape(batch, seq, RWKV_PAD)
    rows = RWKV_ROWS
    vec = lambda n: _const_spec((1, n))
    return pl.pallas_call(
        _rwkv_kernel,
        grid=(batch, seq // rows),
        in_specs=[
            pl.BlockSpec((None, rows, RWKV_PAD), lambda b, c: (b, c, 0)),
            pl.BlockSpec((None, 8, RWKV_PAD), lambda b, c: (b, jnp.maximum(c * (rows // 8) - 1, 0), 0)),
            vec(RWKV_PAD), vec(RWKV_WIDTH), _const_spec((LANES, RWKV_WIDTH)), vec(RWKV_WIDTH),
            _const_spec((LANES, RWKV_WIDTH)), _const_spec((GATE_PAD, RWKV_WIDTH)),
            vec(RWKV_WIDTH), vec(RWKV_WIDTH), vec(RWKV_WIDTH), vec(RWKV_WIDTH), vec(RWKV_WIDTH),
            _const_spec((2 * rows, rows)), _const_spec((RWKV_QUAD, RWKV_QUAD)),
        ],
        out_specs=pl.BlockSpec((None, rows, RWKV_WIDTH), lambda b, c: (b, c, 0)),
        out_shape=jax.ShapeDtypeStruct((batch, seq, RWKV_WIDTH), BF16),
        scratch_shapes=[pltpu.VMEM((RWKV_WIDTH // RWKV_QUAD, RWKV_CHUNK, RWKV_QUAD), F32)],
        compiler_params=_params("parallel", "arbitrary"),
        name="rwkv7",
    )(rw, rw, mu, w0, w2, a0, a2, g2, k_k, k_a, r_k, lnx_g, lnx_b,
      _rwkv_cum_matrix(), _group_matrix())


def _row(a):
    return a.reshape(1, -1).astype(F32)


def _pad_rows(w, n):
    return jnp.pad(w, ((0, n - w.shape[0]), (0, 0)))


def kernel(x, mem, ffn1_pre, ffn1_post, ffn1_w_in, ffn1_w_out, mix_pre, mix_post, mix_w_in, rwkv_mu, rwkv_w0, rwkv_w2, rwkv_a0, rwkv_a2, rwkv_g2, rwkv_k_k, rwkv_k_a, rwkv_r_k, rwkv_lnx_g, rwkv_lnx_b, sb_out_g, mix_w_out, mem_pre, mem_post, mem_kv_g, mem_w_q, mem_w_kv, mem_w_o, ffn2_pre, ffn2_post, ffn2_w_in, ffn2_w_out):
    batch, seq, _ = x.shape
    depth = ffn1_pre.shape[0]
    h = x.reshape(batch * seq, D_MODEL)
    mem2 = mem.reshape(-1, D_MODEL)
    pad_cols = RWKV_PAD - RWKV_IN
    for l in range(depth):
        w_mix_in = jnp.pad(mix_w_in[l], ((0, 0), (0, pad_cols))).astype(BF16)
        h, qkv, rw = _pre_mix(h, _row(ffn1_pre[l]), _row(ffn1_post[l]), ffn1_w_in[l].astype(BF16),
                              ffn1_w_out[l].astype(BF16), _row(mix_pre[l]), w_mix_in)
        sb_o = _sb_attention(qkv, _row(sb_out_g[l]), batch, seq)
        w2 = _pad_rows(rwkv_w2[l], LANES).astype(BF16)
        a2 = jnp.pad(rwkv_a2[l], ((DECAY_LORA, 0), (0, 0))).astype(BF16)
        g2 = _pad_rows(rwkv_g2[l], GATE_PAD).astype(BF16)
        mu = jnp.pad(rwkv_mu[l], (0, pad_cols))
        rw_o = _rwkv(rw, batch, seq, _row(mu), _row(rwkv_w0[l]), w2, _row(rwkv_a0[l]), a2, g2,
                     _row(rwkv_k_k[l]), _row(rwkv_k_a[l]), _row(rwkv_r_k[l]),
                     _row(rwkv_lnx_g[l]), _row(rwkv_lnx_b[l]))
        kv = _mem_kv(mem2, _row(mem_kv_g[l]), mem_w_kv[l].astype(BF16))
        h = _post_mix(sb_o, rw_o, h, kv, mix_w_out[l].astype(BF16), _row(mix_post[l]),
                      _row(mem_pre[l]), mem_w_q[l].astype(BF16), mem_w_o[l].astype(BF16),
                      _row(mem_post[l]), _row(ffn2_pre[l]), _row(ffn2_post[l]),
                      ffn2_w_in[l].astype(BF16), ffn2_w_out[l].astype(BF16), batch, seq)
    return h.reshape(batch, seq, D_MODEL)
```

```python
import functools

import jax
import jax.numpy as jnp
from jax import lax
from jax.experimental import pallas as pl
from jax.experimental.pallas import tpu as pltpu

D_MODEL = 1024
HEAD_DIM = 64
SB_HEADS = 8
RWKV_HEADS = 8
SB_WIDTH = SB_HEADS * HEAD_DIM
RWKV_WIDTH = RWKV_HEADS * HEAD_DIM
MIX_WIDTH = SB_WIDTH + RWKV_WIDTH
DECAY_LORA = 64
AAA_LORA = 64
GATE_LORA = 160
RWKV_IN = 3 * RWKV_WIDTH + DECAY_LORA + AAA_LORA + GATE_LORA
MEM_HEADS = 4
MEM_HEAD_DIM = 128
MEM_WIDTH = MEM_HEADS * MEM_HEAD_DIM
D_FF = 2816
NORM_EPS = 1e-6
LNX_EPS = 64e-5

LANES = 128
MXU_DIM = 256
V7X_VMEM_BYTES = 64 * 1024 * 1024
VMEM_LIMIT_BYTES = V7X_VMEM_BYTES - 8 * 1024 * 1024

RWKV_PAD = 15 * LANES
GATE_PAD = RWKV_PAD - (3 * RWKV_WIDTH + LANES)
SB_TILE = 128
SB_QROWS = 512
SB_CARRY_STOP = 112.0
SB_FIXED_WAVES = 3
LOG2E = 1.4426950408889634
RWKV_CHUNK = 64
RWKV_ROWS = 256
RWKV_QUAD = MXU_DIM
TOKEN_TILE = 512
FF_CHUNK = 256
ROW_PARTS = 2

BF16 = jnp.bfloat16
F32 = jnp.float32


def _const_spec(shape):
    nd = len(shape)
    return pl.BlockSpec(shape, lambda *_: (0,) * nd, pipeline_mode=pl.Buffered(1))


def _params(*sem):
    return pltpu.CompilerParams(dimension_semantics=sem, vmem_limit_bytes=VMEM_LIMIT_BYTES)


def _rms(x, g):
    ms = jnp.mean(x * x, axis=-1, keepdims=True)
    return x * lax.rsqrt(ms + NORM_EPS) * g


def _dot(a, b):
    return jnp.dot(a, b, preferred_element_type=F32)


def _dot_nt(a, b):
    return lax.dot_general(a, b, (((1,), (1,)), ((), ())), preferred_element_type=F32)


def _dot_tn(a, b):
    return lax.dot_general(a, b, (((0,), (0,)), ((), ())), preferred_element_type=F32)


def _split2(x):
    hi = x.astype(BF16)
    lo = (x - hi.astype(F32)).astype(BF16)
    return hi, lo


def _ffn_body(x, pre_ref, post_ref, win_ref, wout_ref, act_ref):
    xn = _rms(x, pre_ref[...]).astype(BF16)
    for c in range(D_FF // FF_CHUNK):
        lo = c * FF_CHUNK
        gate = _dot(xn, win_ref[:, lo:lo + FF_CHUNK])
        up = _dot(xn, win_ref[:, D_FF + lo:D_FF + lo + FF_CHUNK])
        act_ref[:, lo:lo + FF_CHUNK] = (gate * jax.nn.sigmoid(gate) * up).astype(BF16)
    y = _dot(act_ref[...], wout_ref[...])
    return x + 0.5 * _rms(y, post_ref[...])


def _pre_mix_kernel(x_ref, fpre_ref, fpost_ref, win_ref, wout_ref, mpre_ref, wmix_ref,
                    h_ref, qkv_ref, rw_ref, act_ref):
    h = _ffn_body(x_ref[...], fpre_ref, fpost_ref, win_ref, wout_ref, act_ref)
    h_ref[...] = h
    hn = _rms(h, mpre_ref[...]).astype(BF16)
    qkv_ref[...] = _dot(hn, wmix_ref[:, :3 * SB_WIDTH]).astype(BF16)
    rw_ref[...] = _dot(hn, wmix_ref[:, 3 * SB_WIDTH:])


def _pre_mix(x, fpre, fpost, w_in, w_out, mpre, w_mix):
    t = x.shape[0]
    tm = min(TOKEN_TILE, t)
    row = lambda i: (i, 0)
    vec = _const_spec((1, D_MODEL))
    return pl.pallas_call(
        _pre_mix_kernel,
        grid=(t // tm,),
        in_specs=[
            pl.BlockSpec((tm, D_MODEL), row), vec, vec,
            _const_spec((D_MODEL, 2 * D_FF)), _const_spec((D_FF, D_MODEL)),
            vec, _const_spec((D_MODEL, 3 * SB_WIDTH + RWKV_PAD)),
        ],
        out_specs=[pl.BlockSpec((tm, D_MODEL), row), pl.BlockSpec((tm, 3 * SB_WIDTH), row),
                   pl.BlockSpec((tm, RWKV_PAD), row)],
        out_shape=[jax.ShapeDtypeStruct((t, D_MODEL), F32),
                   jax.ShapeDtypeStruct((t, 3 * SB_WIDTH), BF16),
                   jax.ShapeDtypeStruct((t, RWKV_PAD), F32)],
        scratch_shapes=[pltpu.VMEM((tm, D_FF), BF16)],
        compiler_params=_params("parallel"),
        name="ffn1_mix_in",
    )(x, fpre, fpost, w_in, w_out, mpre, w_mix)


def _interleave(*stages):
    gens = [g for g, _ in stages]
    counts = [n for _, n in stages]
    done = [0] * len(gens)
    live = [True] * len(gens)
    while any(live):
        which = min((n for n in range(len(gens)) if live[n]), key=lambda n: (done[n] + 1) / counts[n])
        try:
            next(gens[which])
            done[which] += 1
        except StopIteration:
            live[which] = False


def _post_mix_kernel(sb_ref, rw_ref, h_ref, kv_ref, wmix_ref, mixpost_ref, mempre_ref, wq_ref, wo_ref,
                     mempost_ref, fpre_ref, fpost_ref, win_ref, wout_ref, o_ref, act_ref, hs_ref, xn_ref):
    step = pl.program_id(0)

    @pl.when(step == 0)
    def _():
        hs_ref[...] = jnp.zeros_like(hs_ref)
        xn_ref[...] = jnp.zeros_like(xn_ref)

    result = {}

    def ffn():
        xn = xn_ref[...]
        for c in range(D_FF // FF_CHUNK):
            lo = c * FF_CHUNK
            gate = _dot(xn, win_ref[:, lo:lo + FF_CHUNK])
            up = _dot(xn, win_ref[:, D_FF + lo:D_FF + lo + FF_CHUNK])
            act_ref[:, lo:lo + FF_CHUNK] = (gate * jax.nn.sigmoid(gate) * up).astype(BF16)
            yield
        tm = hs_ref.shape[0]
        for lo in range(0, tm, tm // ROW_PARTS):
            rows = slice(lo, lo + tm // ROW_PARTS)
            y = _dot(act_ref[rows, :], wout_ref[...])
            o_ref[rows, :] = hs_ref[rows, :] + 0.5 * _rms(y, fpost_ref[...])
            yield

    def attn():
        mixed = _dot(sb_ref[...], wmix_ref[:SB_WIDTH, :]) + _dot(rw_ref[...], wmix_ref[SB_WIDTH:, :])
        yield
        h = h_ref[...] + _rms(mixed, mixpost_ref[...])
        q = _dot(_rms(h, mempre_ref[...]).astype(BF16), wq_ref[...])
        q = (q * (MEM_HEAD_DIM ** -0.5)).astype(BF16)
        yield
        outs = []
        for hd in range(MEM_HEADS):
            lo = hd * MEM_HEAD_DIM
            s = _dot_nt(q[:, lo:lo + MEM_HEAD_DIM], kv_ref[:, lo:lo + MEM_HEAD_DIM])
            e = jnp.exp(s - jnp.max(s, axis=-1, keepdims=True))
            pr = e / jnp.sum(e, axis=-1, keepdims=True)
            outs.append(_dot(pr.astype(BF16), kv_ref[:, MEM_WIDTH + lo:MEM_WIDTH + lo + MEM_HEAD_DIM]))
            yield
        o = jnp.concatenate(outs, axis=1).astype(BF16)
        h = h + _rms(_dot(o, wo_ref[...]), mempost_ref[...])
        yield
        result["h"] = h
        result["xn"] = _rms(h, fpre_ref[...]).astype(BF16)

    _interleave((ffn(), D_FF // FF_CHUNK + ROW_PARTS), (attn(), MEM_HEADS + 4))
    hs_ref[...] = result["h"]
    xn_ref[...] = result["xn"]


def _post_mix(sb_o, rw_o, h, kv, w_mix, mix_post, mem_pre, w_q, w_o, mem_post, fpre, fpost, w_in, w_out,
              batch, seq):
    mem_len = kv.shape[0] // batch
    kv = kv.reshape(batch, mem_len, 2 * MEM_WIDTH)
    t = batch * seq
    sb_o = sb_o.reshape(t, SB_WIDTH)
    rw_o = rw_o.reshape(t, RWKV_WIDTH)
    tm = min(TOKEN_TILE, seq)
    n = t // tm
    per_batch = seq // tm
    cur = lambda s: jnp.minimum(s, n - 1)
    tile = lambda width: pl.BlockSpec((tm, width), lambda s: (cur(s), 0))
    vec = _const_spec((1, D_MODEL))
    return pl.pallas_call(
        _post_mix_kernel,
        grid=(n + 1,),
        in_specs=[
            tile(SB_WIDTH), tile(RWKV_WIDTH), tile(D_MODEL),
            pl.BlockSpec((None, mem_len, 2 * MEM_WIDTH), lambda s: (cur(s) // per_batch, 0, 0)),
            _const_spec((MIX_WIDTH, D_MODEL)), vec, vec,
            _const_spec((D_MODEL, MEM_WIDTH)), _const_spec((MEM_WIDTH, D_MODEL)), vec,
            vec, vec, _const_spec((D_MODEL, 2 * D_FF)), _const_spec((D_FF, D_MODEL)),
        ],
        out_specs=pl.BlockSpec((tm, D_MODEL), lambda s: (jnp.maximum(s - 1, 0), 0)),
        out_shape=jax.ShapeDtypeStruct((t, D_MODEL), F32),
        scratch_shapes=[pltpu.VMEM((tm, D_FF), BF16), pltpu.VMEM((tm, D_MODEL), F32),
                        pltpu.VMEM((tm, D_MODEL), BF16)],
        compiler_params=_params("arbitrary"),
        name="mix_out_mem_ffn2",
    )(sb_o, rw_o, h, kv, w_mix, mix_post, mem_pre, w_q, w_o, mem_post, fpre, fpost, w_in, w_out)


def _mem_kv_kernel(m_ref, g_ref, w_ref, o_ref):
    o_ref[...] = _dot(_rms(m_ref[...], g_ref[...]).astype(BF16), w_ref[...]).astype(BF16)


def _mem_kv(mem, g, w):
    t = mem.shape[0]
    tm = min(TOKEN_TILE, t)
    row = lambda i: (i, 0)
    return pl.pallas_call(
        _mem_kv_kernel,
        grid=(t // tm,),
        in_specs=[pl.BlockSpec((tm, D_MODEL), row), _const_spec((1, D_MODEL)),
                  _const_spec((D_MODEL, 2 * MEM_WIDTH))],
        out_specs=pl.BlockSpec((tm, 2 * MEM_WIDTH), row),
        out_shape=jax.ShapeDtypeStruct((t, 2 * MEM_WIDTH), BF16),
        compiler_params=_params("parallel"),
        name="mem_kv",
    )(mem, g, w)


def _sb_kernel(q_ref, k_ref, v_ref, g_ref, cum_ref, o_ref, k2_ref, v2_ref, carry_ref, acc_ref, *, seq):
    i = pl.program_id(2)
    nk = seq // SB_TILE
    tq = SB_QROWS
    sub = tq // SB_TILE
    lane = lax.broadcasted_iota(jnp.int32, (SB_TILE, LANES), 1)
    head0 = lane < HEAD_DIM

    @pl.when(i == 0)
    def _():
        for j in range(nk):
            kb = k_ref[j * SB_TILE:(j + 1) * SB_TILE, :] * jnp.asarray(HEAD_DIM ** -0.5, BF16)
            vb = v_ref[j * SB_TILE:(j + 1) * SB_TILE, :]
            zero = jnp.zeros_like(kb)
            k2_ref[j, :SB_TILE, :] = jnp.where(head0, kb, zero)
            k2_ref[j, SB_TILE:, :] = jnp.where(head0, zero, kb)
            v2_ref[j, :SB_TILE, :] = jnp.where(head0, vb, zero)
            v2_ref[j, SB_TILE:, :] = jnp.where(head0, zero, vb)

    cum = cum_ref[...]
    row = lax.broadcasted_iota(jnp.int32, (SB_TILE, 2 * SB_TILE), 0)
    col = lax.broadcasted_iota(jnp.int32, (SB_TILE, 2 * SB_TILE), 1) & (SB_TILE - 1)
    causal = col < row

    def scores(blocks, tiles, diagonal, valid=None):
        z = jnp.concatenate([_dot_nt(q_ref[rb * SB_TILE:(rb + 1) * SB_TILE, :], k2_ref[j])
                             for rb, j in zip(blocks, tiles)], axis=0)
        e = jnp.exp2(jnp.abs(z) * (-LOG2E))
        sp = jnp.maximum(z, 0.0) + jnp.log(1.0 + e)
        lsig = z - sp
        keep = None
        if diagonal:
            keep = jnp.concatenate([causal] * len(blocks), axis=0)
        if valid is not None:
            keep = valid if keep is None else jnp.logical_and(keep, valid)
        if keep is not None:
            sp = jnp.where(keep, sp, 0.0)
        return lsig, _dot(sp.astype(BF16), cum), keep

    def weights(blocks, tiles, lsig, r, keep):
        lo, hi = blocks[0] * SB_TILE, (blocks[-1] + 1) * SB_TILE
        carry = carry_ref[lo:hi, :]
        a = jnp.exp(lsig - r[:, :2 * SB_TILE] - carry)
        if keep is not None:
            a = jnp.where(keep, a, 0.0)
        a = a.astype(BF16)
        out = jnp.concatenate([_dot(a[n * SB_TILE:(n + 1) * SB_TILE, :], v2_ref[j])
                               for n, j in enumerate(tiles)], axis=0)
        carry_ref[lo:hi, :] = carry + r[:, 2 * SB_TILE:]
        acc_ref[lo:hi, :] += out

    def waves(plan):
        sc = [scores(*w) for w in plan]
        for (blocks, tiles, _), (lsig, r, keep) in zip(plan, sc):
            weights(blocks, tiles, lsig, r, keep)

    carry_ref[...] = jnp.zeros_like(carry_ref)
    acc_ref[...] = jnp.zeros_like(acc_ref)
    first = sub * i
    everyone = list(range(sub))

    @pl.when(i == 0)
    def _():
        waves([(everyone[d:], [rb - d for rb in everyone[d:]], d == 0) for d in range(sub)])

    @pl.when(i > 0)
    def _():
        waves([(everyone, [first + rb - d for rb in everyone], d == 0) for d in range(SB_FIXED_WAVES)])

    rb_of_row = lax.broadcasted_iota(jnp.int32, (tq, 1), 0) // SB_TILE

    def cond(state):
        d, smallest = state
        return jnp.logical_and(jnp.logical_and(i > 0, d < first + sub), smallest < SB_CARRY_STOP)

    def body(state):
        d, _ = state
        tiles = [jnp.maximum(first + rb - d, 0) for rb in everyone]
        lsig, r, keep = scores(everyone, tiles, False, (first + rb_of_row - d) >= 0)
        weights(everyone, tiles, lsig, r, keep)
        return d + 1, jnp.min(carry_ref[...])

    lax.while_loop(cond, body, (jnp.int32(SB_FIXED_WAVES), jnp.min(carry_ref[...])))

    acc = acc_ref[...]
    h0 = lax.broadcasted_iota(jnp.int32, acc.shape, 1) < HEAD_DIM
    sq = acc * acc
    ms0 = jnp.sum(jnp.where(h0, sq, 0.0), axis=-1, keepdims=True)
    ms1 = jnp.sum(jnp.where(h0, 0.0, sq), axis=-1, keepdims=True)
    ms = jnp.where(h0, ms0, ms1) * (1.0 / HEAD_DIM)
    o_ref[...] = (acc * lax.rsqrt(ms + NORM_EPS) * g_ref[...]).astype(BF16)


def _sb_cum_matrix():
    n = 2 * SB_TILE
    j = jnp.arange(n)[:, None]
    s = jnp.arange(n)[None, :]
    same = (j // SB_TILE) == (s // SB_TILE)
    tri = same & (j > s)
    return jnp.concatenate([tri, same], axis=1).astype(BF16)


def _sb_attention(qkv, out_g, batch, seq):
    assert seq % SB_QROWS == 0 and SB_FIXED_WAVES <= SB_QROWS // SB_TILE
    qkv = qkv.reshape(batch, seq, 3 * SB_WIDTH)
    pairs = SB_WIDTH // LANES
    kern = functools.partial(_sb_kernel, seq=seq)
    return pl.pallas_call(
        kern,
        grid=(batch, pairs, seq // SB_QROWS),
        in_specs=[
            pl.BlockSpec((None, SB_QROWS, LANES), lambda b, p, i: (b, i, p)),
            pl.BlockSpec((None, seq, LANES), lambda b, p, i: (b, 0, pairs + p)),
            pl.BlockSpec((None, seq, LANES), lambda b, p, i: (b, 0, 2 * pairs + p)),
            pl.BlockSpec((1, LANES), lambda b, p, i: (0, p)),
            _const_spec((2 * SB_TILE, 4 * SB_TILE)),
        ],
        out_specs=pl.BlockSpec((None, SB_QROWS, LANES), lambda b, p, i: (b, i, p)),
        out_shape=jax.ShapeDtypeStruct((batch, seq, SB_WIDTH), BF16),
        scratch_shapes=[pltpu.VMEM((seq // SB_TILE, 2 * SB_TILE, LANES), BF16),
                        pltpu.VMEM((seq // SB_TILE, 2 * SB_TILE, LANES), BF16),
                        pltpu.VMEM((SB_QROWS, 2 * SB_TILE), F32),
                        pltpu.VMEM((SB_QROWS, LANES), F32)],
        compiler_params=_params("parallel", "parallel", "arbitrary"),
        name="sb_attention",
    )(qkv, qkv, qkv, out_g, _sb_cum_matrix())


def _rwkv_kernel(rw_ref, prev_ref, mu_ref, w0_ref, w2_ref, a0_ref, a2_ref, g2_ref, kk_ref, ka_ref,
                 rk_ref, lg_ref, lb_ref, cum_ref, grp_ref, o_ref, state_s):
    c = pl.program_id(1)
    rows = RWKV_ROWS
    ch = RWKV_CHUNK
    width = RWKV_WIDTH
    quad = RWKV_QUAD
    heads = quad // HEAD_DIM

    @pl.when(c == 0)
    def _():
        state_s[...] = jnp.zeros_like(state_s)

    def gsum(x):
        x = x.astype(BF16)
        return jnp.concatenate([_dot(x[:, q0:q0 + quad], grp_ref[...]) for q0 in range(0, width, quad)],
                               axis=1)

    p = rw_ref[...]
    prev_last = jnp.where(c == 0, 0.0, prev_ref[7:8, :])
    rowi = lax.broadcasted_iota(jnp.int32, p.shape, 0)
    shifted = jnp.where(rowi == 0, prev_last, pltpu.roll(p, 1, axis=0))
    p = p + (shifted - p) * mu_ref[...]
    r = p[:, :width]
    k = p[:, width:2 * width]
    v = p[:, 2 * width:3 * width]
    xwa = p[:, 3 * width:3 * width + LANES]
    xg = p[:, 3 * width + LANES:]
    lane = lax.broadcasted_iota(jnp.int32, xwa.shape, 1)
    xw = jnp.where(lane < DECAY_LORA, jnp.tanh(xwa), 0.0).astype(BF16)
    xa = jnp.where(lane < DECAY_LORA, 0.0, xwa).astype(BF16)
    wlin = w0_ref[...] + _dot(xw, w2_ref[...])
    wlog = -jax.nn.softplus(-wlin) - 0.5
    lw = -jnp.exp(wlog)
    lr = jax.nn.sigmoid(a0_ref[...] + _dot(xa, a2_ref[...]))
    gate = _dot(jax.nn.sigmoid(xg).astype(BF16), g2_ref[...])
    kk = k * kk_ref[...]
    kk = kk * lax.rsqrt(jnp.maximum(gsum(kk * kk), 1e-24))
    k = k * (1.0 + (lr - 1.0) * ka_ref[...])
    a_vec = -kk
    b_vec = kk * lr

    cs = _dot(cum_ref[...], jnp.concatenate(_split2(lw), axis=1))
    g_in = cs[:, :width] + cs[:, width:]
    g_tot = jnp.concatenate([jnp.broadcast_to(g_in[c0 + ch - 1:c0 + ch, :], (ch, width))
                             for c0 in range(0, rows, ch)], axis=0)
    e_in = jnp.exp(g_in)
    e_inv = jnp.exp(-g_in)
    e_end = jnp.exp(g_tot - g_in)
    at_all = (a_vec * jnp.exp(g_in - lw)).astype(BF16)
    rt_all = (r * e_in).astype(BF16)
    bt_all = (b_vec * e_inv).astype(BF16)
    kt_all = (k * e_inv).astype(BF16)
    v_all = v.astype(BF16)
    bh_all = (b_vec * e_end).astype(BF16)
    kh_all = (k * e_end).astype(BF16)
    gam_all = jnp.exp(g_tot)

    t_i = lax.broadcasted_iota(jnp.int32, (ch, quad), 0)
    l_i = lax.broadcasted_iota(jnp.int32, (ch, quad), 1)
    i_i = l_i & (HEAD_DIM - 1)
    strict = i_i < t_i
    incl = i_i <= t_i
    eye = (i_i == t_i).astype(F32)
    in_head = [(l_i // HEAD_DIM) == h for h in range(heads)]
    rr = lax.broadcasted_iota(jnp.int32, (quad, quad), 0) // HEAD_DIM
    cc = lax.broadcasted_iota(jnp.int32, (quad, quad), 1) // HEAD_DIM
    same_head = rr == cc

    def bd(y):
        y = y.astype(BF16)
        zero = jnp.zeros_like(y)
        return jnp.concatenate([jnp.where(m, y, zero) for m in in_head], axis=0)

    def pp(x, y):
        return _dot(x.astype(BF16), bd(y))

    def unpack_diag(full):
        out = jnp.where(in_head[0], full[:ch], 0.0)
        for h in range(1, heads):
            out = out + jnp.where(in_head[h], full[h * ch:(h + 1) * ch], 0.0)
        return out

    def pp2(x, y0, y1):
        x = x.astype(BF16)
        y0 = y0.astype(BF16)
        y1 = y1.astype(BF16)
        half_lane = lax.broadcasted_iota(jnp.int32, (ch, LANES), 1) // HEAD_DIM
        in_half = [half_lane == h for h in range(LANES // HEAD_DIM)]
        outs = []
        for lo in range(0, quad, LANES):
            blocks = []
            for y in (y0, y1):
                part = y[:, lo:lo + LANES]
                zero = jnp.zeros_like(part)
                blocks.append(jnp.concatenate([jnp.where(m, part, zero) for m in in_half], axis=0))
            outs.append(_dot(x[:, lo:lo + LANES], jnp.concatenate(blocks, axis=1)))
        first = jnp.concatenate([o[:, :LANES] for o in outs], axis=1)
        second = jnp.concatenate([o[:, LANES:] for o in outs], axis=1)
        return jnp.concatenate([first, second], axis=1)

    def state_terms(x, v_, b, k_):
        left = jnp.concatenate([x.astype(BF16), jnp.concatenate([jnp.zeros_like(v_), v_], axis=1)], axis=0)
        return _dot_tn(left, jnp.concatenate([b, k_], axis=0))

    nq = width // quad
    combos = [(ci, qd) for ci in range(rows // ch) for qd in range(nq)]

    def cut(x, ci, qd):
        return x[ci * ch:(ci + 1) * ch, qd * quad:(qd + 1) * quad]

    def each(fn, *lists):
        return [fn(*args) for args in zip(*lists)]

    at = [cut(at_all, *cq) for cq in combos]
    rt = [cut(rt_all, *cq) for cq in combos]
    vv = [cut(v_all, *cq) for cq in combos]
    bh = [cut(bh_all, *cq) for cq in combos]
    kh = [cut(kh_all, *cq) for cq in combos]
    aa = [_dot_nt(jnp.concatenate([a, r_], axis=0),
                  jnp.concatenate([bd(cut(bt_all, *cq)), bd(cut(kt_all, *cq))], axis=0))
          for a, r_, cq in zip(at, rt, combos)]
    n_ab = [jnp.where(strict, x[:ch, :quad], 0.0) for x in aa]
    n_ak = [jnp.where(strict, x[:ch, quad:], 0.0) for x in aa]
    a_rb = [jnp.where(incl, x[ch:, :quad], 0.0) for x in aa]
    a_rk = [jnp.where(incl, x[ch:, quad:], 0.0) for x in aa]

    blk = 8
    diag = (t_i // blk) == (i_i // blk)
    nd = [jnp.where(diag, x, 0.0) for x in n_ab]
    n2 = each(pp, nd, nd)
    tm = [eye + x for x in nd]
    both = each(lambda t, n: _dot(jnp.concatenate([t, n], axis=0).astype(BF16), bd(n)), tm, n2)
    tm = each(lambda t, x: t + x[:ch], tm, both)
    tm = each(lambda t, x: t + pp(t, x[ch:]), tm, both)
    while blk < ch:
        off = ((t_i // (2 * blk)) == (i_i // (2 * blk))) & ((t_i // blk) != (i_i // blk))
        half = each(lambda t, n: pp(t, jnp.where(off, n, 0.0)), tm, n_ab)
        tm = each(lambda t, h: t + pp(h, t), tm, half)
        blk *= 2

    kv_both = each(lambda a, b, v_: _dot(jnp.concatenate([a, b], axis=0).astype(BF16), bd(v_)), n_ak, a_rk, vv)
    akv = [x[:ch] for x in kv_both]
    wu = each(pp2, tm, at, akv)
    wa = [x[:, :quad] for x in wu]
    uv = [x[:, quad:] for x in wu]
    ry = each(pp2, a_rb, wa, uv)
    rhat = each(lambda r_, x: r_.astype(F32) + x[:, :quad], rt, ry)
    yv = each(lambda x, y: x[:, quad:] + y[ch:], ry, kv_both)
    gs = each(state_terms, wu, vv, bh, kh)
    gc = [jnp.where(same_head, x[:quad], 0.0).astype(BF16) for x in gs]
    sv = [unpack_diag(x[quad:]) for x in gs]

    states = [state_s[qd] for qd in range(nq)]
    y_out = {}
    for n, (ci, qd) in enumerate(combos):
        s0 = states[qd]
        y_out[ci, qd] = _dot_nt(rhat[n].astype(BF16), bd(s0)) + yv[n]
        gam_row = gam_all[ci * ch:ci * ch + 1, qd * quad:(qd + 1) * quad]
        states[qd] = s0 * gam_row + _dot(s0.astype(BF16), gc[n]) + sv[n]
    for qd in range(nq):
        state_s[qd] = states[qd]
    y_rows = [jnp.concatenate([y_out[ci, qd] for qd in range(nq)], axis=1)
              for ci in range(rows // ch)]

    y = jnp.concatenate(y_rows, axis=0)
    mean = gsum(y) * (1.0 / HEAD_DIM)
    d = y - mean
    var = gsum(d * d) * (1.0 / HEAD_DIM)
    y = d * lax.rsqrt(var + LNX_EPS) * lg_ref[...] + lb_ref[...]
    bonus = gsum(r * k * rk_ref[...]) * v
    o_ref[...] = ((y + bonus) * gate).astype(BF16)


def _rwkv_cum_matrix():
    t = jnp.arange(RWKV_ROWS)[:, None]
    i = jnp.arange(RWKV_ROWS)[None, :]
    return (((t // RWKV_CHUNK) == (i // RWKV_CHUNK)) & (i <= t)).astype(BF16)


def _group_matrix():
    a = jnp.arange(RWKV_QUAD)
    return (a[:, None] // HEAD_DIM == a[None, :] // HEAD_DIM).astype(BF16)


def _rwkv(rw, batch, seq, mu, w0, w2, a0, a2, g2, k_k, k_a, r_k, lnx_g, lnx_b):
    rw = rw.reshape(batch, seq, RWKV_PAD)
    rows = RWKV_ROWS
    vec = lambda n: _const_spec((1, n))
    return pl.pallas_call(
        _rwkv_kernel,
        grid=(batch, seq // rows),
        in_specs=[
            pl.BlockSpec((None, rows, RWKV_PAD), lambda b, c: (b, c, 0)),
            pl.BlockSpec((None, 8, RWKV_PAD), lambda b, c: (b, jnp.maximum(c * (rows // 8) - 1, 0), 0)),
            vec(RWKV_PAD), vec(RWKV_WIDTH), _const_spec((LANES, RWKV_WIDTH)), vec(RWKV_WIDTH),
            _const_spec((LANES, RWKV_WIDTH)), _const_spec((GATE_PAD, RWKV_WIDTH)),
            vec(RWKV_WIDTH), vec(RWKV_WIDTH), vec(RWKV_WIDTH), vec(RWKV_WIDTH), vec(RWKV_WIDTH),
            _const_spec((rows, rows)), _const_spec((RWKV_QUAD, RWKV_QUAD)),
        ],
        out_specs=pl.BlockSpec((None, rows, RWKV_WIDTH), lambda b, c: (b, c, 0)),
        out_shape=jax.ShapeDtypeStruct((batch, seq, RWKV_WIDTH), BF16),
        scratch_shapes=[pltpu.VMEM((RWKV_WIDTH // RWKV_QUAD, RWKV_CHUNK, RWKV_QUAD), F32)],
        compiler_params=_params("parallel", "arbitrary"),
        name="rwkv7",
    )(rw, rw, mu, w0, w2, a0, a2, g2, k_k, k_a, r_k, lnx_g, lnx_b,
      _rwkv_cum_matrix(), _group_matrix())


def _row(a):
    return a.reshape(1, -1).astype(F32)


def _pad_rows(w, n):
    return jnp.pad(w, ((0, n - w.shape[0]), (0, 0)))


def kernel(x, mem, ffn1_pre, ffn1_post, ffn1_w_in, ffn1_w_out, mix_pre, mix_post, mix_w_in, rwkv_mu, rwkv_w0, rwkv_w2, rwkv_a0, rwkv_a2, rwkv_g2, rwkv_k_k, rwkv_k_a, rwkv_r_k, rwkv_lnx_g, rwkv_lnx_b, sb_out_g, mix_w_out, mem_pre, mem_post, mem_kv_g, mem_w_q, mem_w_kv, mem_w_o, ffn2_pre, ffn2_post, ffn2_w_in, ffn2_w_out):
    batch, seq, _ = x.shape
    depth = ffn1_pre.shape[0]
    h = x.reshape(batch * seq, D_MODEL)
    mem2 = mem.reshape(-1, D_MODEL)
    pad_cols = RWKV_PAD - RWKV_IN
    for l in range(depth):
        w_mix_in = jnp.pad(mix_w_in[l].astype(BF16), ((0, 0), (0, pad_cols)))
        h, qkv, rw = _pre_mix(h, _row(ffn1_pre[l]), _row(ffn1_post[l]), ffn1_w_in[l].astype(BF16),
                              ffn1_w_out[l].astype(BF16), _row(mix_pre[l]), w_mix_in)
        sb_o = _sb_attention(qkv, _row(sb_out_g[l]), batch, seq)
        w2 = _pad_rows(rwkv_w2[l], LANES).astype(BF16)
        a2 = jnp.pad(rwkv_a2[l], ((DECAY_LORA, 0), (0, 0))).astype(BF16)
        g2 = _pad_rows(rwkv_g2[l], GATE_PAD).astype(BF16)
        mu = jnp.pad(rwkv_mu[l], (0, pad_cols))
        rw_o = _rwkv(rw, batch, seq, _row(mu), _row(rwkv_w0[l]), w2, _row(rwkv_a0[l]), a2, g2,
                     _row(rwkv_k_k[l]), _row(rwkv_k_a[l]), _row(rwkv_r_k[l]),
                     _row(rwkv_lnx_g[l]), _row(rwkv_lnx_b[l]))
        kv = _mem_kv(mem2, _row(mem_kv_g[l]), mem_w_kv[l].astype(BF16))
        h = _post_mix(sb_o, rw_o, h, kv, mix_w_out[l].astype(BF16), _row(mix_post[l]),
                      _row(mem_pre[l]), mem_w_q[l].astype(BF16), mem_w_o[l].astype(BF16),
                      _row(mem_post[l]), _row(ffn2_pre[l]), _row(ffn2_post[l]),
                      ffn2_w_in[l].astype(BF16), ffn2_w_out[l].astype(BF16), batch, seq)
    return h.reshape(batch, seq, D_MODEL)
```

```python
import functools

import jax
import jax.numpy as jnp
from jax import lax
from jax.experimental import pallas as pl
from jax.experimental.pallas import tpu as pltpu

D_MODEL = 1024
HEAD_DIM = 64
SB_HEADS = 8
RWKV_HEADS = 8
SB_WIDTH = SB_HEADS * HEAD_DIM
RWKV_WIDTH = RWKV_HEADS * HEAD_DIM
MIX_WIDTH = SB_WIDTH + RWKV_WIDTH
DECAY_LORA = 64
AAA_LORA = 64
GATE_LORA = 160
RWKV_IN = 3 * RWKV_WIDTH + DECAY_LORA + AAA_LORA + GATE_LORA
MEM_HEADS = 4
MEM_HEAD_DIM = 128
MEM_WIDTH = MEM_HEADS * MEM_HEAD_DIM
D_FF = 2816
NORM_EPS = 1e-6
LNX_EPS = 64e-5

LANES = 128
MXU_DIM = 256
V7X_VMEM_BYTES = 64 * 1024 * 1024
VMEM_LIMIT_BYTES = V7X_VMEM_BYTES - 8 * 1024 * 1024

RWKV_PAD = 15 * LANES
GATE_PAD = RWKV_PAD - (3 * RWKV_WIDTH + LANES)
SB_TILE = 128
SB_QROWS = 512
SB_CARRY_STOP = 112.0
SB_FIXED_WAVES = 3
SB_PAIRS = 2
LOG2E = 1.4426950408889634
RWKV_CHUNK = 64
RWKV_ROWS = 512
RWKV_QUAD = MXU_DIM
TOKEN_TILE = 512
FF_CHUNK = 256
ROW_PARTS = 2

BF16 = jnp.bfloat16
F32 = jnp.float32


def _const_spec(shape):
    nd = len(shape)
    return pl.BlockSpec(shape, lambda *_: (0,) * nd, pipeline_mode=pl.Buffered(1))


def _params(*sem):
    return pltpu.CompilerParams(dimension_semantics=sem, vmem_limit_bytes=VMEM_LIMIT_BYTES)


def _rms(x, g):
    ms = jnp.mean(x * x, axis=-1, keepdims=True)
    return x * lax.rsqrt(ms + NORM_EPS) * g


def _dot(a, b):
    return jnp.dot(a, b, preferred_element_type=F32)


def _dot_nt(a, b):
    return lax.dot_general(a, b, (((1,), (1,)), ((), ())), preferred_element_type=F32)


def _dot_tn(a, b):
    return lax.dot_general(a, b, (((0,), (0,)), ((), ())), preferred_element_type=F32)


def _split2(x):
    hi = x.astype(BF16)
    lo = (x - hi.astype(F32)).astype(BF16)
    return hi, lo


def _ffn_body(x, pre_ref, post_ref, win_ref, wout_ref, act_ref):
    xn = _rms(x, pre_ref[...]).astype(BF16)
    for c in range(D_FF // FF_CHUNK):
        lo = c * FF_CHUNK
        gate = _dot(xn, win_ref[:, lo:lo + FF_CHUNK])
        up = _dot(xn, win_ref[:, D_FF + lo:D_FF + lo + FF_CHUNK])
        act_ref[:, lo:lo + FF_CHUNK] = (gate * jax.nn.sigmoid(gate) * up).astype(BF16)
    y = _dot(act_ref[...], wout_ref[...])
    return x + 0.5 * _rms(y, post_ref[...])


def _pre_mix_kernel(x_ref, fpre_ref, fpost_ref, win_ref, wout_ref, mpre_ref, wmix_ref,
                    h_ref, qkv_ref, rw_ref, act_ref):
    h = _ffn_body(x_ref[...], fpre_ref, fpost_ref, win_ref, wout_ref, act_ref)
    h_ref[...] = h
    hn = _rms(h, mpre_ref[...]).astype(BF16)
    qkv_ref[...] = _dot(hn, wmix_ref[:, :3 * SB_WIDTH]).astype(BF16)
    rw_ref[...] = _dot(hn, wmix_ref[:, 3 * SB_WIDTH:])


def _pre_mix(x, fpre, fpost, w_in, w_out, mpre, w_mix):
    t = x.shape[0]
    tm = min(TOKEN_TILE, t)
    row = lambda i: (i, 0)
    vec = _const_spec((1, D_MODEL))
    return pl.pallas_call(
        _pre_mix_kernel,
        grid=(t // tm,),
        in_specs=[
            pl.BlockSpec((tm, D_MODEL), row), vec, vec,
            _const_spec((D_MODEL, 2 * D_FF)), _const_spec((D_FF, D_MODEL)),
            vec, _const_spec((D_MODEL, 3 * SB_WIDTH + RWKV_PAD)),
        ],
        out_specs=[pl.BlockSpec((tm, D_MODEL), row), pl.BlockSpec((tm, 3 * SB_WIDTH), row),
                   pl.BlockSpec((tm, RWKV_PAD), row)],
        out_shape=[jax.ShapeDtypeStruct((t, D_MODEL), F32),
                   jax.ShapeDtypeStruct((t, 3 * SB_WIDTH), BF16),
                   jax.ShapeDtypeStruct((t, RWKV_PAD), F32)],
        scratch_shapes=[pltpu.VMEM((tm, D_FF), BF16)],
        compiler_params=_params("parallel"),
        name="ffn1_mix_in",
    )(x, fpre, fpost, w_in, w_out, mpre, w_mix)


def _interleave(*stages):
    gens = [g for g, _ in stages]
    counts = [n for _, n in stages]
    done = [0] * len(gens)
    live = [True] * len(gens)
    while any(live):
        which = min((n for n in range(len(gens)) if live[n]), key=lambda n: (done[n] + 1) / counts[n])
        try:
            next(gens[which])
            done[which] += 1
        except StopIteration:
            live[which] = False


def _post_mix_kernel(sb_ref, rw_ref, h_ref, kv_ref, wmix_ref, mixpost_ref, mempre_ref, wq_ref, wo_ref,
                     mempost_ref, fpre_ref, fpost_ref, win_ref, wout_ref, o_ref, act_ref, hs_ref, xn_ref):
    step = pl.program_id(0)

    @pl.when(step == 0)
    def _():
        hs_ref[...] = jnp.zeros_like(hs_ref)
        xn_ref[...] = jnp.zeros_like(xn_ref)

    result = {}

    def ffn():
        xn = xn_ref[...]
        for c in range(D_FF // FF_CHUNK):
            lo = c * FF_CHUNK
            gate = _dot(xn, win_ref[:, lo:lo + FF_CHUNK])
            up = _dot(xn, win_ref[:, D_FF + lo:D_FF + lo + FF_CHUNK])
            act_ref[:, lo:lo + FF_CHUNK] = (gate * jax.nn.sigmoid(gate) * up).astype(BF16)
            yield
        tm = hs_ref.shape[0]
        for lo in range(0, tm, tm // ROW_PARTS):
            rows = slice(lo, lo + tm // ROW_PARTS)
            y = _dot(act_ref[rows, :], wout_ref[...])
            o_ref[rows, :] = hs_ref[rows, :] + 0.5 * _rms(y, fpost_ref[...])
            yield

    def attn():
        mixed = _dot(sb_ref[...], wmix_ref[:SB_WIDTH, :]) + _dot(rw_ref[...], wmix_ref[SB_WIDTH:, :])
        yield
        h = h_ref[...] + _rms(mixed, mixpost_ref[...])
        q = _dot(_rms(h, mempre_ref[...]).astype(BF16), wq_ref[...])
        q = (q * (MEM_HEAD_DIM ** -0.5)).astype(BF16)
        yield
        outs = []
        for hd in range(MEM_HEADS):
            lo = hd * MEM_HEAD_DIM
            s = _dot_nt(q[:, lo:lo + MEM_HEAD_DIM], kv_ref[:, lo:lo + MEM_HEAD_DIM])
            e = jnp.exp(s - jnp.max(s, axis=-1, keepdims=True))
            pr = e / jnp.sum(e, axis=-1, keepdims=True)
            outs.append(_dot(pr.astype(BF16), kv_ref[:, MEM_WIDTH + lo:MEM_WIDTH + lo + MEM_HEAD_DIM]))
            yield
        o = jnp.concatenate(outs, axis=1).astype(BF16)
        h = h + _rms(_dot(o, wo_ref[...]), mempost_ref[...])
        yield
        result["h"] = h
        result["xn"] = _rms(h, fpre_ref[...]).astype(BF16)

    _interleave((ffn(), D_FF // FF_CHUNK + ROW_PARTS), (attn(), MEM_HEADS + 4))
    hs_ref[...] = result["h"]
    xn_ref[...] = result["xn"]


def _post_mix(sb_o, rw_o, h, kv, w_mix, mix_post, mem_pre, w_q, w_o, mem_post, fpre, fpost, w_in, w_out,
              batch, seq):
    mem_len = kv.shape[0] // batch
    kv = kv.reshape(batch, mem_len, 2 * MEM_WIDTH)
    t = batch * seq
    sb_o = sb_o.reshape(t, SB_WIDTH)
    rw_o = rw_o.reshape(t, RWKV_WIDTH)
    tm = min(TOKEN_TILE, seq)
    n = t // tm
    per_batch = seq // tm
    cur = lambda s: jnp.minimum(s, n - 1)
    tile = lambda width: pl.BlockSpec((tm, width), lambda s: (cur(s), 0))
    vec = _const_spec((1, D_MODEL))
    return pl.pallas_call(
        _post_mix_kernel,
        grid=(n + 1,),
        in_specs=[
            tile(SB_WIDTH), tile(RWKV_WIDTH), tile(D_MODEL),
            pl.BlockSpec((None, mem_len, 2 * MEM_WIDTH), lambda s: (cur(s) // per_batch, 0, 0)),
            _const_spec((MIX_WIDTH, D_MODEL)), vec, vec,
            _const_spec((D_MODEL, MEM_WIDTH)), _const_spec((MEM_WIDTH, D_MODEL)), vec,
            vec, vec, _const_spec((D_MODEL, 2 * D_FF)), _const_spec((D_FF, D_MODEL)),
        ],
        out_specs=pl.BlockSpec((tm, D_MODEL), lambda s: (jnp.maximum(s - 1, 0), 0)),
        out_shape=jax.ShapeDtypeStruct((t, D_MODEL), F32),
        scratch_shapes=[pltpu.VMEM((tm, D_FF), BF16), pltpu.VMEM((tm, D_MODEL), F32),
                        pltpu.VMEM((tm, D_MODEL), BF16)],
        compiler_params=_params("arbitrary"),
        name="mix_out_mem_ffn2",
    )(sb_o, rw_o, h, kv, w_mix, mix_post, mem_pre, w_q, w_o, mem_post, fpre, fpost, w_in, w_out)


def _mem_kv_kernel(m_ref, g_ref, w_ref, o_ref):
    o_ref[...] = _dot(_rms(m_ref[...], g_ref[...]).astype(BF16), w_ref[...]).astype(BF16)


def _mem_kv(mem, g, w):
    t = mem.shape[0]
    tm = min(TOKEN_TILE, t)
    row = lambda i: (i, 0)
    return pl.pallas_call(
        _mem_kv_kernel,
        grid=(t // tm,),
        in_specs=[pl.BlockSpec((tm, D_MODEL), row), _const_spec((1, D_MODEL)),
                  _const_spec((D_MODEL, 2 * MEM_WIDTH))],
        out_specs=pl.BlockSpec((tm, 2 * MEM_WIDTH), row),
        out_shape=jax.ShapeDtypeStruct((t, 2 * MEM_WIDTH), BF16),
        compiler_params=_params("parallel"),
        name="mem_kv",
    )(mem, g, w)


def _sb_pair(q_ref, k_ref, v_ref, g_ref, cum_ref, o_ref, k2_ref, v2_ref, carry_ref, acc_ref, *, seq):
    i = pl.program_id(2)
    nk = seq // SB_TILE
    tq = SB_QROWS
    sub = tq // SB_TILE
    lane = lax.broadcasted_iota(jnp.int32, (SB_TILE, LANES), 1)
    head0 = lane < HEAD_DIM

    @pl.when(i == 0)
    def _():
        for j in range(nk):
            kb = k_ref[j * SB_TILE:(j + 1) * SB_TILE, :] * jnp.asarray(HEAD_DIM ** -0.5, BF16)
            vb = v_ref[j * SB_TILE:(j + 1) * SB_TILE, :]
            zero = jnp.zeros_like(kb)
            k2_ref[j, :SB_TILE, :] = jnp.where(head0, kb, zero)
            k2_ref[j, SB_TILE:, :] = jnp.where(head0, zero, kb)
            v2_ref[j, :SB_TILE, :] = jnp.where(head0, vb, zero)
            v2_ref[j, SB_TILE:, :] = jnp.where(head0, zero, vb)

    cum = cum_ref[...]
    row = lax.broadcasted_iota(jnp.int32, (SB_TILE, 2 * SB_TILE), 0)
    col = lax.broadcasted_iota(jnp.int32, (SB_TILE, 2 * SB_TILE), 1) & (SB_TILE - 1)
    causal = col < row

    def scores(blocks, tiles, diagonal, valid=None):
        z = jnp.concatenate([_dot_nt(q_ref[rb * SB_TILE:(rb + 1) * SB_TILE, :], k2_ref[j])
                             for rb, j in zip(blocks, tiles)], axis=0)
        e = jnp.exp2(jnp.abs(z) * (-LOG2E))
        sp = jnp.maximum(z, 0.0) + jnp.log(1.0 + e)
        lsig = z - sp
        keep = None
        if diagonal:
            keep = jnp.concatenate([causal] * len(blocks), axis=0)
        if valid is not None:
            keep = valid if keep is None else jnp.logical_and(keep, valid)
        if keep is not None:
            sp = jnp.where(keep, sp, 0.0)
        return lsig, _dot(sp.astype(BF16), cum), keep

    def weights(blocks, tiles, lsig, r, keep):
        lo, hi = blocks[0] * SB_TILE, (blocks[-1] + 1) * SB_TILE
        carry = carry_ref[lo:hi, :]
        a = jnp.exp(lsig - r[:, :2 * SB_TILE] - carry)
        if keep is not None:
            a = jnp.where(keep, a, 0.0)
        a = a.astype(BF16)
        out = jnp.concatenate([_dot(a[n * SB_TILE:(n + 1) * SB_TILE, :], v2_ref[j])
                               for n, j in enumerate(tiles)], axis=0)
        carry_ref[lo:hi, :] = carry + r[:, 2 * SB_TILE:]
        acc_ref[lo:hi, :] += out

    def waves(plan):
        sc = [scores(*w) for w in plan]
        for (blocks, tiles, _), (lsig, r, keep) in zip(plan, sc):
            weights(blocks, tiles, lsig, r, keep)

    carry_ref[...] = jnp.zeros_like(carry_ref)
    acc_ref[...] = jnp.zeros_like(acc_ref)
    first = sub * i
    everyone = list(range(sub))

    @pl.when(i == 0)
    def _():
        waves([(everyone[d:], [rb - d for rb in everyone[d:]], d == 0) for d in range(sub)])

    @pl.when(i > 0)
    def _():
        waves([(everyone, [first + rb - d for rb in everyone], d == 0) for d in range(SB_FIXED_WAVES)])

    rb_of_row = lax.broadcasted_iota(jnp.int32, (tq, 1), 0) // SB_TILE

    def cond(state):
        d, smallest = state
        return jnp.logical_and(jnp.logical_and(i > 0, d < first + sub), smallest < SB_CARRY_STOP)

    def body(state):
        d, _ = state
        tiles = [jnp.maximum(first + rb - d, 0) for rb in everyone]
        lsig, r, keep = scores(everyone, tiles, False, (first + rb_of_row - d) >= 0)
        weights(everyone, tiles, lsig, r, keep)
        return d + 1, jnp.min(carry_ref[...])

    lax.while_loop(cond, body, (jnp.int32(SB_FIXED_WAVES), jnp.min(carry_ref[...])))

    acc = acc_ref[...]
    h0 = lax.broadcasted_iota(jnp.int32, acc.shape, 1) < HEAD_DIM
    sq = acc * acc
    ms0 = jnp.sum(jnp.where(h0, sq, 0.0), axis=-1, keepdims=True)
    ms1 = jnp.sum(jnp.where(h0, 0.0, sq), axis=-1, keepdims=True)
    ms = jnp.where(h0, ms0, ms1) * (1.0 / HEAD_DIM)
    o_ref[...] = (acc * lax.rsqrt(ms + NORM_EPS) * g_ref[...]).astype(BF16)


def _sb_kernel(q_ref, k_ref, v_ref, g_ref, cum_ref, o_ref, k2_ref, v2_ref, carry_ref, acc_ref, *, seq):
    for p in range(SB_PAIRS):
        cols = pl.ds(p * LANES, LANES)
        _sb_pair(q_ref.at[:, cols], k_ref.at[:, cols], v_ref.at[:, cols], g_ref.at[:, cols], cum_ref,
                 o_ref.at[:, cols], k2_ref.at[p], v2_ref.at[p], carry_ref, acc_ref, seq=seq)


def _sb_cum_matrix():
    n = 2 * SB_TILE
    j = jnp.arange(n)[:, None]
    s = jnp.arange(n)[None, :]
    same = (j // SB_TILE) == (s // SB_TILE)
    tri = same & (j > s)
    return jnp.concatenate([tri, same], axis=1).astype(BF16)


def _sb_attention(qkv, out_g, batch, seq):
    assert seq % SB_QROWS == 0 and SB_FIXED_WAVES <= SB_QROWS // SB_TILE
    qkv = qkv.reshape(batch, seq, 3 * SB_WIDTH)
    width = SB_PAIRS * LANES
    groups = SB_WIDTH // width
    nk = seq // SB_TILE
    kern = functools.partial(_sb_kernel, seq=seq)
    return pl.pallas_call(
        kern,
        grid=(batch, groups, seq // SB_QROWS),
        in_specs=[
            pl.BlockSpec((None, SB_QROWS, width), lambda b, p, i: (b, i, p)),
            pl.BlockSpec((None, seq, width), lambda b, p, i: (b, 0, groups + p)),
            pl.BlockSpec((None, seq, width), lambda b, p, i: (b, 0, 2 * groups + p)),
            pl.BlockSpec((1, width), lambda b, p, i: (0, p)),
            _const_spec((2 * SB_TILE, 4 * SB_TILE)),
        ],
        out_specs=pl.BlockSpec((None, SB_QROWS, width), lambda b, p, i: (b, i, p)),
        out_shape=jax.ShapeDtypeStruct((batch, seq, SB_WIDTH), BF16),
        scratch_shapes=[pltpu.VMEM((SB_PAIRS, nk, 2 * SB_TILE, LANES), BF16),
                        pltpu.VMEM((SB_PAIRS, nk, 2 * SB_TILE, LANES), BF16),
                        pltpu.VMEM((SB_QROWS, 2 * SB_TILE), F32),
                        pltpu.VMEM((SB_QROWS, LANES), F32)],
        compiler_params=_params("parallel", "parallel", "arbitrary"),
        name="sb_attention",
    )(qkv, qkv, qkv, out_g, _sb_cum_matrix())


def _rwkv_kernel(rw_ref, prev_ref, mu_ref, w0_ref, w2_ref, a0_ref, a2_ref, g2_ref, kk_ref, ka_ref,
                 rk_ref, lg_ref, lb_ref, cum_ref, grp_ref, o_ref, state_s):
    c = pl.program_id(1)
    rows = RWKV_ROWS
    ch = RWKV_CHUNK
    width = RWKV_WIDTH
    quad = RWKV_QUAD
    heads = quad // HEAD_DIM

    @pl.when(c == 0)
    def _():
        state_s[...] = jnp.zeros_like(state_s)

    def gsum(x):
        x = x.astype(BF16)
        return jnp.concatenate([_dot(x[:, q0:q0 + quad], grp_ref[...]) for q0 in range(0, width, quad)],
                               axis=1)

    p = rw_ref[...]
    prev_last = jnp.where(c == 0, 0.0, prev_ref[7:8, :])
    rowi = lax.broadcasted_iota(jnp.int32, p.shape, 0)
    shifted = jnp.where(rowi == 0, prev_last, pltpu.roll(p, 1, axis=0))
    p = p + (shifted - p) * mu_ref[...]
    r = p[:, :width]
    k = p[:, width:2 * width]
    v = p[:, 2 * width:3 * width]
    xwa = p[:, 3 * width:3 * width + LANES]
    xg = p[:, 3 * width + LANES:]
    lane = lax.broadcasted_iota(jnp.int32, xwa.shape, 1)
    xw = jnp.where(lane < DECAY_LORA, jnp.tanh(xwa), 0.0).astype(BF16)
    xa = jnp.where(lane < DECAY_LORA, 0.0, xwa).astype(BF16)
    wlin = w0_ref[...] + _dot(xw, w2_ref[...])
    wlog = -(jnp.maximum(-wlin, 0.0) + jnp.log(1.0 + jnp.exp(-jnp.abs(wlin)))) - 0.5
    lw = -jnp.exp(wlog)
    lr = jax.nn.sigmoid(a0_ref[...] + _dot(xa, a2_ref[...]))
    gate = _dot(jax.nn.sigmoid(xg).astype(BF16), g2_ref[...])
    kk = k * kk_ref[...]
    kk = kk * lax.rsqrt(jnp.maximum(gsum(kk * kk), 1e-24))
    k = k * (1.0 + (lr - 1.0) * ka_ref[...])
    a_vec = -kk
    b_vec = kk * lr

    cs = _dot(cum_ref[...], jnp.concatenate(_split2(lw), axis=1))
    g_in = cs[:, :width] + cs[:, width:]
    g_tot = jnp.concatenate([jnp.broadcast_to(g_in[c0 + ch - 1:c0 + ch, :], (ch, width))
                             for c0 in range(0, rows, ch)], axis=0)
    e_in = jnp.exp(g_in)
    e_inv = jnp.exp(-g_in)
    e_end = jnp.exp(g_tot - g_in)
    at_all = (a_vec * jnp.exp(g_in - lw)).astype(BF16)
    rt_all = (r * e_in).astype(BF16)
    bt_all = (b_vec * e_inv).astype(BF16)
    kt_all = (k * e_inv).astype(BF16)
    v_all = v.astype(BF16)
    bh_all = (b_vec * e_end).astype(BF16)
    kh_all = (k * e_end).astype(BF16)
    gam_all = jnp.exp(g_tot)

    t_i = lax.broadcasted_iota(jnp.int32, (ch, quad), 0)
    l_i = lax.broadcasted_iota(jnp.int32, (ch, quad), 1)
    i_i = l_i & (HEAD_DIM - 1)
    strict = i_i < t_i
    incl = i_i <= t_i
    eye = (i_i == t_i).astype(F32)
    in_head = [(l_i // HEAD_DIM) == h for h in range(heads)]
    rr = lax.broadcasted_iota(jnp.int32, (quad, quad), 0) // HEAD_DIM
    cc = lax.broadcasted_iota(jnp.int32, (quad, quad), 1) // HEAD_DIM
    same_head = rr == cc

    def bd(y):
        y = y.astype(BF16)
        zero = jnp.zeros_like(y)
        return jnp.concatenate([jnp.where(m, y, zero) for m in in_head], axis=0)

    def pp(x, y):
        return _dot(x.astype(BF16), bd(y))

    def unpack_diag(full):
        out = jnp.where(in_head[0], full[:ch], 0.0)
        for h in range(1, heads):
            out = out + jnp.where(in_head[h], full[h * ch:(h + 1) * ch], 0.0)
        return out

    def pp2(x, y0, y1):
        x = x.astype(BF16)
        y0 = y0.astype(BF16)
        y1 = y1.astype(BF16)
        half_lane = lax.broadcasted_iota(jnp.int32, (ch, LANES), 1) // HEAD_DIM
        in_half = [half_lane == h for h in range(LANES // HEAD_DIM)]
        outs = []
        for lo in range(0, quad, LANES):
            blocks = []
            for y in (y0, y1):
                part = y[:, lo:lo + LANES]
                zero = jnp.zeros_like(part)
                blocks.append(jnp.concatenate([jnp.where(m, part, zero) for m in in_half], axis=0))
            outs.append(_dot(x[:, lo:lo + LANES], jnp.concatenate(blocks, axis=1)))
        first = jnp.concatenate([o[:, :LANES] for o in outs], axis=1)
        second = jnp.concatenate([o[:, LANES:] for o in outs], axis=1)
        return jnp.concatenate([first, second], axis=1)

    def state_terms(x, v_, b, k_):
        left = jnp.concatenate([x.astype(BF16), jnp.concatenate([jnp.zeros_like(v_), v_], axis=1)], axis=0)
        return _dot_tn(left, jnp.concatenate([b, k_], axis=0))

    nq = width // quad
    combos = [(ci, qd) for ci in range(rows // ch) for qd in range(nq)]

    def cut(x, ci, qd):
        return x[ci * ch:(ci + 1) * ch, qd * quad:(qd + 1) * quad]

    def each(fn, *lists):
        return [fn(*args) for args in zip(*lists)]

    at = [cut(at_all, *cq) for cq in combos]
    rt = [cut(rt_all, *cq) for cq in combos]
    vv = [cut(v_all, *cq) for cq in combos]
    bh = [cut(bh_all, *cq) for cq in combos]
    kh = [cut(kh_all, *cq) for cq in combos]
    aa = [_dot_nt(jnp.concatenate([a, r_], axis=0),
                  jnp.concatenate([bd(cut(bt_all, *cq)), bd(cut(kt_all, *cq))], axis=0))
          for a, r_, cq in zip(at, rt, combos)]
    n_ab = [jnp.where(strict, x[:ch, :quad], 0.0) for x in aa]
    n_ak = [jnp.where(strict, x[:ch, quad:], 0.0) for x in aa]
    a_rb = [jnp.where(incl, x[ch:, :quad], 0.0) for x in aa]
    a_rk = [jnp.where(incl, x[ch:, quad:], 0.0) for x in aa]

    blk = 8
    diag = (t_i // blk) == (i_i // blk)
    nd = [jnp.where(diag, x, 0.0) for x in n_ab]
    n2 = each(pp, nd, nd)
    tm = [eye + x for x in nd]
    both = each(lambda t, n: _dot(jnp.concatenate([t, n], axis=0).astype(BF16), bd(n)), tm, n2)
    tm = each(lambda t, x: t + x[:ch], tm, both)
    tm = each(lambda t, x: t + pp(t, x[ch:]), tm, both)
    while blk < ch:
        off = ((t_i // (2 * blk)) == (i_i // (2 * blk))) & ((t_i // blk) != (i_i // blk))
        half = each(lambda t, n: pp(t, jnp.where(off, n, 0.0)), tm, n_ab)
        tm = each(lambda t, h: t + pp(h, t), tm, half)
        blk *= 2

    kv_both = each(lambda a, b, v_: _dot(jnp.concatenate([a, b], axis=0).astype(BF16), bd(v_)), n_ak, a_rk, vv)
    akv = [x[:ch] for x in kv_both]
    wu = each(pp2, tm, at, akv)
    wa = [x[:, :quad] for x in wu]
    uv = [x[:, quad:] for x in wu]
    ry = each(pp2, a_rb, wa, uv)
    rhat = each(lambda r_, x: r_.astype(F32) + x[:, :quad], rt, ry)
    yv = each(lambda x, y: x[:, quad:] + y[ch:], ry, kv_both)
    gs = each(state_terms, wu, vv, bh, kh)
    gc = [jnp.where(same_head, x[:quad], 0.0).astype(BF16) for x in gs]
    sv = [unpack_diag(x[quad:]) for x in gs]

    states = [state_s[qd] for qd in range(nq)]
    y_out = {}
    for n, (ci, qd) in enumerate(combos):
        s0 = states[qd]
        y_out[ci, qd] = _dot_nt(rhat[n].astype(BF16), bd(s0)) + yv[n]
        gam_row = gam_all[ci * ch:ci * ch + 1, qd * quad:(qd + 1) * quad]
        states[qd] = s0 * gam_row + _dot(s0.astype(BF16), gc[n]) + sv[n]
    for qd in range(nq):
        state_s[qd] = states[qd]
    y_rows = [jnp.concatenate([y_out[ci, qd] for qd in range(nq)], axis=1)
              for ci in range(rows // ch)]

    y = jnp.concatenate(y_rows, axis=0)
    mean = gsum(y) * (1.0 / HEAD_DIM)
    d = y - mean
    var = gsum(d * d) * (1.0 / HEAD_DIM)
    y = d * lax.rsqrt(var + LNX_EPS) * lg_ref[...] + lb_ref[...]
    bonus = gsum(r * k * rk_ref[...]) * v
    o_ref[...] = ((y + bonus) * gate).astype(BF16)


def _rwkv_cum_matrix():
    t = jnp.arange(RWKV_ROWS)[:, None]
    i = jnp.arange(RWKV_ROWS)[None, :]
    return (((t // RWKV_CHUNK) == (i // RWKV_CHUNK)) & (i <= t)).astype(BF16)


def _group_matrix():
    a = jnp.arange(RWKV_QUAD)
    return (a[:, None] // HEAD_DIM == a[None, :] // HEAD_DIM).astype(BF16)


def _rwkv(rw, batch, seq, mu, w0, w2, a0, a2, g2, k_k, k_a, r_k, lnx_g, lnx_b):
    rw = rw.reshape(batch, seq, RWKV_PAD)
    rows = RWKV_ROWS
    vec = lambda n: _const_spec((1, n))
    return pl.pallas_call(
        _rwkv_kernel,
        grid=(batch, seq // rows),
        in_specs=[
            pl.BlockSpec((None, rows, RWKV_PAD), lambda b, c: (b, c, 0)),
            pl.BlockSpec((None, 8, RWKV_PAD), lambda b, c: (b, jnp.maximum(c * (rows // 8) - 1, 0), 0)),
            vec(RWKV_PAD), vec(RWKV_WIDTH), _const_spec((LANES, RWKV_WIDTH)), vec(RWKV_WIDTH),
            _const_spec((LANES, RWKV_WIDTH)), _const_spec((GATE_PAD, RWKV_WIDTH)),
            vec(RWKV_WIDTH), vec(RWKV_WIDTH), vec(RWKV_WIDTH), vec(RWKV_WIDTH), vec(RWKV_WIDTH),
            _const_spec((rows, rows)), _const_spec((RWKV_QUAD, RWKV_QUAD)),
        ],
        out_specs=pl.BlockSpec((None, rows, RWKV_WIDTH), lambda b, c: (b, c, 0)),
        out_shape=jax.ShapeDtypeStruct((batch, seq, RWKV_WIDTH), BF16),
        scratch_shapes=[pltpu.VMEM((RWKV_WIDTH // RWKV_QUAD, RWKV_CHUNK, RWKV_QUAD), F32)],
        compiler_params=_params("parallel", "arbitrary"),
        name="rwkv7",
    )(rw, rw, mu, w0, w2, a0, a2, g2, k_k, k_a, r_k, lnx_g, lnx_b,
      _rwkv_cum_matrix(), _group_matrix())


def _row(a):
    return a.reshape(1, -1).astype(F32)


def _pad_rows(w, n):
    return jnp.pad(w, ((0, n - w.shape[0]), (0, 0)))


def kernel(x, mem, ffn1_pre, ffn1_post, ffn1_w_in, ffn1_w_out, mix_pre, mix_post, mix_w_in, rwkv_mu, rwkv_w0, rwkv_w2, rwkv_a0, rwkv_a2, rwkv_g2, rwkv_k_k, rwkv_k_a, rwkv_r_k, rwkv_lnx_g, rwkv_lnx_b, sb_out_g, mix_w_out, mem_pre, mem_post, mem_kv_g, mem_w_q, mem_w_kv, mem_w_o, ffn2_pre, ffn2_post, ffn2_w_in, ffn2_w_out):
    batch, seq, _ = x.shape
    depth = ffn1_pre.shape[0]
    h = x.reshape(batch * seq, D_MODEL)
    mem2 = mem.reshape(-1, D_MODEL)
    pad_cols = RWKV_PAD - RWKV_IN
    for l in range(depth):
        w_mix_in = jnp.pad(mix_w_in[l].astype(BF16), ((0, 0), (0, pad_cols)))
        h, qkv, rw = _pre_mix(h, _row(ffn1_pre[l]), _row(ffn1_post[l]), ffn1_w_in[l].astype(BF16),
                              ffn1_w_out[l].astype(BF16), _row(mix_pre[l]), w_mix_in)
        sb_o = _sb_attention(qkv, _row(sb_out_g[l]), batch, seq)
        w2 = _pad_rows(rwkv_w2[l], LANES).astype(BF16)
        a2 = jnp.pad(rwkv_a2[l], ((DECAY_LORA, 0), (0, 0))).astype(BF16)
        g2 = _pad_rows(rwkv_g2[l], GATE_PAD).astype(BF16)
        mu = jnp.pad(rwkv_mu[l], (0, pad_cols))
        rw_o = _rwkv(rw, batch, seq, _row(mu), _row(rwkv_w0[l]), w2, _row(rwkv_a0[l]), a2, g2,
                     _row(rwkv_k_k[l]), _row(rwkv_k_a[l]), _row(rwkv_r_k[l]),
                     _row(rwkv_lnx_g[l]), _row(rwkv_lnx_b[l]))
        kv = _mem_kv(mem2, _row(mem_kv_g[l]), mem_w_kv[l].astype(BF16))
        h = _post_mix(sb_o, rw_o, h, kv, mix_w_out[l].astype(BF16), _row(mix_post[l]),
                      _row(mem_pre[l]), mem_w_q[l].astype(BF16), mem_w_o[l].astype(BF16),
                      _row(mem_post[l]), _row(ffn2_pre[l]), _row(ffn2_post[l]),
                      ffn2_w_in[l].astype(BF16), ffn2_w_out[l].astype(BF16), batch, seq)
    return h.reshape(batch, seq, D_MODEL)
```

```python
import functools

import jax
import jax.numpy as jnp
from jax import lax
from jax.experimental import pallas as pl
from jax.experimental.pallas import tpu as pltpu

D_MODEL = 1024
HEAD_DIM = 64
SB_HEADS = 8
RWKV_HEADS = 8
SB_WIDTH = SB_HEADS * HEAD_DIM
RWKV_WIDTH = RWKV_HEADS * HEAD_DIM
MIX_WIDTH = SB_WIDTH + RWKV_WIDTH
DECAY_LORA = 64
AAA_LORA = 64
GATE_LORA = 160
RWKV_IN = 3 * RWKV_WIDTH + DECAY_LORA + AAA_LORA + GATE_LORA
MEM_HEADS = 4
MEM_HEAD_DIM = 128
MEM_WIDTH = MEM_HEADS * MEM_HEAD_DIM
D_FF = 2816
NORM_EPS = 1e-6
LNX_EPS = 64e-5

LANES = 128
SUBLANES = 8
MXU_DIM = 256
V7X_VMEM_BYTES = 64 * 1024 * 1024
VMEM_LIMIT_BYTES = V7X_VMEM_BYTES - 8 * 1024 * 1024

RWKV_PAD = 15 * LANES
GATE_PAD = RWKV_PAD - (3 * RWKV_WIDTH + LANES)
SB_TILE = 128
SB_QROWS = 512
SB_CARRY_STOP = 112.0
SB_FIXED_WAVES = 3
SB_PAIRS = 2
LOG2E = 1.4426950408889634
RWKV_CHUNK = 64
RWKV_ROWS = 512
RWKV_QUAD = MXU_DIM
TOKEN_TILE = 512
FF_CHUNK = 256
ROW_PARTS = 2

BF16 = jnp.bfloat16
F32 = jnp.float32


def _const_spec(shape):
    nd = len(shape)
    return pl.BlockSpec(shape, lambda *_: (0,) * nd, pipeline_mode=pl.Buffered(1))


def _params(*sem):
    return pltpu.CompilerParams(dimension_semantics=sem, vmem_limit_bytes=VMEM_LIMIT_BYTES)


def _rms(x, g):
    ms = jnp.mean(x * x, axis=-1, keepdims=True)
    return x * lax.rsqrt(ms + NORM_EPS) * g


def _dot(a, b):
    return jnp.dot(a, b, preferred_element_type=F32)


def _dot_nt(a, b):
    return lax.dot_general(a, b, (((1,), (1,)), ((), ())), preferred_element_type=F32)


def _dot_tn(a, b):
    return lax.dot_general(a, b, (((0,), (0,)), ((), ())), preferred_element_type=F32)


def _split2(x):
    hi = x.astype(BF16)
    lo = (x - hi.astype(F32)).astype(BF16)
    return hi, lo


def _pre_mix_kernel(x_ref, fpre_ref, fpost_ref, win_ref, wout_ref, mpre_ref, wmix_ref,
                    h_ref, qkv_ref, rw_ref, act_ref):
    x = x_ref[...]
    xn = _rms(x, fpre_ref[...]).astype(BF16)
    for c in range(D_FF // FF_CHUNK):
        lo = c * FF_CHUNK
        gate = _dot(xn, win_ref[:, lo:lo + FF_CHUNK])
        up = _dot(xn, win_ref[:, D_FF + lo:D_FF + lo + FF_CHUNK])
        act_ref[:, lo:lo + FF_CHUNK] = (gate * jax.nn.sigmoid(gate) * up).astype(BF16)
    tm = x.shape[0]
    parts = [slice(lo, lo + tm // ROW_PARTS) for lo in range(0, tm, tm // ROW_PARTS)]
    hn = []
    for rows in parts:
        y = _dot(act_ref[rows, :], wout_ref[...])
        h = x[rows, :] + 0.5 * _rms(y, fpost_ref[...])
        h_ref[rows, :] = h
        hn.append(_rms(h, mpre_ref[...]).astype(BF16))
    for rows, part in zip(parts, hn):
        qkv_ref[rows, :] = _dot(part, wmix_ref[:, :3 * SB_WIDTH]).astype(BF16)
        rw_ref[rows, :] = _dot(part, wmix_ref[:, 3 * SB_WIDTH:])


def _pre_mix(x, fpre, fpost, w_in, w_out, mpre, w_mix):
    t = x.shape[0]
    tm = min(TOKEN_TILE, t)
    row = lambda i: (i, 0)
    vec = _const_spec((1, D_MODEL))
    return pl.pallas_call(
        _pre_mix_kernel,
        grid=(t // tm,),
        in_specs=[
            pl.BlockSpec((tm, D_MODEL), row), vec, vec,
            _const_spec((D_MODEL, 2 * D_FF)), _const_spec((D_FF, D_MODEL)),
            vec, _const_spec((D_MODEL, 3 * SB_WIDTH + RWKV_PAD)),
        ],
        out_specs=[pl.BlockSpec((tm, D_MODEL), row), pl.BlockSpec((tm, 3 * SB_WIDTH), row),
                   pl.BlockSpec((tm, RWKV_PAD), row)],
        out_shape=[jax.ShapeDtypeStruct((t, D_MODEL), F32),
                   jax.ShapeDtypeStruct((t, 3 * SB_WIDTH), BF16),
                   jax.ShapeDtypeStruct((t, RWKV_PAD), F32)],
        scratch_shapes=[pltpu.VMEM((tm, D_FF), BF16)],
        compiler_params=_params("parallel"),
        name="ffn1_mix_in",
    )(x, fpre, fpost, w_in, w_out, mpre, w_mix)


def _interleave(*stages):
    gens = [g for g, _ in stages]
    counts = [n for _, n in stages]
    done = [0] * len(gens)
    live = [True] * len(gens)
    while any(live):
        which = min((n for n in range(len(gens)) if live[n]), key=lambda n: (done[n] + 1) / counts[n])
        try:
            next(gens[which])
            done[which] += 1
        except StopIteration:
            live[which] = False


def _post_mix_kernel(sb_ref, rw_ref, h_ref, kv_ref, wmix_ref, mixpost_ref, mempre_ref, wq_ref, wo_ref,
                     mempost_ref, fpre_ref, fpost_ref, win_ref, wout_ref, o_ref, act_ref, hs_ref, xn_ref):
    step = pl.program_id(0)

    @pl.when(step == 0)
    def _():
        hs_ref[...] = jnp.zeros_like(hs_ref)
        xn_ref[...] = jnp.zeros_like(xn_ref)

    result = {}

    def ffn():
        xn = xn_ref[...]
        for c in range(D_FF // FF_CHUNK):
            lo = c * FF_CHUNK
            gate = _dot(xn, win_ref[:, lo:lo + FF_CHUNK])
            up = _dot(xn, win_ref[:, D_FF + lo:D_FF + lo + FF_CHUNK])
            act_ref[:, lo:lo + FF_CHUNK] = (gate * jax.nn.sigmoid(gate) * up).astype(BF16)
            yield
        tm = hs_ref.shape[0]
        for lo in range(0, tm, tm // ROW_PARTS):
            rows = slice(lo, lo + tm // ROW_PARTS)
            y = _dot(act_ref[rows, :], wout_ref[...])
            o_ref[rows, :] = hs_ref[rows, :] + 0.5 * _rms(y, fpost_ref[...])
            yield

    def attn():
        mixed = _dot(sb_ref[...], wmix_ref[:SB_WIDTH, :]) + _dot(rw_ref[...], wmix_ref[SB_WIDTH:, :])
        yield
        h = h_ref[...] + _rms(mixed, mixpost_ref[...])
        q = _dot(_rms(h, mempre_ref[...]).astype(BF16), wq_ref[...])
        q = (q * (MEM_HEAD_DIM ** -0.5)).astype(BF16)
        yield
        outs = []
        for hd in range(MEM_HEADS):
            lo = hd * MEM_HEAD_DIM
            s = _dot_nt(q[:, lo:lo + MEM_HEAD_DIM], kv_ref[:, lo:lo + MEM_HEAD_DIM])
            e = jnp.exp(s - jnp.max(s, axis=-1, keepdims=True))
            pr = e / jnp.sum(e, axis=-1, keepdims=True)
            outs.append(_dot(pr.astype(BF16), kv_ref[:, MEM_WIDTH + lo:MEM_WIDTH + lo + MEM_HEAD_DIM]))
            yield
        o = jnp.concatenate(outs, axis=1).astype(BF16)
        h = h + _rms(_dot(o, wo_ref[...]), mempost_ref[...])
        yield
        result["h"] = h
        result["xn"] = _rms(h, fpre_ref[...]).astype(BF16)

    _interleave((ffn(), D_FF // FF_CHUNK + ROW_PARTS), (attn(), MEM_HEADS + 4))
    hs_ref[...] = result["h"]
    xn_ref[...] = result["xn"]


def _post_mix(sb_o, rw_o, h, kv, w_mix, mix_post, mem_pre, w_q, w_o, mem_post, fpre, fpost, w_in, w_out,
              batch, seq):
    mem_len = kv.shape[0] // batch
    kv = kv.reshape(batch, mem_len, 2 * MEM_WIDTH)
    t = batch * seq
    sb_o = sb_o.reshape(t, SB_WIDTH)
    rw_o = rw_o.reshape(t, RWKV_WIDTH)
    tm = min(TOKEN_TILE, seq)
    n = t // tm
    per_batch = seq // tm
    cur = lambda s: jnp.minimum(s, n - 1)
    tile = lambda width: pl.BlockSpec((tm, width), lambda s: (cur(s), 0))
    vec = _const_spec((1, D_MODEL))
    return pl.pallas_call(
        _post_mix_kernel,
        grid=(n + 1,),
        in_specs=[
            tile(SB_WIDTH), tile(RWKV_WIDTH), tile(D_MODEL),
            pl.BlockSpec((None, mem_len, 2 * MEM_WIDTH), lambda s: (cur(s) // per_batch, 0, 0)),
            _const_spec((MIX_WIDTH, D_MODEL)), vec, vec,
            _const_spec((D_MODEL, MEM_WIDTH)), _const_spec((MEM_WIDTH, D_MODEL)), vec,
            vec, vec, _const_spec((D_MODEL, 2 * D_FF)), _const_spec((D_FF, D_MODEL)),
        ],
        out_specs=pl.BlockSpec((tm, D_MODEL), lambda s: (jnp.maximum(s - 1, 0), 0)),
        out_shape=jax.ShapeDtypeStruct((t, D_MODEL), F32),
        scratch_shapes=[pltpu.VMEM((tm, D_FF), BF16), pltpu.VMEM((tm, D_MODEL), F32),
                        pltpu.VMEM((tm, D_MODEL), BF16)],
        compiler_params=_params("arbitrary"),
        name="mix_out_mem_ffn2",
    )(sb_o, rw_o, h, kv, w_mix, mix_post, mem_pre, w_q, w_o, mem_post, fpre, fpost, w_in, w_out)


def _mem_kv_kernel(m_ref, g_ref, w_ref, o_ref):
    o_ref[...] = _dot(_rms(m_ref[...], g_ref[...]).astype(BF16), w_ref[...]).astype(BF16)


def _mem_kv(mem, g, w):
    t = mem.shape[0]
    tm = min(TOKEN_TILE, t)
    row = lambda i: (i, 0)
    return pl.pallas_call(
        _mem_kv_kernel,
        grid=(t // tm,),
        in_specs=[pl.BlockSpec((tm, D_MODEL), row), _const_spec((1, D_MODEL)),
                  _const_spec((D_MODEL, 2 * MEM_WIDTH))],
        out_specs=pl.BlockSpec((tm, 2 * MEM_WIDTH), row),
        out_shape=jax.ShapeDtypeStruct((t, 2 * MEM_WIDTH), BF16),
        compiler_params=_params("parallel"),
        name="mem_kv",
    )(mem, g, w)


def _sb_pair(q_ref, k_ref, v_ref, g_ref, cum_ref, o_ref, k2_ref, v2_ref, carry_ref, acc_ref, *, seq):
    i = pl.program_id(2)
    nk = seq // SB_TILE
    tq = SB_QROWS
    sub = tq // SB_TILE
    lane = lax.broadcasted_iota(jnp.int32, (SB_TILE, LANES), 1)
    head0 = lane < HEAD_DIM

    @pl.when(i == 0)
    def _():
        for j in range(nk):
            kb = k_ref[j * SB_TILE:(j + 1) * SB_TILE, :] * jnp.asarray(HEAD_DIM ** -0.5, BF16)
            vb = v_ref[j * SB_TILE:(j + 1) * SB_TILE, :]
            zero = jnp.zeros_like(kb)
            k2_ref[j, :SB_TILE, :] = jnp.where(head0, kb, zero)
            k2_ref[j, SB_TILE:, :] = jnp.where(head0, zero, kb)
            v2_ref[j, :SB_TILE, :] = jnp.where(head0, vb, zero)
            v2_ref[j, SB_TILE:, :] = jnp.where(head0, zero, vb)

    cum = cum_ref[...]
    row = lax.broadcasted_iota(jnp.int32, (SB_TILE, 2 * SB_TILE), 0)
    col = lax.broadcasted_iota(jnp.int32, (SB_TILE, 2 * SB_TILE), 1) & (SB_TILE - 1)
    causal = col < row

    def scores(blocks, tiles, diagonal, valid=None):
        z = jnp.concatenate([_dot_nt(q_ref[rb * SB_TILE:(rb + 1) * SB_TILE, :], k2_ref[j])
                             for rb, j in zip(blocks, tiles)], axis=0)
        e = jnp.exp2(jnp.abs(z) * (-LOG2E))
        sp = jnp.maximum(z, 0.0) + jnp.log(1.0 + e)
        lsig = z - sp
        keep = None
        if diagonal:
            keep = jnp.concatenate([causal] * len(blocks), axis=0)
        if valid is not None:
            keep = valid if keep is None else jnp.logical_and(keep, valid)
        if keep is not None:
            sp = jnp.where(keep, sp, 0.0)
        return lsig, _dot(sp.astype(BF16), cum), keep

    def weights(blocks, tiles, lsig, r, keep):
        lo, hi = blocks[0] * SB_TILE, (blocks[-1] + 1) * SB_TILE
        carry = carry_ref[lo:hi, :]
        a = jnp.exp(lsig - r[:, :2 * SB_TILE] - carry)
        if keep is not None:
            a = jnp.where(keep, a, 0.0)
        a = a.astype(BF16)
        out = jnp.concatenate([_dot(a[n * SB_TILE:(n + 1) * SB_TILE, :], v2_ref[j])
                               for n, j in enumerate(tiles)], axis=0)
        carry_ref[lo:hi, :] = carry + r[:, 2 * SB_TILE:]
        acc_ref[lo:hi, :] += out

    def waves(plan):
        sc = [scores(*w) for w in plan]
        for (blocks, tiles, _), (lsig, r, keep) in zip(plan, sc):
            weights(blocks, tiles, lsig, r, keep)

    carry_ref[...] = jnp.zeros_like(carry_ref)
    acc_ref[...] = jnp.zeros_like(acc_ref)
    first = sub * i
    everyone = list(range(sub))

    @pl.when(i == 0)
    def _():
        waves([(everyone[d:], [rb - d for rb in everyone[d:]], d == 0) for d in range(sub)])

    @pl.when(i > 0)
    def _():
        waves([(everyone, [first + rb - d for rb in everyone], d == 0) for d in range(SB_FIXED_WAVES)])

    rb_of_row = lax.broadcasted_iota(jnp.int32, (tq, 1), 0) // SB_TILE

    def cond(state):
        d, smallest = state
        return jnp.logical_and(jnp.logical_and(i > 0, d < first + sub), smallest < SB_CARRY_STOP)

    def body(state):
        d, _ = state
        tiles = [jnp.maximum(first + rb - d, 0) for rb in everyone]
        lsig, r, keep = scores(everyone, tiles, False, (first + rb_of_row - d) >= 0)
        weights(everyone, tiles, lsig, r, keep)
        return d + 1, jnp.min(carry_ref[...])

    lax.while_loop(cond, body, (jnp.int32(SB_FIXED_WAVES), jnp.min(carry_ref[...])))

    acc = acc_ref[...]
    h0 = lax.broadcasted_iota(jnp.int32, acc.shape, 1) < HEAD_DIM
    sq = acc * acc
    ms0 = jnp.sum(jnp.where(h0, sq, 0.0), axis=-1, keepdims=True)
    ms1 = jnp.sum(jnp.where(h0, 0.0, sq), axis=-1, keepdims=True)
    ms = jnp.where(h0, ms0, ms1) * (1.0 / HEAD_DIM)
    o_ref[...] = (acc * lax.rsqrt(ms + NORM_EPS) * g_ref[...]).astype(BF16)


def _sb_kernel(q_ref, k_ref, v_ref, g_ref, cum_ref, o_ref, k2_ref, v2_ref, carry_ref, acc_ref, *, seq):
    for p in range(SB_PAIRS):
        cols = pl.ds(p * LANES, LANES)
        _sb_pair(q_ref.at[:, cols], k_ref.at[:, cols], v_ref.at[:, cols], g_ref.at[:, cols], cum_ref,
                 o_ref.at[:, cols], k2_ref.at[p], v2_ref.at[p], carry_ref, acc_ref, seq=seq)


def _sb_cum_matrix():
    n = 2 * SB_TILE
    j = jnp.arange(n)[:, None]
    s = jnp.arange(n)[None, :]
    same = (j // SB_TILE) == (s // SB_TILE)
    tri = same & (j > s)
    return jnp.concatenate([tri, same], axis=1).astype(BF16)


def _sb_attention(qkv, out_g, batch, seq):
    assert seq % SB_QROWS == 0 and SB_FIXED_WAVES <= SB_QROWS // SB_TILE
    qkv = qkv.reshape(batch, seq, 3 * SB_WIDTH)
    width = SB_PAIRS * LANES
    groups = SB_WIDTH // width
    nk = seq // SB_TILE
    kern = functools.partial(_sb_kernel, seq=seq)
    return pl.pallas_call(
        kern,
        grid=(batch, groups, seq // SB_QROWS),
        in_specs=[
            pl.BlockSpec((None, SB_QROWS, width), lambda b, p, i: (b, i, p)),
            pl.BlockSpec((None, seq, width), lambda b, p, i: (b, 0, groups + p)),
            pl.BlockSpec((None, seq, width), lambda b, p, i: (b, 0, 2 * groups + p)),
            pl.BlockSpec((1, width), lambda b, p, i: (0, p)),
            _const_spec((2 * SB_TILE, 4 * SB_TILE)),
        ],
        out_specs=pl.BlockSpec((None, SB_QROWS, width), lambda b, p, i: (b, i, p)),
        out_shape=jax.ShapeDtypeStruct((batch, seq, SB_WIDTH), BF16),
        scratch_shapes=[pltpu.VMEM((SB_PAIRS, nk, 2 * SB_TILE, LANES), BF16),
                        pltpu.VMEM((SB_PAIRS, nk, 2 * SB_TILE, LANES), BF16),
                        pltpu.VMEM((SB_QROWS, 2 * SB_TILE), F32),
                        pltpu.VMEM((SB_QROWS, LANES), F32)],
        compiler_params=_params("parallel", "parallel", "arbitrary"),
        name="sb_attention",
    )(qkv, qkv, qkv, out_g, _sb_cum_matrix())


def _rwkv_kernel(rw_ref, prev_ref, mu_ref, w0_ref, w2_ref, a0_ref, a2_ref, g2_ref, kk_ref, ka_ref,
                 rk_ref, lg_ref, lb_ref, cum_ref, grp_ref, o_ref, state_s):
    c = pl.program_id(1)
    rows = RWKV_ROWS
    ch = RWKV_CHUNK
    width = RWKV_WIDTH
    quad = RWKV_QUAD
    heads = quad // HEAD_DIM

    @pl.when(c == 0)
    def _():
        state_s[...] = jnp.zeros_like(state_s)

    def gsum(x):
        x = x.astype(BF16)
        return jnp.concatenate([_dot(x[:, q0:q0 + quad], grp_ref[...]) for q0 in range(0, width, quad)],
                               axis=1)

    p = rw_ref[...]
    prev_last = jnp.where(c == 0, 0.0, prev_ref[SUBLANES - 1:, :])
    rowi = lax.broadcasted_iota(jnp.int32, p.shape, 0)
    shifted = jnp.where(rowi == 0, prev_last, pltpu.roll(p, 1, axis=0))
    p = p + (shifted - p) * mu_ref[...]
    r = p[:, :width]
    k = p[:, width:2 * width]
    v = p[:, 2 * width:3 * width]
    xwa = p[:, 3 * width:3 * width + LANES]
    xg = p[:, 3 * width + LANES:]
    lane = lax.broadcasted_iota(jnp.int32, xwa.shape, 1)
    xw = jnp.where(lane < DECAY_LORA, jnp.tanh(xwa), 0.0).astype(BF16)
    xa = jnp.where(lane < DECAY_LORA, 0.0, xwa).astype(BF16)
    wlin = w0_ref[...] + _dot(xw, w2_ref[...])
    wlog = -(jnp.maximum(-wlin, 0.0) + jnp.log(1.0 + jnp.exp(-jnp.abs(wlin)))) - 0.5
    lw = -jnp.exp(wlog)
    lr = jax.nn.sigmoid(a0_ref[...] + _dot(xa, a2_ref[...]))
    gate = _dot(jax.nn.sigmoid(xg).astype(BF16), g2_ref[...])
    kk = k * kk_ref[...]
    kk = kk * lax.rsqrt(jnp.maximum(gsum(kk * kk), 1e-24))
    k = k * (1.0 + (lr - 1.0) * ka_ref[...])
    a_vec = -kk
    b_vec = kk * lr

    cs = _dot(cum_ref[...], jnp.concatenate(_split2(lw), axis=1))
    g_in = cs[:, :width] + cs[:, width:]
    g_tot = jnp.concatenate([jnp.broadcast_to(g_in[c0 + ch - 1:c0 + ch, :], (ch, width))
                             for c0 in range(0, rows, ch)], axis=0)
    e_in = jnp.exp(g_in)
    e_inv = jnp.exp(-g_in)
    e_end = jnp.exp(g_tot - g_in)
    at_all = (a_vec * jnp.exp(g_in - lw)).astype(BF16)
    rt_all = (r * e_in).astype(BF16)
    bt_all = (b_vec * e_inv).astype(BF16)
    kt_all = (k * e_inv).astype(BF16)
    v_all = v.astype(BF16)
    bh_all = (b_vec * e_end).astype(BF16)
    kh_all = (k * e_end).astype(BF16)
    gam_all = jnp.exp(g_tot)

    t_i = lax.broadcasted_iota(jnp.int32, (ch, quad), 0)
    l_i = lax.broadcasted_iota(jnp.int32, (ch, quad), 1)
    i_i = l_i & (HEAD_DIM - 1)
    strict = i_i < t_i
    incl = i_i <= t_i
    eye = (i_i == t_i).astype(F32)
    in_head = [(l_i // HEAD_DIM) == h for h in range(heads)]
    rr = lax.broadcasted_iota(jnp.int32, (quad, quad), 0) // HEAD_DIM
    cc = lax.broadcasted_iota(jnp.int32, (quad, quad), 1) // HEAD_DIM
    same_head = rr == cc

    def bd(y):
        y = y.astype(BF16)
        zero = jnp.zeros_like(y)
        return jnp.concatenate([jnp.where(m, y, zero) for m in in_head], axis=0)

    def pp(x, y):
        return _dot(x.astype(BF16), bd(y))

    def unpack_diag(full):
        out = jnp.where(in_head[0], full[:ch], 0.0)
        for h in range(1, heads):
            out = out + jnp.where(in_head[h], full[h * ch:(h + 1) * ch], 0.0)
        return out

    def pp2(x, y0, y1):
        x = x.astype(BF16)
        y0 = y0.astype(BF16)
        y1 = y1.astype(BF16)
        half_lane = lax.broadcasted_iota(jnp.int32, (ch, LANES), 1) // HEAD_DIM
        in_half = [half_lane == h for h in range(LANES // HEAD_DIM)]
        outs = []
        for lo in range(0, quad, LANES):
            blocks = []
            for y in (y0, y1):
                part = y[:, lo:lo + LANES]
                zero = jnp.zeros_like(part)
                blocks.append(jnp.concatenate([jnp.where(m, part, zero) for m in in_half], axis=0))
            outs.append(_dot(x[:, lo:lo + LANES], jnp.concatenate(blocks, axis=1)))
        first = jnp.concatenate([o[:, :LANES] for o in outs], axis=1)
        second = jnp.concatenate([o[:, LANES:] for o in outs], axis=1)
        return jnp.concatenate([first, second], axis=1)

    def state_terms(x, v_, b, k_):
        left = jnp.concatenate([x.astype(BF16), jnp.concatenate([jnp.zeros_like(v_), v_], axis=1)], axis=0)
        return _dot_tn(left, jnp.concatenate([b, k_], axis=0))

    nq = width // quad
    combos = [(ci, qd) for ci in range(rows // ch) for qd in range(nq)]

    def cut(x, ci, qd):
        return x[ci * ch:(ci + 1) * ch, qd * quad:(qd + 1) * quad]

    def each(fn, *lists):
        return [fn(*args) for args in zip(*lists)]

    at = [cut(at_all, *cq) for cq in combos]
    rt = [cut(rt_all, *cq) for cq in combos]
    vv = [cut(v_all, *cq) for cq in combos]
    bh = [cut(bh_all, *cq) for cq in combos]
    kh = [cut(kh_all, *cq) for cq in combos]
    aa = [_dot_nt(jnp.concatenate([a, r_], axis=0),
                  jnp.concatenate([bd(cut(bt_all, *cq)), bd(cut(kt_all, *cq))], axis=0))
          for a, r_, cq in zip(at, rt, combos)]
    n_ab = [jnp.where(strict, x[:ch, :quad], 0.0) for x in aa]
    n_ak = [jnp.where(strict, x[:ch, quad:], 0.0) for x in aa]
    a_rb = [jnp.where(incl, x[ch:, :quad], 0.0) for x in aa]
    a_rk = [jnp.where(incl, x[ch:, quad:], 0.0) for x in aa]

    blk = 8
    diag = (t_i // blk) == (i_i // blk)
    nd = [jnp.where(diag, x, 0.0) for x in n_ab]
    n2 = each(pp, nd, nd)
    tm = [eye + x for x in nd]
    both = each(lambda t, n: _dot(jnp.concatenate([t, n], axis=0).astype(BF16), bd(n)), tm, n2)
    tm = each(lambda t, x: t + x[:ch], tm, both)
    tm = each(lambda t, x: t + pp(t, x[ch:]), tm, both)
    while blk < ch:
        off = ((t_i // (2 * blk)) == (i_i // (2 * blk))) & ((t_i // blk) != (i_i // blk))
        half = each(lambda t, n: pp(t, jnp.where(off, n, 0.0)), tm, n_ab)
        tm = each(lambda t, h: t + pp(h, t), tm, half)
        blk *= 2

    kv_both = each(lambda a, b, v_: _dot(jnp.concatenate([a, b], axis=0).astype(BF16), bd(v_)), n_ak, a_rk, vv)
    akv = [x[:ch] for x in kv_both]
    wu = each(pp2, tm, at, akv)
    wa = [x[:, :quad] for x in wu]
    uv = [x[:, quad:] for x in wu]
    ry = each(pp2, a_rb, wa, uv)
    rhat = each(lambda r_, x: r_.astype(F32) + x[:, :quad], rt, ry)
    yv = each(lambda x, y: x[:, quad:] + y[ch:], ry, kv_both)
    gs = each(state_terms, wu, vv, bh, kh)
    gc = [jnp.where(same_head, x[:quad], 0.0).astype(BF16) for x in gs]
    sv = [unpack_diag(x[quad:]) for x in gs]

    states = [state_s[qd] for qd in range(nq)]
    y_out = {}
    for n, (ci, qd) in enumerate(combos):
        s0 = states[qd]
        y_out[ci, qd] = _dot_nt(rhat[n].astype(BF16), bd(s0)) + yv[n]
        gam_row = gam_all[ci * ch:ci * ch + 1, qd * quad:(qd + 1) * quad]
        states[qd] = s0 * gam_row + _dot(s0.astype(BF16), gc[n]) + sv[n]
    for qd in range(nq):
        state_s[qd] = states[qd]
    y_rows = [jnp.concatenate([y_out[ci, qd] for qd in range(nq)], axis=1)
              for ci in range(rows // ch)]

    y = jnp.concatenate(y_rows, axis=0)
    mean = gsum(y) * (1.0 / HEAD_DIM)
    d = y - mean
    var = gsum(d * d) * (1.0 / HEAD_DIM)
    y = d * lax.rsqrt(var + LNX_EPS) * lg_ref[...] + lb_ref[...]
    bonus = gsum(r * k * rk_ref[...]) * v
    o_ref[...] = ((y + bonus) * gate).astype(BF16)


def _rwkv_cum_matrix():
    t = jnp.arange(RWKV_ROWS)[:, None]
    i = jnp.arange(RWKV_ROWS)[None, :]
    return (((t // RWKV_CHUNK) == (i // RWKV_CHUNK)) & (i <= t)).astype(BF16)


def _group_matrix():
    a = jnp.arange(RWKV_QUAD)
    return (a[:, None] // HEAD_DIM == a[None, :] // HEAD_DIM).astype(BF16)


def _rwkv(rw, batch, seq, mu, w0, w2, a0, a2, g2, k_k, k_a, r_k, lnx_g, lnx_b):
    rw = rw.reshape(batch, seq, RWKV_PAD)
    rows = RWKV_ROWS
    vec = lambda n: _const_spec((1, n))
    return pl.pallas_call(
        _rwkv_kernel,
        grid=(batch, seq // rows),
        in_specs=[
            pl.BlockSpec((None, rows, RWKV_PAD), lambda b, c: (b, c, 0)),
            pl.BlockSpec((None, SUBLANES, RWKV_PAD),
                         lambda b, c: (b, jnp.maximum(c * (rows // SUBLANES) - 1, 0), 0)),
            vec(RWKV_PAD), vec(RWKV_WIDTH), _const_spec((LANES, RWKV_WIDTH)), vec(RWKV_WIDTH),
            _const_spec((LANES, RWKV_WIDTH)), _const_spec((GATE_PAD, RWKV_WIDTH)),
            vec(RWKV_WIDTH), vec(RWKV_WIDTH), vec(RWKV_WIDTH), vec(RWKV_WIDTH), vec(RWKV_WIDTH),
            _const_spec((rows, rows)), _const_spec((RWKV_QUAD, RWKV_QUAD)),
        ],
        out_specs=pl.BlockSpec((None, rows, RWKV_WIDTH), lambda b, c: (b, c, 0)),
        out_shape=jax.ShapeDtypeStruct((batch, seq, RWKV_WIDTH), BF16),
        scratch_shapes=[pltpu.VMEM((RWKV_WIDTH // RWKV_QUAD, RWKV_CHUNK, RWKV_QUAD), F32)],
        compiler_params=_params("parallel", "arbitrary"),
        name="rwkv7",
    )(rw, rw, mu, w0, w2, a0, a2, g2, k_k, k_a, r_k, lnx_g, lnx_b,
      _rwkv_cum_matrix(), _group_matrix())


def _row(a):
    return a.reshape(1, -1).astype(F32)


def _pad_rows(w, n):
    return jnp.pad(w, ((0, n - w.shape[0]), (0, 0)))


def kernel(x, mem, ffn1_pre, ffn1_post, ffn1_w_in, ffn1_w_out, mix_pre, mix_post, mix_w_in, rwkv_mu, rwkv_w0, rwkv_w2, rwkv_a0, rwkv_a2, rwkv_g2, rwkv_k_k, rwkv_k_a, rwkv_r_k, rwkv_lnx_g, rwkv_lnx_b, sb_out_g, mix_w_out, mem_pre, mem_post, mem_kv_g, mem_w_q, mem_w_kv, mem_w_o, ffn2_pre, ffn2_post, ffn2_w_in, ffn2_w_out):
    batch, seq, _ = x.shape
    depth = ffn1_pre.shape[0]
    h = x.reshape(batch * seq, D_MODEL)
    mem2 = mem.reshape(-1, D_MODEL)
    pad_cols = RWKV_PAD - RWKV_IN
    for l in range(depth):
        w_mix_in = jnp.pad(mix_w_in[l].astype(BF16), ((0, 0), (0, pad_cols)))
        h, qkv, rw = _pre_mix(h, _row(ffn1_pre[l]), _row(ffn1_post[l]), ffn1_w_in[l].astype(BF16),
                              ffn1_w_out[l].astype(BF16), _row(mix_pre[l]), w_mix_in)
        sb_o = _sb_attention(qkv, _row(sb_out_g[l]), batch, seq)
        w2 = _pad_rows(rwkv_w2[l], LANES).astype(BF16)
        a2 = jnp.pad(rwkv_a2[l], ((DECAY_LORA, 0), (0, 0))).astype(BF16)
        g2 = _pad_rows(rwkv_g2[l], GATE_PAD).astype(BF16)
        mu = jnp.pad(rwkv_mu[l], (0, pad_cols))
        rw_o = _rwkv(rw, batch, seq, _row(mu), _row(rwkv_w0[l]), w2, _row(rwkv_a0[l]), a2, g2,
                     _row(rwkv_k_k[l]), _row(rwkv_k_a[l]), _row(rwkv_r_k[l]),
                     _row(rwkv_lnx_g[l]), _row(rwkv_lnx_b[l]))
        kv = _mem_kv(mem2, _row(mem_kv_g[l]), mem_w_kv[l].astype(BF16))
        h = _post_mix(sb_o, rw_o, h, kv, mix_w_out[l].astype(BF16), _row(mix_post[l]),
                      _row(mem_pre[l]), mem_w_q[l].astype(BF16), mem_w_o[l].astype(BF16),
                      _row(mem_post[l]), _row(ffn2_pre[l]), _row(ffn2_post[l]),
                      ffn2_w_in[l].astype(BF16), ffn2_w_out[l].astype(BF16), batch, seq)
    return h.reshape(batch, seq, D_MODEL)
```

```python
import functools

import jax
import jax.numpy as jnp
from jax import lax
from jax.experimental import pallas as pl
from jax.experimental.pallas import tpu as pltpu

D_MODEL = 1024
HEAD_DIM = 64
SB_HEADS = 8
RWKV_HEADS = 8
SB_WIDTH = SB_HEADS * HEAD_DIM
RWKV_WIDTH = RWKV_HEADS * HEAD_DIM
MIX_WIDTH = SB_WIDTH + RWKV_WIDTH
DECAY_LORA = 64
AAA_LORA = 64
GATE_LORA = 160
RWKV_IN = 3 * RWKV_WIDTH + DECAY_LORA + AAA_LORA + GATE_LORA
MEM_HEADS = 4
MEM_HEAD_DIM = 128
MEM_WIDTH = MEM_HEADS * MEM_HEAD_DIM
D_FF = 2816
NORM_EPS = 1e-6
LNX_EPS = 64e-5

LANES = 128
SUBLANES = 8
MXU_DIM = 256
V7X_VMEM_BYTES = 64 * 1024 * 1024
VMEM_LIMIT_BYTES = V7X_VMEM_BYTES - 8 * 1024 * 1024

RWKV_PAD = 15 * LANES
GATE_PAD = RWKV_PAD - (3 * RWKV_WIDTH + LANES)
SB_TILE = 128
SB_QROWS = 512
SB_CARRY_STOP = 112.0
SB_FIXED_WAVES = 3
SB_PAIRS = 4
LOG2E = 1.4426950408889634
RWKV_CHUNK = 64
RWKV_ROWS = 512
RWKV_QUAD = MXU_DIM
TOKEN_TILE = 512
FF_CHUNK = 256
ROW_PARTS = 2

BF16 = jnp.bfloat16
F32 = jnp.float32


def _const_spec(shape):
    nd = len(shape)
    return pl.BlockSpec(shape, lambda *_: (0,) * nd, pipeline_mode=pl.Buffered(1))


def _params(*sem):
    return pltpu.CompilerParams(dimension_semantics=sem, vmem_limit_bytes=VMEM_LIMIT_BYTES)


def _rms(x, g):
    ms = jnp.mean(x * x, axis=-1, keepdims=True)
    return x * lax.rsqrt(ms + NORM_EPS) * g


def _dot(a, b):
    return jnp.dot(a, b, preferred_element_type=F32)


def _dot_nt(a, b):
    return lax.dot_general(a, b, (((1,), (1,)), ((), ())), preferred_element_type=F32)


def _dot_tn(a, b):
    return lax.dot_general(a, b, (((0,), (0,)), ((), ())), preferred_element_type=F32)


def _split2(x):
    hi = x.astype(BF16)
    lo = (x - hi.astype(F32)).astype(BF16)
    return hi, lo


def _pre_mix_kernel(x_ref, fpre_ref, fpost_ref, win_ref, wout_ref, mpre_ref, wmix_ref,
                    h_ref, qkv_ref, rw_ref, act_ref):
    x = x_ref[...]
    xn = _rms(x, fpre_ref[...]).astype(BF16)
    for c in range(D_FF // FF_CHUNK):
        lo = c * FF_CHUNK
        gate = _dot(xn, win_ref[:, lo:lo + FF_CHUNK])
        up = _dot(xn, win_ref[:, D_FF + lo:D_FF + lo + FF_CHUNK])
        act_ref[:, lo:lo + FF_CHUNK] = (gate * jax.nn.sigmoid(gate) * up).astype(BF16)
    tm = x.shape[0]
    parts = [slice(lo, lo + tm // ROW_PARTS) for lo in range(0, tm, tm // ROW_PARTS)]
    hn = []
    for rows in parts:
        y = _dot(act_ref[rows, :], wout_ref[...])
        h = x[rows, :] + 0.5 * _rms(y, fpost_ref[...])
        h_ref[rows, :] = h
        hn.append(_rms(h, mpre_ref[...]).astype(BF16))
    for rows, part in zip(parts, hn):
        qkv_ref[rows, :] = _dot(part, wmix_ref[:, :3 * SB_WIDTH]).astype(BF16)
        rw_ref[rows, :] = _dot(part, wmix_ref[:, 3 * SB_WIDTH:])


def _pre_mix(x, fpre, fpost, w_in, w_out, mpre, w_mix):
    t = x.shape[0]
    tm = min(TOKEN_TILE, t)
    row = lambda i: (i, 0)
    vec = _const_spec((1, D_MODEL))
    return pl.pallas_call(
        _pre_mix_kernel,
        grid=(t // tm,),
        in_specs=[
            pl.BlockSpec((tm, D_MODEL), row), vec, vec,
            _const_spec((D_MODEL, 2 * D_FF)), _const_spec((D_FF, D_MODEL)),
            vec, _const_spec((D_MODEL, 3 * SB_WIDTH + RWKV_PAD)),
        ],
        out_specs=[pl.BlockSpec((tm, D_MODEL), row), pl.BlockSpec((tm, 3 * SB_WIDTH), row),
                   pl.BlockSpec((tm, RWKV_PAD), row)],
        out_shape=[jax.ShapeDtypeStruct((t, D_MODEL), F32),
                   jax.ShapeDtypeStruct((t, 3 * SB_WIDTH), BF16),
                   jax.ShapeDtypeStruct((t, RWKV_PAD), F32)],
        scratch_shapes=[pltpu.VMEM((tm, D_FF), BF16)],
        compiler_params=_params("parallel"),
        name="ffn1_mix_in",
    )(x, fpre, fpost, w_in, w_out, mpre, w_mix)


def _interleave(*stages):
    gens = [g for g, _ in stages]
    counts = [n for _, n in stages]
    done = [0] * len(gens)
    live = [True] * len(gens)
    while any(live):
        which = min((n for n in range(len(gens)) if live[n]), key=lambda n: (done[n] + 1) / counts[n])
        try:
            next(gens[which])
            done[which] += 1
        except StopIteration:
            live[which] = False


def _post_mix_kernel(sb_ref, rw_ref, h_ref, kv_ref, wmix_ref, mixpost_ref, mempre_ref, wq_ref, wo_ref,
                     mempost_ref, fpre_ref, fpost_ref, win_ref, wout_ref, o_ref, act_ref, hs_ref, xn_ref):
    step = pl.program_id(0)

    @pl.when(step == 0)
    def _():
        hs_ref[...] = jnp.zeros_like(hs_ref)
        xn_ref[...] = jnp.zeros_like(xn_ref)

    result = {}

    def ffn():
        xn = xn_ref[...]
        for c in range(D_FF // FF_CHUNK):
            lo = c * FF_CHUNK
            gate = _dot(xn, win_ref[:, lo:lo + FF_CHUNK])
            up = _dot(xn, win_ref[:, D_FF + lo:D_FF + lo + FF_CHUNK])
            act_ref[:, lo:lo + FF_CHUNK] = (gate * jax.nn.sigmoid(gate) * up).astype(BF16)
            yield
        tm = hs_ref.shape[0]
        for lo in range(0, tm, tm // ROW_PARTS):
            rows = slice(lo, lo + tm // ROW_PARTS)
            y = _dot(act_ref[rows, :], wout_ref[...])
            o_ref[rows, :] = hs_ref[rows, :] + 0.5 * _rms(y, fpost_ref[...])
            yield

    def attn():
        mixed = _dot(sb_ref[...], wmix_ref[:SB_WIDTH, :]) + _dot(rw_ref[...], wmix_ref[SB_WIDTH:, :])
        yield
        h = h_ref[...] + _rms(mixed, mixpost_ref[...])
        q = _dot(_rms(h, mempre_ref[...]).astype(BF16), wq_ref[...])
        q = (q * (MEM_HEAD_DIM ** -0.5)).astype(BF16)
        yield
        outs = []
        for hd in range(MEM_HEADS):
            lo = hd * MEM_HEAD_DIM
            s = _dot_nt(q[:, lo:lo + MEM_HEAD_DIM], kv_ref[:, lo:lo + MEM_HEAD_DIM])
            e = jnp.exp(s - jnp.max(s, axis=-1, keepdims=True))
            pr = e / jnp.sum(e, axis=-1, keepdims=True)
            outs.append(_dot(pr.astype(BF16), kv_ref[:, MEM_WIDTH + lo:MEM_WIDTH + lo + MEM_HEAD_DIM]))
            yield
        o = jnp.concatenate(outs, axis=1).astype(BF16)
        h = h + _rms(_dot(o, wo_ref[...]), mempost_ref[...])
        yield
        result["h"] = h
        result["xn"] = _rms(h, fpre_ref[...]).astype(BF16)

    _interleave((ffn(), D_FF // FF_CHUNK + ROW_PARTS), (attn(), MEM_HEADS + 4))
    hs_ref[...] = result["h"]
    xn_ref[...] = result["xn"]


def _post_mix(sb_o, rw_o, h, kv, w_mix, mix_post, mem_pre, w_q, w_o, mem_post, fpre, fpost, w_in, w_out,
              batch, seq):
    mem_len = kv.shape[0] // batch
    kv = kv.reshape(batch, mem_len, 2 * MEM_WIDTH)
    t = batch * seq
    sb_o = sb_o.reshape(t, SB_WIDTH)
    rw_o = rw_o.reshape(t, RWKV_WIDTH)
    tm = min(TOKEN_TILE, seq)
    n = t // tm
    per_batch = seq // tm
    cur = lambda s: jnp.minimum(s, n - 1)
    tile = lambda width: pl.BlockSpec((tm, width), lambda s: (cur(s), 0))
    vec = _const_spec((1, D_MODEL))
    return pl.pallas_call(
        _post_mix_kernel,
        grid=(n + 1,),
        in_specs=[
            tile(SB_WIDTH), tile(RWKV_WIDTH), tile(D_MODEL),
            pl.BlockSpec((None, mem_len, 2 * MEM_WIDTH), lambda s: (cur(s) // per_batch, 0, 0)),
            _const_spec((MIX_WIDTH, D_MODEL)), vec, vec,
            _const_spec((D_MODEL, MEM_WIDTH)), _const_spec((MEM_WIDTH, D_MODEL)), vec,
            vec, vec, _const_spec((D_MODEL, 2 * D_FF)), _const_spec((D_FF, D_MODEL)),
        ],
        out_specs=pl.BlockSpec((tm, D_MODEL), lambda s: (jnp.maximum(s - 1, 0), 0)),
        out_shape=jax.ShapeDtypeStruct((t, D_MODEL), F32),
        scratch_shapes=[pltpu.VMEM((tm, D_FF), BF16), pltpu.VMEM((tm, D_MODEL), F32),
                        pltpu.VMEM((tm, D_MODEL), BF16)],
        compiler_params=_params("arbitrary"),
        name="mix_out_mem_ffn2",
    )(sb_o, rw_o, h, kv, w_mix, mix_post, mem_pre, w_q, w_o, mem_post, fpre, fpost, w_in, w_out)


def _mem_kv_kernel(m_ref, g_ref, w_ref, o_ref):
    o_ref[...] = _dot(_rms(m_ref[...], g_ref[...]).astype(BF16), w_ref[...]).astype(BF16)


def _mem_kv(mem, g, w):
    t = mem.shape[0]
    tm = min(TOKEN_TILE, t)
    row = lambda i: (i, 0)
    return pl.pallas_call(
        _mem_kv_kernel,
        grid=(t // tm,),
        in_specs=[pl.BlockSpec((tm, D_MODEL), row), _const_spec((1, D_MODEL)),
                  _const_spec((D_MODEL, 2 * MEM_WIDTH))],
        out_specs=pl.BlockSpec((tm, 2 * MEM_WIDTH), row),
        out_shape=jax.ShapeDtypeStruct((t, 2 * MEM_WIDTH), BF16),
        compiler_params=_params("parallel"),
        name="mem_kv",
    )(mem, g, w)


def _sb_kernel(q_ref, k_ref, v_ref, g_ref, cum_ref, o_ref, k2_ref, v2_ref, carry_ref, acc_ref, *, seq):
    i = pl.program_id(2)
    nk = seq // SB_TILE
    tq = SB_QROWS
    sub = tq // SB_TILE
    pairs = list(range(SB_PAIRS))
    lane = lax.broadcasted_iota(jnp.int32, (SB_TILE, LANES), 1)
    head0 = lane < HEAD_DIM

    def cols(p):
        return slice(p * LANES, (p + 1) * LANES)

    @pl.when(i == 0)
    def _():
        for p in pairs:
            for j in range(nk):
                kb = k_ref[j * SB_TILE:(j + 1) * SB_TILE, cols(p)] * jnp.asarray(HEAD_DIM ** -0.5, BF16)
                vb = v_ref[j * SB_TILE:(j + 1) * SB_TILE, cols(p)]
                zero = jnp.zeros_like(kb)
                k2_ref[p, j, :SB_TILE, :] = jnp.where(head0, kb, zero)
                k2_ref[p, j, SB_TILE:, :] = jnp.where(head0, zero, kb)
                v2_ref[p, j, :SB_TILE, :] = jnp.where(head0, vb, zero)
                v2_ref[p, j, SB_TILE:, :] = jnp.where(head0, zero, vb)

    cum = cum_ref[...]
    row = lax.broadcasted_iota(jnp.int32, (SB_TILE, 2 * SB_TILE), 0)
    col = lax.broadcasted_iota(jnp.int32, (SB_TILE, 2 * SB_TILE), 1) & (SB_TILE - 1)
    causal = col < row

    def scores(p, blocks, tiles, diagonal, valid=None):
        z = jnp.concatenate([_dot_nt(q_ref[rb * SB_TILE:(rb + 1) * SB_TILE, cols(p)], k2_ref[p, j])
                             for rb, j in zip(blocks, tiles)], axis=0)
        e = jnp.exp2(jnp.abs(z) * (-LOG2E))
        sp = jnp.maximum(z, 0.0) + jnp.log(1.0 + e)
        lsig = z - sp
        keep = None
        if diagonal:
            keep = jnp.concatenate([causal] * len(blocks), axis=0)
        if valid is not None:
            keep = valid if keep is None else jnp.logical_and(keep, valid)
        if keep is not None:
            sp = jnp.where(keep, sp, 0.0)
        return lsig, _dot(sp.astype(BF16), cum), keep

    def weights(p, blocks, tiles, lsig, r, keep):
        lo, hi = blocks[0] * SB_TILE, (blocks[-1] + 1) * SB_TILE
        carry = carry_ref[p, lo:hi, :]
        a = jnp.exp(lsig - r[:, :2 * SB_TILE] - carry)
        if keep is not None:
            a = jnp.where(keep, a, 0.0)
        a = a.astype(BF16)
        out = jnp.concatenate([_dot(a[n * SB_TILE:(n + 1) * SB_TILE, :], v2_ref[p, j])
                               for n, j in enumerate(tiles)], axis=0)
        carry_ref[p, lo:hi, :] = carry + r[:, 2 * SB_TILE:]
        acc_ref[p, lo:hi, :] += out

    def waves(plan):
        sc = [scores(*w) for w in plan]
        for (p, blocks, tiles, _), (lsig, r, keep) in zip(plan, sc):
            weights(p, blocks, tiles, lsig, r, keep)

    carry_ref[...] = jnp.zeros_like(carry_ref)
    acc_ref[...] = jnp.zeros_like(acc_ref)
    first = sub * i
    everyone = list(range(sub))

    @pl.when(i == 0)
    def _():
        waves([(p, everyone[d:], [rb - d for rb in everyone[d:]], d == 0)
               for d in range(sub) for p in pairs])

    @pl.when(i > 0)
    def _():
        waves([(p, everyone, [first + rb - d for rb in everyone], d == 0)
               for d in range(SB_FIXED_WAVES) for p in pairs])

    rb_of_row = lax.broadcasted_iota(jnp.int32, (tq, 1), 0) // SB_TILE

    def cond(state):
        d, smallest = state
        return jnp.logical_and(jnp.logical_and(i > 0, d < first + sub), smallest < SB_CARRY_STOP)

    def body(state):
        d, _ = state
        tiles = [jnp.maximum(first + rb - d, 0) for rb in everyone]
        valid = (first + rb_of_row - d) >= 0
        sc = [scores(p, everyone, tiles, False, valid) for p in pairs]
        for p, (lsig, r, keep) in zip(pairs, sc):
            weights(p, everyone, tiles, lsig, r, keep)
        return d + 1, jnp.min(carry_ref[...])

    lax.while_loop(cond, body, (jnp.int32(SB_FIXED_WAVES), jnp.min(carry_ref[...])))

    for p in pairs:
        acc = acc_ref[p]
        h0 = lax.broadcasted_iota(jnp.int32, acc.shape, 1) < HEAD_DIM
        sq = acc * acc
        ms0 = jnp.sum(jnp.where(h0, sq, 0.0), axis=-1, keepdims=True)
        ms1 = jnp.sum(jnp.where(h0, 0.0, sq), axis=-1, keepdims=True)
        ms = jnp.where(h0, ms0, ms1) * (1.0 / HEAD_DIM)
        o_ref[:, cols(p)] = (acc * lax.rsqrt(ms + NORM_EPS) * g_ref[:, cols(p)]).astype(BF16)


def _sb_cum_matrix():
    n = 2 * SB_TILE
    j = jnp.arange(n)[:, None]
    s = jnp.arange(n)[None, :]
    same = (j // SB_TILE) == (s // SB_TILE)
    tri = same & (j > s)
    return jnp.concatenate([tri, same], axis=1).astype(BF16)


def _sb_attention(qkv, out_g, batch, seq):
    assert seq % SB_QROWS == 0 and SB_FIXED_WAVES <= SB_QROWS // SB_TILE
    qkv = qkv.reshape(batch, seq, 3 * SB_WIDTH)
    width = SB_PAIRS * LANES
    groups = SB_WIDTH // width
    nk = seq // SB_TILE
    kern = functools.partial(_sb_kernel, seq=seq)
    return pl.pallas_call(
        kern,
        grid=(batch, groups, seq // SB_QROWS),
        in_specs=[
            pl.BlockSpec((None, SB_QROWS, width), lambda b, p, i: (b, i, p)),
            pl.BlockSpec((None, seq, width), lambda b, p, i: (b, 0, groups + p)),
            pl.BlockSpec((None, seq, width), lambda b, p, i: (b, 0, 2 * groups + p)),
            pl.BlockSpec((1, width), lambda b, p, i: (0, p)),
            _const_spec((2 * SB_TILE, 4 * SB_TILE)),
        ],
        out_specs=pl.BlockSpec((None, SB_QROWS, width), lambda b, p, i: (b, i, p)),
        out_shape=jax.ShapeDtypeStruct((batch, seq, SB_WIDTH), BF16),
        scratch_shapes=[pltpu.VMEM((SB_PAIRS, nk, 2 * SB_TILE, LANES), BF16),
                        pltpu.VMEM((SB_PAIRS, nk, 2 * SB_TILE, LANES), BF16),
                        pltpu.VMEM((SB_PAIRS, SB_QROWS, 2 * SB_TILE), F32),
                        pltpu.VMEM((SB_PAIRS, SB_QROWS, LANES), F32)],
        compiler_params=_params("parallel", "parallel", "arbitrary"),
        name="sb_attention",
    )(qkv, qkv, qkv, out_g, _sb_cum_matrix())


def _rwkv_kernel(rw_ref, prev_ref, mu_ref, w0_ref, w2_ref, a0_ref, a2_ref, g2_ref, kk_ref, ka_ref,
                 rk_ref, lg_ref, lb_ref, cum_ref, grp_ref, o_ref, state_s):
    c = pl.program_id(1)
    rows = RWKV_ROWS
    ch = RWKV_CHUNK
    width = RWKV_WIDTH
    quad = RWKV_QUAD
    heads = quad // HEAD_DIM

    @pl.when(c == 0)
    def _():
        state_s[...] = jnp.zeros_like(state_s)

    def gsum(x):
        x = x.astype(BF16)
        return jnp.concatenate([_dot(x[:, q0:q0 + quad], grp_ref[...]) for q0 in range(0, width, quad)],
                               axis=1)

    p = rw_ref[...]
    prev_last = jnp.where(c == 0, 0.0, prev_ref[SUBLANES - 1:, :])
    rowi = lax.broadcasted_iota(jnp.int32, p.shape, 0)
    shifted = jnp.where(rowi == 0, prev_last, pltpu.roll(p, 1, axis=0))
    p = p + (shifted - p) * mu_ref[...]
    r = p[:, :width]
    k = p[:, width:2 * width]
    v = p[:, 2 * width:3 * width]
    xwa = p[:, 3 * width:3 * width + LANES]
    xg = p[:, 3 * width + LANES:]
    lane = lax.broadcasted_iota(jnp.int32, xwa.shape, 1)
    xw = jnp.where(lane < DECAY_LORA, jnp.tanh(xwa), 0.0).astype(BF16)
    xa = jnp.where(lane < DECAY_LORA, 0.0, xwa).astype(BF16)
    wlin = w0_ref[...] + _dot(xw, w2_ref[...])
    wlog = -(jnp.maximum(-wlin, 0.0) + jnp.log(1.0 + jnp.exp(-jnp.abs(wlin)))) - 0.5
    lw = -jnp.exp(wlog)
    lr = jax.nn.sigmoid(a0_ref[...] + _dot(xa, a2_ref[...]))
    gate = _dot(jax.nn.sigmoid(xg).astype(BF16), g2_ref[...])
    kk = k * kk_ref[...]
    kk = kk * lax.rsqrt(jnp.maximum(gsum(kk * kk), 1e-24))
    k = k * (1.0 + (lr - 1.0) * ka_ref[...])
    a_vec = -kk
    b_vec = kk * lr

    cs = _dot(cum_ref[...], jnp.concatenate(_split2(lw), axis=1))
    g_in = cs[:, :width] + cs[:, width:]
    g_tot = jnp.concatenate([jnp.broadcast_to(g_in[c0 + ch - 1:c0 + ch, :], (ch, width))
                             for c0 in range(0, rows, ch)], axis=0)
    e_in = jnp.exp(g_in)
    e_inv = jnp.exp(-g_in)
    e_end = jnp.exp(g_tot - g_in)
    at_all = (a_vec * jnp.exp(g_in - lw)).astype(BF16)
    rt_all = (r * e_in).astype(BF16)
    bt_all = (b_vec * e_inv).astype(BF16)
    kt_all = (k * e_inv).astype(BF16)
    v_all = v.astype(BF16)
    bh_all = (b_vec * e_end).astype(BF16)
    kh_all = (k * e_end).astype(BF16)
    gam_all = jnp.exp(g_tot)

    t_i = lax.broadcasted_iota(jnp.int32, (ch, quad), 0)
    l_i = lax.broadcasted_iota(jnp.int32, (ch, quad), 1)
    i_i = l_i & (HEAD_DIM - 1)
    strict = i_i < t_i
    incl = i_i <= t_i
    eye = (i_i == t_i).astype(F32)
    in_head = [(l_i // HEAD_DIM) == h for h in range(heads)]
    rr = lax.broadcasted_iota(jnp.int32, (quad, quad), 0) // HEAD_DIM
    cc = lax.broadcasted_iota(jnp.int32, (quad, quad), 1) // HEAD_DIM
    same_head = rr == cc

    def bd(y):
        y = y.astype(BF16)
        zero = jnp.zeros_like(y)
        return jnp.concatenate([jnp.where(m, y, zero) for m in in_head], axis=0)

    def pp(x, y):
        return _dot(x.astype(BF16), bd(y))

    def unpack_diag(full):
        out = jnp.where(in_head[0], full[:ch], 0.0)
        for h in range(1, heads):
            out = out + jnp.where(in_head[h], full[h * ch:(h + 1) * ch], 0.0)
        return out

    def pp2(x, y0, y1):
        x = x.astype(BF16)
        y0 = y0.astype(BF16)
        y1 = y1.astype(BF16)
        half_lane = lax.broadcasted_iota(jnp.int32, (ch, LANES), 1) // HEAD_DIM
        in_half = [half_lane == h for h in range(LANES // HEAD_DIM)]
        outs = []
        for lo in range(0, quad, LANES):
            blocks = []
            for y in (y0, y1):
                part = y[:, lo:lo + LANES]
                zero = jnp.zeros_like(part)
                blocks.append(jnp.concatenate([jnp.where(m, part, zero) for m in in_half], axis=0))
            outs.append(_dot(x[:, lo:lo + LANES], jnp.concatenate(blocks, axis=1)))
        first = jnp.concatenate([o[:, :LANES] for o in outs], axis=1)
        second = jnp.concatenate([o[:, LANES:] for o in outs], axis=1)
        return jnp.concatenate([first, second], axis=1)

    def state_terms(x, v_, b, k_):
        left = jnp.concatenate([x.astype(BF16), jnp.concatenate([jnp.zeros_like(v_), v_], axis=1)], axis=0)
        return _dot_tn(left, jnp.concatenate([b, k_], axis=0))

    nq = width // quad
    combos = [(ci, qd) for ci in range(rows // ch) for qd in range(nq)]

    def cut(x, ci, qd):
        return x[ci * ch:(ci + 1) * ch, qd * quad:(qd + 1) * quad]

    def each(fn, *lists):
        return [fn(*args) for args in zip(*lists)]

    at = [cut(at_all, *cq) for cq in combos]
    rt = [cut(rt_all, *cq) for cq in combos]
    vv = [cut(v_all, *cq) for cq in combos]
    bh = [cut(bh_all, *cq) for cq in combos]
    kh = [cut(kh_all, *cq) for cq in combos]
    aa = [_dot_nt(jnp.concatenate([a, r_], axis=0),
                  jnp.concatenate([bd(cut(bt_all, *cq)), bd(cut(kt_all, *cq))], axis=0))
          for a, r_, cq in zip(at, rt, combos)]
    n_ab = [jnp.where(strict, x[:ch, :quad], 0.0) for x in aa]
    n_ak = [jnp.where(strict, x[:ch, quad:], 0.0) for x in aa]
    a_rb = [jnp.where(incl, x[ch:, :quad], 0.0) for x in aa]
    a_rk = [jnp.where(incl, x[ch:, quad:], 0.0) for x in aa]

    blk = 8
    diag = (t_i // blk) == (i_i // blk)
    nd = [jnp.where(diag, x, 0.0) for x in n_ab]
    n2 = each(pp, nd, nd)
    tm = [eye + x for x in nd]
    both = each(lambda t, n: _dot(jnp.concatenate([t, n], axis=0).astype(BF16), bd(n)), tm, n2)
    tm = each(lambda t, x: t + x[:ch], tm, both)
    tm = each(lambda t, x: t + pp(t, x[ch:]), tm, both)
    while blk < ch:
        off = ((t_i // (2 * blk)) == (i_i // (2 * blk))) & ((t_i // blk) != (i_i // blk))
        half = each(lambda t, n: pp(t, jnp.where(off, n, 0.0)), tm, n_ab)
        tm = each(lambda t, h: t + pp(h, t), tm, half)
        blk *= 2

    kv_both = each(lambda a, b, v_: _dot(jnp.concatenate([a, b], axis=0).astype(BF16), bd(v_)), n_ak, a_rk, vv)
    akv = [x[:ch] for x in kv_both]
    wu = each(pp2, tm, at, akv)
    wa = [x[:, :quad] for x in wu]
    uv = [x[:, quad:] for x in wu]
    ry = each(pp2, a_rb, wa, uv)
    rhat = each(lambda r_, x: r_.astype(F32) + x[:, :quad], rt, ry)
    yv = each(lambda x, y: x[:, quad:] + y[ch:], ry, kv_both)
    gs = each(state_terms, wu, vv, bh, kh)
    gc = [jnp.where(same_head, x[:quad], 0.0).astype(BF16) for x in gs]
    sv = [unpack_diag(x[quad:]) for x in gs]

    states = [state_s[qd] for qd in range(nq)]
    y_out = {}
    for n, (ci, qd) in enumerate(combos):
        s0 = states[qd]
        y_out[ci, qd] = _dot_nt(rhat[n].astype(BF16), bd(s0)) + yv[n]
        gam_row = gam_all[ci * ch:ci * ch + 1, qd * quad:(qd + 1) * quad]
        states[qd] = s0 * gam_row + _dot(s0.astype(BF16), gc[n]) + sv[n]
    for qd in range(nq):
        state_s[qd] = states[qd]
    y_rows = [jnp.concatenate([y_out[ci, qd] for qd in range(nq)], axis=1)
              for ci in range(rows // ch)]

    y = jnp.concatenate(y_rows, axis=0)
    mean = gsum(y) * (1.0 / HEAD_DIM)
    d = y - mean
    var = gsum(d * d) * (1.0 / HEAD_DIM)
    y = d * lax.rsqrt(var + LNX_EPS) * lg_ref[...] + lb_ref[...]
    bonus = gsum(r * k * rk_ref[...]) * v
    o_ref[...] = ((y + bonus) * gate).astype(BF16)


def _rwkv_cum_matrix():
    t = jnp.arange(RWKV_ROWS)[:, None]
    i = jnp.arange(RWKV_ROWS)[None, :]
    return (((t // RWKV_CHUNK) == (i // RWKV_CHUNK)) & (i <= t)).astype(BF16)


def _group_matrix():
    a = jnp.arange(RWKV_QUAD)
    return (a[:, None] // HEAD_DIM == a[None, :] // HEAD_DIM).astype(BF16)


def _rwkv(rw, batch, seq, mu, w0, w2, a0, a2, g2, k_k, k_a, r_k, lnx_g, lnx_b):
    rw = rw.reshape(batch, seq, RWKV_PAD)
    rows = RWKV_ROWS
    vec = lambda n: _const_spec((1, n))
    return pl.pallas_call(
        _rwkv_kernel,
        grid=(batch, seq // rows),
        in_specs=[
            pl.BlockSpec((None, rows, RWKV_PAD), lambda b, c: (b, c, 0)),
            pl.BlockSpec((None, SUBLANES, RWKV_PAD),
                         lambda b, c: (b, jnp.maximum(c * (rows // SUBLANES) - 1, 0), 0)),
            vec(RWKV_PAD), vec(RWKV_WIDTH), _const_spec((LANES, RWKV_WIDTH)), vec(RWKV_WIDTH),
            _const_spec((LANES, RWKV_WIDTH)), _const_spec((GATE_PAD, RWKV_WIDTH)),
            vec(RWKV_WIDTH), vec(RWKV_WIDTH), vec(RWKV_WIDTH), vec(RWKV_WIDTH), vec(RWKV_WIDTH),
            _const_spec((rows, rows)), _const_spec((RWKV_QUAD, RWKV_QUAD)),
        ],
        out_specs=pl.BlockSpec((None, rows, RWKV_WIDTH), lambda b, c: (b, c, 0)),
        out_shape=jax.ShapeDtypeStruct((batch, seq, RWKV_WIDTH), BF16),
        scratch_shapes=[pltpu.VMEM((RWKV_WIDTH // RWKV_QUAD, RWKV_CHUNK, RWKV_QUAD), F32)],
        compiler_params=_params("parallel", "arbitrary"),
        name="rwkv7",
    )(rw, rw, mu, w0, w2, a0, a2, g2, k_k, k_a, r_k, lnx_g, lnx_b,
      _rwkv_cum_matrix(), _group_matrix())


def _row(a):
    return a.reshape(1, -1).astype(F32)


def _pad_rows(w, n):
    return jnp.pad(w, ((0, n - w.shape[0]), (0, 0)))


def kernel(x, mem, ffn1_pre, ffn1_post, ffn1_w_in, ffn1_w_out, mix_pre, mix_post, mix_w_in, rwkv_mu, rwkv_w0, rwkv_w2, rwkv_a0, rwkv_a2, rwkv_g2, rwkv_k_k, rwkv_k_a, rwkv_r_k, rwkv_lnx_g, rwkv_lnx_b, sb_out_g, mix_w_out, mem_pre, mem_post, mem_kv_g, mem_w_q, mem_w_kv, mem_w_o, ffn2_pre, ffn2_post, ffn2_w_in, ffn2_w_out):
    batch, seq, _ = x.shape
    depth = ffn1_pre.shape[0]
    h = x.reshape(batch * seq, D_MODEL)
    mem2 = mem.reshape(-1, D_MODEL)
    pad_cols = RWKV_PAD - RWKV_IN
    for l in range(depth):
        w_mix_in = jnp.pad(mix_w_in[l].astype(BF16), ((0, 0), (0, pad_cols)))
        h, qkv, rw = _pre_mix(h, _row(ffn1_pre[l]), _row(ffn1_post[l]), ffn1_w_in[l].astype(BF16),
                              ffn1_w_out[l].astype(BF16), _row(mix_pre[l]), w_mix_in)
        sb_o = _sb_attention(qkv, _row(sb_out_g[l]), batch, seq)
        w2 = _pad_rows(rwkv_w2[l], LANES).astype(BF16)
        a2 = jnp.pad(rwkv_a2[l], ((DECAY_LORA, 0), (0, 0))).astype(BF16)
        g2 = _pad_rows(rwkv_g2[l], GATE_PAD).astype(BF16)
        mu = jnp.pad(rwkv_mu[l], (0, pad_cols))
        rw_o = _rwkv(rw, batch, seq, _row(mu), _row(rwkv_w0[l]), w2, _row(rwkv_a0[l]), a2, g2,
                     _row(rwkv_k_k[l]), _row(rwkv_k_a[l]), _row(rwkv_r_k[l]),
                     _row(rwkv_lnx_g[l]), _row(rwkv_lnx_b[l]))
        kv = _mem_kv(mem2, _row(mem_kv_g[l]), mem_w_kv[l].astype(BF16))
        h = _post_mix(sb_o, rw_o, h, kv, mix_w_out[l].astype(BF16), _row(mix_post[l]),
                      _row(mem_pre[l]), mem_w_q[l].astype(BF16), mem_w_o[l].astype(BF16),
                      _row(mem_post[l]), _row(ffn2_pre[l]), _row(ffn2_post[l]),
                      ffn2_w_in[l].astype(BF16), ffn2_w_out[l].astype(BF16), batch, seq)
    return h.reshape(batch, seq, D_MODEL)
```

```python
import functools

import jax
import jax.numpy as jnp
from jax import lax
from jax.experimental import pallas as pl
from jax.experimental.pallas import tpu as pltpu

D_MODEL = 1024
HEAD_DIM = 64
SB_HEADS = 8
RWKV_HEADS = 8
SB_WIDTH = SB_HEADS * HEAD_DIM
RWKV_WIDTH = RWKV_HEADS * HEAD_DIM
MIX_WIDTH = SB_WIDTH + RWKV_WIDTH
DECAY_LORA = 64
AAA_LORA = 64
GATE_LORA = 160
RWKV_IN = 3 * RWKV_WIDTH + DECAY_LORA + AAA_LORA + GATE_LORA
MEM_HEADS = 4
MEM_HEAD_DIM = 128
MEM_WIDTH = MEM_HEADS * MEM_HEAD_DIM
D_FF = 2816
NORM_EPS = 1e-6
LNX_EPS = 64e-5

LANES = 128
SUBLANES = 8
MXU_DIM = 256
V7X_VMEM_BYTES = 64 * 1024 * 1024
VMEM_LIMIT_BYTES = V7X_VMEM_BYTES - 8 * 1024 * 1024

RWKV_PAD = 15 * LANES
GATE_PAD = RWKV_PAD - (3 * RWKV_WIDTH + LANES)
SB_TILE = 128
SB_QROWS = 512
SB_CARRY_STOP = 112.0
SB_FIXED_WAVES = 3
SB_PAIRS = 4
LOG2E = 1.4426950408889634
RWKV_CHUNK = 64
RWKV_ROWS = 512
RWKV_QUAD = MXU_DIM
TOKEN_TILE = 512
FF_CHUNK = 256
ROW_PARTS = 2

BF16 = jnp.bfloat16
F32 = jnp.float32


def _const_spec(shape):
    nd = len(shape)
    return pl.BlockSpec(shape, lambda *_: (0,) * nd, pipeline_mode=pl.Buffered(1))


def _params(*sem):
    return pltpu.CompilerParams(dimension_semantics=sem, vmem_limit_bytes=VMEM_LIMIT_BYTES)


def _rms(x, g):
    ms = jnp.mean(x * x, axis=-1, keepdims=True)
    return x * lax.rsqrt(ms + NORM_EPS) * g


def _dot(a, b):
    return jnp.dot(a, b, preferred_element_type=F32)


def _dot_nt(a, b):
    return lax.dot_general(a, b, (((1,), (1,)), ((), ())), preferred_element_type=F32)


def _dot_tn(a, b):
    return lax.dot_general(a, b, (((0,), (0,)), ((), ())), preferred_element_type=F32)


def _split2(x):
    hi = x.astype(BF16)
    lo = (x - hi.astype(F32)).astype(BF16)
    return hi, lo


def _pre_mix_kernel(x_ref, fpre_ref, fpost_ref, win_ref, wout_ref, mpre_ref, wmix_ref,
                    h_ref, qkv_ref, rw_ref, act_ref):
    x = x_ref[...]
    xn = _rms(x, fpre_ref[...]).astype(BF16)
    for c in range(D_FF // FF_CHUNK):
        lo = c * FF_CHUNK
        gate = _dot(xn, win_ref[:, lo:lo + FF_CHUNK])
        up = _dot(xn, win_ref[:, D_FF + lo:D_FF + lo + FF_CHUNK])
        act_ref[:, lo:lo + FF_CHUNK] = (gate * jax.nn.sigmoid(gate) * up).astype(BF16)
    tm = x.shape[0]
    parts = [slice(lo, lo + tm // ROW_PARTS) for lo in range(0, tm, tm // ROW_PARTS)]
    hn = []
    for rows in parts:
        y = _dot(act_ref[rows, :], wout_ref[...])
        h = x[rows, :] + 0.5 * _rms(y, fpost_ref[...])
        h_ref[rows, :] = h
        hn.append(_rms(h, mpre_ref[...]).astype(BF16))
    for rows, part in zip(parts, hn):
        qkv_ref[rows, :] = _dot(part, wmix_ref[:, :3 * SB_WIDTH]).astype(BF16)
        rw_ref[rows, :] = _dot(part, wmix_ref[:, 3 * SB_WIDTH:])


def _pre_mix(x, fpre, fpost, w_in, w_out, mpre, w_mix):
    t = x.shape[0]
    tm = min(TOKEN_TILE, t)
    row = lambda i: (i, 0)
    vec = _const_spec((1, D_MODEL))
    return pl.pallas_call(
        _pre_mix_kernel,
        grid=(t // tm,),
        in_specs=[
            pl.BlockSpec((tm, D_MODEL), row), vec, vec,
            _const_spec((D_MODEL, 2 * D_FF)), _const_spec((D_FF, D_MODEL)),
            vec, _const_spec((D_MODEL, 3 * SB_WIDTH + RWKV_PAD)),
        ],
        out_specs=[pl.BlockSpec((tm, D_MODEL), row), pl.BlockSpec((tm, 3 * SB_WIDTH), row),
                   pl.BlockSpec((tm, RWKV_PAD), row)],
        out_shape=[jax.ShapeDtypeStruct((t, D_MODEL), F32),
                   jax.ShapeDtypeStruct((t, 3 * SB_WIDTH), BF16),
                   jax.ShapeDtypeStruct((t, RWKV_PAD), F32)],
        scratch_shapes=[pltpu.VMEM((tm, D_FF), BF16)],
        compiler_params=_params("parallel"),
        name="ffn1_mix_in",
    )(x, fpre, fpost, w_in, w_out, mpre, w_mix)


def _interleave(*stages):
    gens = [g for g, _ in stages]
    counts = [n for _, n in stages]
    done = [0] * len(gens)
    live = [True] * len(gens)
    while any(live):
        which = min((n for n in range(len(gens)) if live[n]), key=lambda n: (done[n] + 1) / counts[n])
        try:
            next(gens[which])
            done[which] += 1
        except StopIteration:
            live[which] = False


def _post_mix_kernel(sb_ref, rw_ref, h_ref, kv_ref, wmix_ref, mixpost_ref, mempre_ref, wq_ref, wo_ref,
                     mempost_ref, fpre_ref, fpost_ref, win_ref, wout_ref, o_ref, act_ref, hs_ref, xn_ref):
    step = pl.program_id(0)

    @pl.when(step == 0)
    def _():
        hs_ref[...] = jnp.zeros_like(hs_ref)
        xn_ref[...] = jnp.zeros_like(xn_ref)

    result = {}

    def ffn():
        xn = xn_ref[...]
        for c in range(D_FF // FF_CHUNK):
            lo = c * FF_CHUNK
            gate = _dot(xn, win_ref[:, lo:lo + FF_CHUNK])
            up = _dot(xn, win_ref[:, D_FF + lo:D_FF + lo + FF_CHUNK])
            act_ref[:, lo:lo + FF_CHUNK] = (gate * jax.nn.sigmoid(gate) * up).astype(BF16)
            yield
        tm = hs_ref.shape[0]
        for lo in range(0, tm, tm // ROW_PARTS):
            rows = slice(lo, lo + tm // ROW_PARTS)
            y = _dot(act_ref[rows, :], wout_ref[...])
            o_ref[rows, :] = hs_ref[rows, :] + 0.5 * _rms(y, fpost_ref[...])
            yield

    def attn():
        mixed = _dot(sb_ref[...], wmix_ref[:SB_WIDTH, :]) + _dot(rw_ref[...], wmix_ref[SB_WIDTH:, :])
        yield
        h = h_ref[...] + _rms(mixed, mixpost_ref[...])
        q = _dot(_rms(h, mempre_ref[...]).astype(BF16), wq_ref[...])
        q = (q * (MEM_HEAD_DIM ** -0.5)).astype(BF16)
        yield
        outs = []
        for hd in range(MEM_HEADS):
            lo = hd * MEM_HEAD_DIM
            s = _dot_nt(q[:, lo:lo + MEM_HEAD_DIM], kv_ref[:, lo:lo + MEM_HEAD_DIM])
            e = jnp.exp(s - jnp.max(s, axis=-1, keepdims=True))
            pr = e / jnp.sum(e, axis=-1, keepdims=True)
            outs.append(_dot(pr.astype(BF16), kv_ref[:, MEM_WIDTH + lo:MEM_WIDTH + lo + MEM_HEAD_DIM]))
            yield
        o = jnp.concatenate(outs, axis=1).astype(BF16)
        h = h + _rms(_dot(o, wo_ref[...]), mempost_ref[...])
        yield
        result["h"] = h
        result["xn"] = _rms(h, fpre_ref[...]).astype(BF16)

    _interleave((ffn(), D_FF // FF_CHUNK + ROW_PARTS), (attn(), MEM_HEADS + 4))
    hs_ref[...] = result["h"]
    xn_ref[...] = result["xn"]


def _post_mix(sb_o, rw_o, h, kv, w_mix, mix_post, mem_pre, w_q, w_o, mem_post, fpre, fpost, w_in, w_out,
              batch, seq):
    mem_len = kv.shape[0] // batch
    kv = kv.reshape(batch, mem_len, 2 * MEM_WIDTH)
    t = batch * seq
    sb_o = sb_o.reshape(t, SB_WIDTH)
    rw_o = rw_o.reshape(t, RWKV_WIDTH)
    tm = min(TOKEN_TILE, seq)
    n = t // tm
    per_batch = seq // tm
    cur = lambda s: jnp.minimum(s, n - 1)
    tile = lambda width: pl.BlockSpec((tm, width), lambda s: (cur(s), 0))
    vec = _const_spec((1, D_MODEL))
    return pl.pallas_call(
        _post_mix_kernel,
        grid=(n + 1,),
        in_specs=[
            tile(SB_WIDTH), tile(RWKV_WIDTH), tile(D_MODEL),
            pl.BlockSpec((None, mem_len, 2 * MEM_WIDTH), lambda s: (cur(s) // per_batch, 0, 0)),
            _const_spec((MIX_WIDTH, D_MODEL)), vec, vec,
            _const_spec((D_MODEL, MEM_WIDTH)), _const_spec((MEM_WIDTH, D_MODEL)), vec,
            vec, vec, _const_spec((D_MODEL, 2 * D_FF)), _const_spec((D_FF, D_MODEL)),
        ],
        out_specs=pl.BlockSpec((tm, D_MODEL), lambda s: (jnp.maximum(s - 1, 0), 0)),
        out_shape=jax.ShapeDtypeStruct((t, D_MODEL), F32),
        scratch_shapes=[pltpu.VMEM((tm, D_FF), BF16), pltpu.VMEM((tm, D_MODEL), F32),
                        pltpu.VMEM((tm, D_MODEL), BF16)],
        compiler_params=_params("arbitrary"),
        name="mix_out_mem_ffn2",
    )(sb_o, rw_o, h, kv, w_mix, mix_post, mem_pre, w_q, w_o, mem_post, fpre, fpost, w_in, w_out)


def _mem_kv_kernel(m_ref, g_ref, w_ref, o_ref):
    o_ref[...] = _dot(_rms(m_ref[...], g_ref[...]).astype(BF16), w_ref[...]).astype(BF16)


def _mem_kv(mem, g, w):
    t = mem.shape[0]
    tm = min(TOKEN_TILE, t)
    row = lambda i: (i, 0)
    return pl.pallas_call(
        _mem_kv_kernel,
        grid=(t // tm,),
        in_specs=[pl.BlockSpec((tm, D_MODEL), row), _const_spec((1, D_MODEL)),
                  _const_spec((D_MODEL, 2 * MEM_WIDTH))],
        out_specs=pl.BlockSpec((tm, 2 * MEM_WIDTH), row),
        out_shape=jax.ShapeDtypeStruct((t, 2 * MEM_WIDTH), BF16),
        compiler_params=_params("parallel"),
        name="mem_kv",
    )(mem, g, w)


def _sb_kernel(q_ref, k_ref, v_ref, g_ref, cum_ref, o_ref, k2_ref, v2_ref, carry_ref, acc_ref, *, seq):
    i = pl.program_id(2)
    nk = seq // SB_TILE
    tq = SB_QROWS
    sub = tq // SB_TILE
    pairs = list(range(SB_PAIRS))
    lane = lax.broadcasted_iota(jnp.int32, (SB_TILE, LANES), 1)
    head0 = lane < HEAD_DIM

    def cols(p):
        return slice(p * LANES, (p + 1) * LANES)

    @pl.when(i == 0)
    def _():
        for p in pairs:
            for j in range(nk):
                kb = k_ref[j * SB_TILE:(j + 1) * SB_TILE, cols(p)] * jnp.asarray(HEAD_DIM ** -0.5, BF16)
                vb = v_ref[j * SB_TILE:(j + 1) * SB_TILE, cols(p)]
                zero = jnp.zeros_like(kb)
                k2_ref[p, j, :SB_TILE, :] = jnp.where(head0, kb, zero)
                k2_ref[p, j, SB_TILE:, :] = jnp.where(head0, zero, kb)
                v2_ref[p, j, :SB_TILE, :] = jnp.where(head0, vb, zero)
                v2_ref[p, j, SB_TILE:, :] = jnp.where(head0, zero, vb)

    cum = cum_ref[...]
    row = lax.broadcasted_iota(jnp.int32, (SB_TILE, 2 * SB_TILE), 0)
    col = lax.broadcasted_iota(jnp.int32, (SB_TILE, 2 * SB_TILE), 1) & (SB_TILE - 1)
    causal = col < row

    def scores(p, blocks, tiles, diagonal, valid=None):
        z = jnp.concatenate([_dot_nt(q_ref[rb * SB_TILE:(rb + 1) * SB_TILE, cols(p)], k2_ref[p, j])
                             for rb, j in zip(blocks, tiles)], axis=0)
        e = jnp.exp2(jnp.abs(z) * (-LOG2E))
        sp = jnp.maximum(z, 0.0) + jnp.log(1.0 + e)
        lsig = z - sp
        keep = None
        if diagonal:
            keep = jnp.concatenate([causal] * len(blocks), axis=0)
        if valid is not None:
            keep = valid if keep is None else jnp.logical_and(keep, valid)
        if keep is not None:
            sp = jnp.where(keep, sp, 0.0)
        return lsig, _dot(sp.astype(BF16), cum), keep

    def weights(p, blocks, tiles, lsig, r, keep):
        lo, hi = blocks[0] * SB_TILE, (blocks[-1] + 1) * SB_TILE
        carry = carry_ref[p, lo:hi, :]
        a = jnp.exp(lsig - r[:, :2 * SB_TILE] - carry)
        if keep is not None:
            a = jnp.where(keep, a, 0.0)
        a = a.astype(BF16)
        out = jnp.concatenate([_dot(a[n * SB_TILE:(n + 1) * SB_TILE, :], v2_ref[p, j])
                               for n, j in enumerate(tiles)], axis=0)
        carry_ref[p, lo:hi, :] = carry + r[:, 2 * SB_TILE:]
        acc_ref[p, lo:hi, :] += out

    def waves(plan):
        sc = [scores(*w) for w in plan]
        for (p, blocks, tiles, _), (lsig, r, keep) in zip(plan, sc):
            weights(p, blocks, tiles, lsig, r, keep)

    carry_ref[...] = jnp.zeros_like(carry_ref)
    acc_ref[...] = jnp.zeros_like(acc_ref)
    first = sub * i
    everyone = list(range(sub))

    @pl.when(i == 0)
    def _():
        waves([(p, everyone[d:], [rb - d for rb in everyone[d:]], d == 0)
               for d in range(sub) for p in pairs])

    @pl.when(i > 0)
    def _():
        waves([(p, everyone, [first + rb - d for rb in everyone], d == 0)
               for d in range(SB_FIXED_WAVES) for p in pairs])

    rb_of_row = lax.broadcasted_iota(jnp.int32, (tq, 1), 0) // SB_TILE

    def cond(state):
        d, smallest = state
        return jnp.logical_and(jnp.logical_and(i > 0, d < first + sub), smallest < SB_CARRY_STOP)

    def body(state):
        d, _ = state
        tiles = [jnp.maximum(first + rb - d, 0) for rb in everyone]
        valid = (first + rb_of_row - d) >= 0
        sc = [scores(p, everyone, tiles, False, valid) for p in pairs]
        for p, (lsig, r, keep) in zip(pairs, sc):
            weights(p, everyone, tiles, lsig, r, keep)
        return d + 1, jnp.min(carry_ref[...])

    lax.while_loop(cond, body, (jnp.int32(SB_FIXED_WAVES), jnp.min(carry_ref[...])))

    for p in pairs:
        acc = acc_ref[p]
        h0 = lax.broadcasted_iota(jnp.int32, acc.shape, 1) < HEAD_DIM
        sq = acc * acc
        ms0 = jnp.sum(jnp.where(h0, sq, 0.0), axis=-1, keepdims=True)
        ms1 = jnp.sum(jnp.where(h0, 0.0, sq), axis=-1, keepdims=True)
        ms = jnp.where(h0, ms0, ms1) * (1.0 / HEAD_DIM)
        o_ref[:, cols(p)] = (acc * lax.rsqrt(ms + NORM_EPS) * g_ref[:, cols(p)]).astype(BF16)


def _sb_cum_matrix():
    n = 2 * SB_TILE
    j = jnp.arange(n)[:, None]
    s = jnp.arange(n)[None, :]
    same = (j // SB_TILE) == (s // SB_TILE)
    tri = same & (j > s)
    return jnp.concatenate([tri, same], axis=1).astype(BF16)


def _sb_attention(qkv, out_g, batch, seq):
    assert seq % SB_QROWS == 0 and SB_FIXED_WAVES <= SB_QROWS // SB_TILE
    qkv = qkv.reshape(batch, seq, 3 * SB_WIDTH)
    width = SB_PAIRS * LANES
    groups = SB_WIDTH // width
    nk = seq // SB_TILE
    kern = functools.partial(_sb_kernel, seq=seq)
    return pl.pallas_call(
        kern,
        grid=(batch, groups, seq // SB_QROWS),
        in_specs=[
            pl.BlockSpec((None, SB_QROWS, width), lambda b, p, i: (b, i, p)),
            pl.BlockSpec((None, seq, width), lambda b, p, i: (b, 0, groups + p)),
            pl.BlockSpec((None, seq, width), lambda b, p, i: (b, 0, 2 * groups + p)),
            pl.BlockSpec((1, width), lambda b, p, i: (0, p)),
            _const_spec((2 * SB_TILE, 4 * SB_TILE)),
        ],
        out_specs=pl.BlockSpec((None, SB_QROWS, width), lambda b, p, i: (b, i, p)),
        out_shape=jax.ShapeDtypeStruct((batch, seq, SB_WIDTH), BF16),
        scratch_shapes=[pltpu.VMEM((SB_PAIRS, nk, 2 * SB_TILE, LANES), BF16),
                        pltpu.VMEM((SB_PAIRS, nk, 2 * SB_TILE, LANES), BF16),
                        pltpu.VMEM((SB_PAIRS, SB_QROWS, 2 * SB_TILE), F32),
                        pltpu.VMEM((SB_PAIRS, SB_QROWS, LANES), F32)],
        compiler_params=_params("parallel", "parallel", "arbitrary"),
        name="sb_attention",
    )(qkv, qkv, qkv, out_g, _sb_cum_matrix())


def _rwkv_kernel(rw_ref, prev_ref, mu_ref, w0_ref, w2_ref, a0_ref, a2_ref, g2_ref, kk_ref, ka_ref,
                 rk_ref, lg_ref, lb_ref, cum_ref, grp_ref, o_ref, state_s):
    c = pl.program_id(1)
    rows = RWKV_ROWS
    ch = RWKV_CHUNK
    width = RWKV_WIDTH
    quad = RWKV_QUAD
    heads = quad // HEAD_DIM

    @pl.when(c == 0)
    def _():
        state_s[...] = jnp.zeros_like(state_s)

    def gsum(x):
        x = x.astype(BF16)
        return jnp.concatenate([_dot(x[:, q0:q0 + quad], grp_ref[...]) for q0 in range(0, width, quad)],
                               axis=1)

    p = rw_ref[...]
    prev_last = jnp.where(c == 0, 0.0, prev_ref[SUBLANES - 1:, :])
    rowi = lax.broadcasted_iota(jnp.int32, p.shape, 0)
    shifted = jnp.where(rowi == 0, prev_last, pltpu.roll(p, 1, axis=0))
    p = p + (shifted - p) * mu_ref[...]
    r = p[:, :width]
    k = p[:, width:2 * width]
    v = p[:, 2 * width:3 * width]
    xwa = p[:, 3 * width:3 * width + LANES]
    xg = p[:, 3 * width + LANES:]
    lane = lax.broadcasted_iota(jnp.int32, xwa.shape, 1)
    xw = jnp.where(lane < DECAY_LORA, jnp.tanh(xwa), 0.0).astype(BF16)
    xa = jnp.where(lane < DECAY_LORA, 0.0, xwa).astype(BF16)
    wlin = w0_ref[...] + _dot(xw, w2_ref[...])
    wlog = -(jnp.maximum(-wlin, 0.0) + jnp.log(1.0 + jnp.exp(-jnp.abs(wlin)))) - 0.5
    lw = -jnp.exp(wlog)
    lr = jax.nn.sigmoid(a0_ref[...] + _dot(xa, a2_ref[...]))
    gate = _dot(jax.nn.sigmoid(xg).astype(BF16), g2_ref[...])
    kk = k * kk_ref[...]
    kk = kk * lax.rsqrt(jnp.maximum(gsum(kk * kk), 1e-24))
    k = k * (1.0 + (lr - 1.0) * ka_ref[...])
    a_vec = -kk
    b_vec = kk * lr

    cs = _dot(cum_ref[...], jnp.concatenate(_split2(lw), axis=1))
    g_in = cs[:, :width] + cs[:, width:]
    gam_all = jnp.concatenate([jnp.broadcast_to(jnp.exp(g_in[c0 + ch - 1:c0 + ch, :]), (ch, width))
                               for c0 in range(0, rows, ch)], axis=0)
    e_in = jnp.exp(g_in)
    e_inv = jnp.exp(-g_in)
    e_end = gam_all * e_inv
    at_all = (a_vec * jnp.exp(g_in - lw)).astype(BF16)
    rt_all = (r * e_in).astype(BF16)
    bt_all = (b_vec * e_inv).astype(BF16)
    kt_all = (k * e_inv).astype(BF16)
    v_all = v.astype(BF16)
    bh_all = (b_vec * e_end).astype(BF16)
    kh_all = (k * e_end).astype(BF16)

    t_i = lax.broadcasted_iota(jnp.int32, (ch, quad), 0)
    l_i = lax.broadcasted_iota(jnp.int32, (ch, quad), 1)
    i_i = l_i & (HEAD_DIM - 1)
    strict = i_i < t_i
    incl = i_i <= t_i
    eye = (i_i == t_i).astype(F32)
    in_head = [(l_i // HEAD_DIM) == h for h in range(heads)]
    rr = lax.broadcasted_iota(jnp.int32, (quad, quad), 0) // HEAD_DIM
    cc = lax.broadcasted_iota(jnp.int32, (quad, quad), 1) // HEAD_DIM
    same_head = rr == cc

    def bd(y):
        y = y.astype(BF16)
        zero = jnp.zeros_like(y)
        return jnp.concatenate([jnp.where(m, y, zero) for m in in_head], axis=0)

    def pp(x, y):
        return _dot(x.astype(BF16), bd(y))

    def unpack_diag(full):
        out = jnp.where(in_head[0], full[:ch], 0.0)
        for h in range(1, heads):
            out = out + jnp.where(in_head[h], full[h * ch:(h + 1) * ch], 0.0)
        return out

    def pp2(x, y0, y1):
        x = x.astype(BF16)
        y0 = y0.astype(BF16)
        y1 = y1.astype(BF16)
        half_lane = lax.broadcasted_iota(jnp.int32, (ch, LANES), 1) // HEAD_DIM
        in_half = [half_lane == h for h in range(LANES // HEAD_DIM)]
        outs = []
        for lo in range(0, quad, LANES):
            blocks = []
            for y in (y0, y1):
                part = y[:, lo:lo + LANES]
                zero = jnp.zeros_like(part)
                blocks.append(jnp.concatenate([jnp.where(m, part, zero) for m in in_half], axis=0))
            outs.append(_dot(x[:, lo:lo + LANES], jnp.concatenate(blocks, axis=1)))
        first = jnp.concatenate([o[:, :LANES] for o in outs], axis=1)
        second = jnp.concatenate([o[:, LANES:] for o in outs], axis=1)
        return jnp.concatenate([first, second], axis=1)

    def state_terms(x, v_, b, k_):
        left = jnp.concatenate([x.astype(BF16), jnp.concatenate([jnp.zeros_like(v_), v_], axis=1)], axis=0)
        return _dot_tn(left, jnp.concatenate([b, k_], axis=0))

    nq = width // quad
    combos = [(ci, qd) for ci in range(rows // ch) for qd in range(nq)]

    def cut(x, ci, qd):
        return x[ci * ch:(ci + 1) * ch, qd * quad:(qd + 1) * quad]

    def each(fn, *lists):
        return [fn(*args) for args in zip(*lists)]

    at = [cut(at_all, *cq) for cq in combos]
    rt = [cut(rt_all, *cq) for cq in combos]
    vv = [cut(v_all, *cq) for cq in combos]
    bh = [cut(bh_all, *cq) for cq in combos]
    kh = [cut(kh_all, *cq) for cq in combos]
    aa = [_dot_nt(jnp.concatenate([a, r_], axis=0),
                  jnp.concatenate([bd(cut(bt_all, *cq)), bd(cut(kt_all, *cq))], axis=0))
          for a, r_, cq in zip(at, rt, combos)]
    n_ab = [jnp.where(strict, x[:ch, :quad], 0.0) for x in aa]
    n_ak = [jnp.where(strict, x[:ch, quad:], 0.0) for x in aa]
    a_rb = [jnp.where(incl, x[ch:, :quad], 0.0) for x in aa]
    a_rk = [jnp.where(incl, x[ch:, quad:], 0.0) for x in aa]

    blk = 8
    diag = (t_i // blk) == (i_i // blk)
    nd = [jnp.where(diag, x, 0.0) for x in n_ab]
    n2 = each(pp, nd, nd)
    tm = [eye + x for x in nd]
    both = each(lambda t, n: _dot(jnp.concatenate([t, n], axis=0).astype(BF16), bd(n)), tm, n2)
    tm = each(lambda t, x: t + x[:ch], tm, both)
    tm = each(lambda t, x: t + pp(t, x[ch:]), tm, both)
    while blk < ch:
        off = ((t_i // (2 * blk)) == (i_i // (2 * blk))) & ((t_i // blk) != (i_i // blk))
        half = each(lambda t, n: pp(t, jnp.where(off, n, 0.0)), tm, n_ab)
        tm = each(lambda t, h: t + pp(h, t), tm, half)
        blk *= 2

    kv_both = each(lambda a, b, v_: _dot(jnp.concatenate([a, b], axis=0).astype(BF16), bd(v_)), n_ak, a_rk, vv)
    akv = [x[:ch] for x in kv_both]
    wu = each(pp2, tm, at, akv)
    wa = [x[:, :quad] for x in wu]
    uv = [x[:, quad:] for x in wu]
    ry = each(pp2, a_rb, wa, uv)
    rhat = each(lambda r_, x: r_.astype(F32) + x[:, :quad], rt, ry)
    yv = each(lambda x, y: x[:, quad:] + y[ch:], ry, kv_both)
    gs = each(state_terms, wu, vv, bh, kh)
    gc = [jnp.where(same_head, x[:quad], 0.0).astype(BF16) for x in gs]
    sv = [unpack_diag(x[quad:]) for x in gs]

    states = [state_s[qd] for qd in range(nq)]
    y_out = {}
    for n, (ci, qd) in enumerate(combos):
        s0 = states[qd]
        y_out[ci, qd] = _dot_nt(rhat[n].astype(BF16), bd(s0)) + yv[n]
        gam_row = gam_all[ci * ch:ci * ch + 1, qd * quad:(qd + 1) * quad]
        states[qd] = s0 * gam_row + _dot(s0.astype(BF16), gc[n]) + sv[n]
    for qd in range(nq):
        state_s[qd] = states[qd]
    y_rows = [jnp.concatenate([y_out[ci, qd] for qd in range(nq)], axis=1)
              for ci in range(rows // ch)]

    y = jnp.concatenate(y_rows, axis=0)
    mean = gsum(y) * (1.0 / HEAD_DIM)
    d = y - mean
    var = gsum(d * d) * (1.0 / HEAD_DIM)
    y = d * lax.rsqrt(var + LNX_EPS) * lg_ref[...] + lb_ref[...]
    bonus = gsum(r * k * rk_ref[...]) * v
    o_ref[...] = ((y + bonus) * gate).astype(BF16)


def _rwkv_cum_matrix():
    t = jnp.arange(RWKV_ROWS)[:, None]
    i = jnp.arange(RWKV_ROWS)[None, :]
    return (((t // RWKV_CHUNK) == (i // RWKV_CHUNK)) & (i <= t)).astype(BF16)


def _group_matrix():
    a = jnp.arange(RWKV_QUAD)
    return (a[:, None] // HEAD_DIM == a[None, :] // HEAD_DIM).astype(BF16)


def _rwkv(rw, batch, seq, mu, w0, w2, a0, a2, g2, k_k, k_a, r_k, lnx_g, lnx_b):
    rw = rw.reshape(batch, seq, RWKV_PAD)
    rows = RWKV_ROWS
    vec = lambda n: _const_spec((1, n))
    return pl.pallas_call(
        _rwkv_kernel,
        grid=(batch, seq // rows),
        in_specs=[
            pl.BlockSpec((None, rows, RWKV_PAD), lambda b, c: (b, c, 0)),
            pl.BlockSpec((None, SUBLANES, RWKV_PAD),
                         lambda b, c: (b, jnp.maximum(c * (rows // SUBLANES) - 1, 0), 0)),
            vec(RWKV_PAD), vec(RWKV_WIDTH), _const_spec((LANES, RWKV_WIDTH)), vec(RWKV_WIDTH),
            _const_spec((LANES, RWKV_WIDTH)), _const_spec((GATE_PAD, RWKV_WIDTH)),
            vec(RWKV_WIDTH), vec(RWKV_WIDTH), vec(RWKV_WIDTH), vec(RWKV_WIDTH), vec(RWKV_WIDTH),
            _const_spec((rows, rows)), _const_spec((RWKV_QUAD, RWKV_QUAD)),
        ],
        out_specs=pl.BlockSpec((None, rows, RWKV_WIDTH), lambda b, c: (b, c, 0)),
        out_shape=jax.ShapeDtypeStruct((batch, seq, RWKV_WIDTH), BF16),
        scratch_shapes=[pltpu.VMEM((RWKV_WIDTH // RWKV_QUAD, RWKV_CHUNK, RWKV_QUAD), F32)],
        compiler_params=_params("parallel", "arbitrary"),
        name="rwkv7",
    )(rw, rw, mu, w0, w2, a0, a2, g2, k_k, k_a, r_k, lnx_g, lnx_b,
      _rwkv_cum_matrix(), _group_matrix())


def _row(a):
    return a.reshape(1, -1).astype(F32)


def _pad_rows(w, n):
    return jnp.pad(w, ((0, n - w.shape[0]), (0, 0)))


def kernel(x, mem, ffn1_pre, ffn1_post, ffn1_w_in, ffn1_w_out, mix_pre, mix_post, mix_w_in, rwkv_mu, rwkv_w0, rwkv_w2, rwkv_a0, rwkv_a2, rwkv_g2, rwkv_k_k, rwkv_k_a, rwkv_r_k, rwkv_lnx_g, rwkv_lnx_b, sb_out_g, mix_w_out, mem_pre, mem_post, mem_kv_g, mem_w_q, mem_w_kv, mem_w_o, ffn2_pre, ffn2_post, ffn2_w_in, ffn2_w_out):
    batch, seq, _ = x.shape
    depth = ffn1_pre.shape[0]
    h = x.reshape(batch * seq, D_MODEL)
    mem2 = mem.reshape(-1, D_MODEL)
    pad_cols = RWKV_PAD - RWKV_IN
    for l in range(depth):
        w_mix_in = jnp.pad(mix_w_in[l].astype(BF16), ((0, 0), (0, pad_cols)))
        h, qkv, rw = _pre_mix(h, _row(ffn1_pre[l]), _row(ffn1_post[l]), ffn1_w_in[l].astype(BF16),
                              ffn1_w_out[l].astype(BF16), _row(mix_pre[l]), w_mix_in)
        sb_o = _sb_attention(qkv, _row(sb_out_g[l]), batch, seq)
        w2 = _pad_rows(rwkv_w2[l], LANES).astype(BF16)
        a2 = jnp.pad(rwkv_a2[l], ((DECAY_LORA, 0), (0, 0))).astype(BF16)
        g2 = _pad_rows(rwkv_g2[l], GATE_PAD).astype(BF16)
        mu = jnp.pad(rwkv_mu[l], (0, pad_cols))
        rw_o = _rwkv(rw, batch, seq, _row(mu), _row(rwkv_w0[l]), w2, _row(rwkv_a0[l]), a2, g2,
                     _row(rwkv_k_k[l]), _row(rwkv_k_a[l]), _row(rwkv_r_k[l]),
                     _row(rwkv_lnx_g[l]), _row(rwkv_lnx_b[l]))
        kv = _mem_kv(mem2, _row(mem_kv_g[l]), mem_w_kv[l].astype(BF16))
        h = _post_mix(sb_o, rw_o, h, kv, mix_w_out[l].astype(BF16), _row(mix_post[l]),
                      _row(mem_pre[l]), mem_w_q[l].astype(BF16), mem_w_o[l].astype(BF16),
                      _row(mem_post[l]), _row(ffn2_pre[l]), _row(ffn2_post[l]),
                      ffn2_w_in[l].astype(BF16), ffn2_w_out[l].astype(BF16), batch, seq)
    return h.reshape(batch, seq, D_MODEL)
```

```python
import functools

import jax
import jax.numpy as jnp
from jax import lax
from jax.experimental import pallas as pl
from jax.experimental.pallas import tpu as pltpu

D_MODEL = 1024
HEAD_DIM = 64
SB_HEADS = 8
RWKV_HEADS = 8
SB_WIDTH = SB_HEADS * HEAD_DIM
RWKV_WIDTH = RWKV_HEADS * HEAD_DIM
MIX_WIDTH = SB_WIDTH + RWKV_WIDTH
DECAY_LORA = 64
AAA_LORA = 64
GATE_LORA = 160
RWKV_IN = 3 * RWKV_WIDTH + DECAY_LORA + AAA_LORA + GATE_LORA
MEM_HEADS = 4
MEM_HEAD_DIM = 128
MEM_WIDTH = MEM_HEADS * MEM_HEAD_DIM
D_FF = 2816
NORM_EPS = 1e-6
LNX_EPS = 64e-5

LANES = 128
SUBLANES = 8
MXU_DIM = 256
V7X_VMEM_BYTES = 64 * 1024 * 1024
VMEM_LIMIT_BYTES = V7X_VMEM_BYTES - 8 * 1024 * 1024

RWKV_PAD = 15 * LANES
GATE_PAD = RWKV_PAD - (3 * RWKV_WIDTH + LANES)
SB_TILE = 128
SB_QROWS = 512
SB_CARRY_STOP = 112.0
SB_FIXED_WAVES = 3
SB_PAIRS = 4
LOG2E = 1.4426950408889634
RWKV_CHUNK = 64
RWKV_ROWS = 512
RWKV_QUAD = MXU_DIM
TOKEN_TILE = 512
FF_CHUNK = 256
ROW_PARTS = 2

BF16 = jnp.bfloat16
F32 = jnp.float32


def _const_spec(shape):
    nd = len(shape)
    return pl.BlockSpec(shape, lambda *_: (0,) * nd, pipeline_mode=pl.Buffered(1))


def _params(*sem):
    return pltpu.CompilerParams(dimension_semantics=sem, vmem_limit_bytes=VMEM_LIMIT_BYTES)


def _rms(x, g):
    ms = jnp.mean(x * x, axis=-1, keepdims=True)
    return x * lax.rsqrt(ms + NORM_EPS) * g


def _dot(a, b):
    return jnp.dot(a, b, preferred_element_type=F32)


def _dot_nt(a, b):
    return lax.dot_general(a, b, (((1,), (1,)), ((), ())), preferred_element_type=F32)


def _dot_tn(a, b):
    return lax.dot_general(a, b, (((0,), (0,)), ((), ())), preferred_element_type=F32)


def _split2(x):
    hi = x.astype(BF16)
    lo = (x - hi.astype(F32)).astype(BF16)
    return hi, lo


def _pre_mix_kernel(x_ref, fpre_ref, fpost_ref, win_ref, wout_ref, mpre_ref, wmix_ref,
                    h_ref, qkv_ref, rw_ref, act_ref):
    x = x_ref[...]
    xn = _rms(x, fpre_ref[...]).astype(BF16)
    for c in range(D_FF // FF_CHUNK):
        lo = c * FF_CHUNK
        gate = _dot(xn, win_ref[:, lo:lo + FF_CHUNK])
        up = _dot(xn, win_ref[:, D_FF + lo:D_FF + lo + FF_CHUNK])
        act_ref[:, lo:lo + FF_CHUNK] = (gate * jax.nn.sigmoid(gate) * up).astype(BF16)
    tm = x.shape[0]
    parts = [slice(lo, lo + tm // ROW_PARTS) for lo in range(0, tm, tm // ROW_PARTS)]
    hn = []
    for rows in parts:
        y = _dot(act_ref[rows, :], wout_ref[...])
        h = x[rows, :] + 0.5 * _rms(y, fpost_ref[...])
        h_ref[rows, :] = h
        hn.append(_rms(h, mpre_ref[...]).astype(BF16))
    for rows, part in zip(parts, hn):
        qkv_ref[rows, :] = _dot(part, wmix_ref[:, :3 * SB_WIDTH]).astype(BF16)
        rw_ref[rows, :] = _dot(part, wmix_ref[:, 3 * SB_WIDTH:])


def _pre_mix(x, fpre, fpost, w_in, w_out, mpre, w_mix):
    t = x.shape[0]
    tm = min(TOKEN_TILE, t)
    row = lambda i: (i, 0)
    vec = _const_spec((1, D_MODEL))
    return pl.pallas_call(
        _pre_mix_kernel,
        grid=(t // tm,),
        in_specs=[
            pl.BlockSpec((tm, D_MODEL), row), vec, vec,
            _const_spec((D_MODEL, 2 * D_FF)), _const_spec((D_FF, D_MODEL)),
            vec, _const_spec((D_MODEL, 3 * SB_WIDTH + RWKV_PAD)),
        ],
        out_specs=[pl.BlockSpec((tm, D_MODEL), row), pl.BlockSpec((tm, 3 * SB_WIDTH), row),
                   pl.BlockSpec((tm, RWKV_PAD), row)],
        out_shape=[jax.ShapeDtypeStruct((t, D_MODEL), F32),
                   jax.ShapeDtypeStruct((t, 3 * SB_WIDTH), BF16),
                   jax.ShapeDtypeStruct((t, RWKV_PAD), F32)],
        scratch_shapes=[pltpu.VMEM((tm, D_FF), BF16)],
        compiler_params=_params("parallel"),
        name="ffn1_mix_in",
    )(x, fpre, fpost, w_in, w_out, mpre, w_mix)


def _interleave(*stages):
    gens = [g for g, _ in stages]
    counts = [n for _, n in stages]
    done = [0] * len(gens)
    live = [True] * len(gens)
    while any(live):
        which = min((n for n in range(len(gens)) if live[n]), key=lambda n: (done[n] + 1) / counts[n])
        try:
            next(gens[which])
            done[which] += 1
        except StopIteration:
            live[which] = False


def _post_mix_kernel(sb_ref, rw_ref, h_ref, kv_ref, wmix_ref, mixpost_ref, mempre_ref, wq_ref, wo_ref,
                     mempost_ref, fpre_ref, fpost_ref, win_ref, wout_ref, o_ref, act_ref, hs_ref, xn_ref):
    step = pl.program_id(0)

    @pl.when(step == 0)
    def _():
        hs_ref[...] = jnp.zeros_like(hs_ref)
        xn_ref[...] = jnp.zeros_like(xn_ref)

    result = {}

    def ffn():
        xn = xn_ref[...]
        for c in range(D_FF // FF_CHUNK):
            lo = c * FF_CHUNK
            gate = _dot(xn, win_ref[:, lo:lo + FF_CHUNK])
            up = _dot(xn, win_ref[:, D_FF + lo:D_FF + lo + FF_CHUNK])
            act_ref[:, lo:lo + FF_CHUNK] = (gate * jax.nn.sigmoid(gate) * up).astype(BF16)
            yield
        tm = hs_ref.shape[0]
        for lo in range(0, tm, tm // ROW_PARTS):
            rows = slice(lo, lo + tm // ROW_PARTS)
            y = _dot(act_ref[rows, :], wout_ref[...])
            o_ref[rows, :] = hs_ref[rows, :] + 0.5 * _rms(y, fpost_ref[...])
            yield

    def attn():
        mixed = _dot(sb_ref[...], wmix_ref[:SB_WIDTH, :]) + _dot(rw_ref[...], wmix_ref[SB_WIDTH:, :])
        yield
        h = h_ref[...] + _rms(mixed, mixpost_ref[...])
        q = _dot(_rms(h, mempre_ref[...]).astype(BF16), wq_ref[...])
        q = (q * (MEM_HEAD_DIM ** -0.5)).astype(BF16)
        yield
        outs = []
        for hd in range(MEM_HEADS):
            lo = hd * MEM_HEAD_DIM
            s = _dot_nt(q[:, lo:lo + MEM_HEAD_DIM], kv_ref[:, lo:lo + MEM_HEAD_DIM])
            e = jnp.exp(s - jnp.max(s, axis=-1, keepdims=True))
            pr = e / jnp.sum(e, axis=-1, keepdims=True)
            outs.append(_dot(pr.astype(BF16), kv_ref[:, MEM_WIDTH + lo:MEM_WIDTH + lo + MEM_HEAD_DIM]))
            yield
        o = jnp.concatenate(outs, axis=1).astype(BF16)
        h = h + _rms(_dot(o, wo_ref[...]), mempost_ref[...])
        yield
        result["h"] = h
        result["xn"] = _rms(h, fpre_ref[...]).astype(BF16)

    _interleave((ffn(), D_FF // FF_CHUNK + ROW_PARTS), (attn(), MEM_HEADS + 4))
    hs_ref[...] = result["h"]
    xn_ref[...] = result["xn"]


def _post_mix(sb_o, rw_o, h, kv, w_mix, mix_post, mem_pre, w_q, w_o, mem_post, fpre, fpost, w_in, w_out,
              batch, seq):
    mem_len = kv.shape[0] // batch
    kv = kv.reshape(batch, mem_len, 2 * MEM_WIDTH)
    t = batch * seq
    sb_o = sb_o.reshape(t, SB_WIDTH)
    rw_o = rw_o.reshape(t, RWKV_WIDTH)
    tm = min(TOKEN_TILE, seq)
    n = t // tm
    per_batch = seq // tm
    cur = lambda s: jnp.minimum(s, n - 1)
    tile = lambda width: pl.BlockSpec((tm, width), lambda s: (cur(s), 0))
    vec = _const_spec((1, D_MODEL))
    return pl.pallas_call(
        _post_mix_kernel,
        grid=(n + 1,),
        in_specs=[
            tile(SB_WIDTH), tile(RWKV_WIDTH), tile(D_MODEL),
            pl.BlockSpec((None, mem_len, 2 * MEM_WIDTH), lambda s: (cur(s) // per_batch, 0, 0)),
            _const_spec((MIX_WIDTH, D_MODEL)), vec, vec,
            _const_spec((D_MODEL, MEM_WIDTH)), _const_spec((MEM_WIDTH, D_MODEL)), vec,
            vec, vec, _const_spec((D_MODEL, 2 * D_FF)), _const_spec((D_FF, D_MODEL)),
        ],
        out_specs=pl.BlockSpec((tm, D_MODEL), lambda s: (jnp.maximum(s - 1, 0), 0)),
        out_shape=jax.ShapeDtypeStruct((t, D_MODEL), F32),
        scratch_shapes=[pltpu.VMEM((tm, D_FF), BF16), pltpu.VMEM((tm, D_MODEL), F32),
                        pltpu.VMEM((tm, D_MODEL), BF16)],
        compiler_params=_params("arbitrary"),
        name="mix_out_mem_ffn2",
    )(sb_o, rw_o, h, kv, w_mix, mix_post, mem_pre, w_q, w_o, mem_post, fpre, fpost, w_in, w_out)


def _mem_kv_kernel(m_ref, g_ref, w_ref, o_ref):
    o_ref[...] = _dot(_rms(m_ref[...], g_ref[...]).astype(BF16), w_ref[...]).astype(BF16)


def _mem_kv(mem, g, w):
    t = mem.shape[0]
    tm = min(TOKEN_TILE, t)
    row = lambda i: (i, 0)
    return pl.pallas_call(
        _mem_kv_kernel,
        grid=(t // tm,),
        in_specs=[pl.BlockSpec((tm, D_MODEL), row), _const_spec((1, D_MODEL)),
                  _const_spec((D_MODEL, 2 * MEM_WIDTH))],
        out_specs=pl.BlockSpec((tm, 2 * MEM_WIDTH), row),
        out_shape=jax.ShapeDtypeStruct((t, 2 * MEM_WIDTH), BF16),
        compiler_params=_params("parallel"),
        name="mem_kv",
    )(mem, g, w)


def _sb_kernel(q_ref, k_ref, v_ref, g_ref, cum_ref, o_ref, k2_ref, v2_ref, carry_ref, acc_ref, *, seq):
    i = pl.program_id(2)
    nk = seq // SB_TILE
    tq = SB_QROWS
    sub = tq // SB_TILE
    pairs = list(range(SB_PAIRS))
    lane = lax.broadcasted_iota(jnp.int32, (SB_TILE, LANES), 1)
    head0 = lane < HEAD_DIM

    def cols(p):
        return slice(p * LANES, (p + 1) * LANES)

    @pl.when(i == 0)
    def _():
        for p in pairs:
            for j in range(nk):
                kb = k_ref[j * SB_TILE:(j + 1) * SB_TILE, cols(p)] * jnp.asarray(HEAD_DIM ** -0.5, BF16)
                vb = v_ref[j * SB_TILE:(j + 1) * SB_TILE, cols(p)]
                zero = jnp.zeros_like(kb)
                k2_ref[p, j, :SB_TILE, :] = jnp.where(head0, kb, zero)
                k2_ref[p, j, SB_TILE:, :] = jnp.where(head0, zero, kb)
                v2_ref[p, j, :SB_TILE, :] = jnp.where(head0, vb, zero)
                v2_ref[p, j, SB_TILE:, :] = jnp.where(head0, zero, vb)

    cum = cum_ref[...]
    row = lax.broadcasted_iota(jnp.int32, (SB_TILE, 2 * SB_TILE), 0)
    col = lax.broadcasted_iota(jnp.int32, (SB_TILE, 2 * SB_TILE), 1) & (SB_TILE - 1)
    causal = col < row

    def scores(p, blocks, tiles, diagonal, valid=None):
        z = jnp.concatenate([_dot_nt(q_ref[rb * SB_TILE:(rb + 1) * SB_TILE, cols(p)], k2_ref[p, j])
                             for rb, j in zip(blocks, tiles)], axis=0)
        e = jnp.exp2(jnp.abs(z) * (-LOG2E))
        sp = jnp.maximum(z, 0.0) + jnp.log(1.0 + e)
        lsig = z - sp
        keep = None
        if diagonal:
            keep = jnp.concatenate([causal] * len(blocks), axis=0)
        if valid is not None:
            keep = valid if keep is None else jnp.logical_and(keep, valid)
        if keep is not None:
            sp = jnp.where(keep, sp, 0.0)
        return lsig, _dot(sp.astype(BF16), cum), keep

    def weights(p, blocks, tiles, lsig, r, keep):
        lo, hi = blocks[0] * SB_TILE, (blocks[-1] + 1) * SB_TILE
        carry = carry_ref[p, lo:hi, :]
        a = jnp.exp(lsig - r[:, :2 * SB_TILE] - carry)
        if keep is not None:
            a = jnp.where(keep, a, 0.0)
        a = a.astype(BF16)
        out = jnp.concatenate([_dot(a[n * SB_TILE:(n + 1) * SB_TILE, :], v2_ref[p, j])
                               for n, j in enumerate(tiles)], axis=0)
        carry_ref[p, lo:hi, :] = carry + r[:, 2 * SB_TILE:]
        acc_ref[p, lo:hi, :] += out

    def waves(plan):
        sc = [scores(*w) for w in plan]
        for (p, blocks, tiles, _), (lsig, r, keep) in zip(plan, sc):
            weights(p, blocks, tiles, lsig, r, keep)

    carry_ref[...] = jnp.zeros_like(carry_ref)
    acc_ref[...] = jnp.zeros_like(acc_ref)
    first = sub * i
    everyone = list(range(sub))

    @pl.when(i == 0)
    def _():
        waves([(p, everyone[d:], [rb - d for rb in everyone[d:]], d == 0)
               for d in range(sub) for p in pairs])

    @pl.when(i > 0)
    def _():
        waves([(p, everyone, [first + rb - d for rb in everyone], d == 0)
               for d in range(SB_FIXED_WAVES) for p in pairs])

    rb_of_row = lax.broadcasted_iota(jnp.int32, (tq, 1), 0) // SB_TILE

    def cond(state):
        d, smallest = state
        return jnp.logical_and(jnp.logical_and(i > 0, d < first + sub), smallest < SB_CARRY_STOP)

    def body(state):
        d, _ = state
        tiles = [jnp.maximum(first + rb - d, 0) for rb in everyone]
        valid = (first + rb_of_row - d) >= 0
        sc = [scores(p, everyone, tiles, False, valid) for p in pairs]
        for p, (lsig, r, keep) in zip(pairs, sc):
            weights(p, everyone, tiles, lsig, r, keep)
        return d + 1, jnp.min(carry_ref[...])

    lax.while_loop(cond, body, (jnp.int32(SB_FIXED_WAVES), jnp.min(carry_ref[...])))

    for p in pairs:
        acc = acc_ref[p]
        h0 = lax.broadcasted_iota(jnp.int32, acc.shape, 1) < HEAD_DIM
        sq = acc * acc
        ms0 = jnp.sum(jnp.where(h0, sq, 0.0), axis=-1, keepdims=True)
        ms1 = jnp.sum(jnp.where(h0, 0.0, sq), axis=-1, keepdims=True)
        ms = jnp.where(h0, ms0, ms1) * (1.0 / HEAD_DIM)
        o_ref[:, cols(p)] = (acc * lax.rsqrt(ms + NORM_EPS) * g_ref[:, cols(p)]).astype(BF16)


def _sb_cum_matrix():
    n = 2 * SB_TILE
    j = jnp.arange(n)[:, None]
    s = jnp.arange(n)[None, :]
    same = (j // SB_TILE) == (s // SB_TILE)
    tri = same & (j > s)
    return jnp.concatenate([tri, same], axis=1).astype(BF16)


def _sb_attention(qkv, out_g, batch, seq):
    assert seq % SB_QROWS == 0 and SB_FIXED_WAVES <= SB_QROWS // SB_TILE
    qkv = qkv.reshape(batch, seq, 3 * SB_WIDTH)
    width = SB_PAIRS * LANES
    groups = SB_WIDTH // width
    nk = seq // SB_TILE
    kern = functools.partial(_sb_kernel, seq=seq)
    return pl.pallas_call(
        kern,
        grid=(batch, groups, seq // SB_QROWS),
        in_specs=[
            pl.BlockSpec((None, SB_QROWS, width), lambda b, p, i: (b, i, p)),
            pl.BlockSpec((None, seq, width), lambda b, p, i: (b, 0, groups + p)),
            pl.BlockSpec((None, seq, width), lambda b, p, i: (b, 0, 2 * groups + p)),
            pl.BlockSpec((1, width), lambda b, p, i: (0, p)),
            _const_spec((2 * SB_TILE, 4 * SB_TILE)),
        ],
        out_specs=pl.BlockSpec((None, SB_QROWS, width), lambda b, p, i: (b, i, p)),
        out_shape=jax.ShapeDtypeStruct((batch, seq, SB_WIDTH), BF16),
        scratch_shapes=[pltpu.VMEM((SB_PAIRS, nk, 2 * SB_TILE, LANES), BF16),
                        pltpu.VMEM((SB_PAIRS, nk, 2 * SB_TILE, LANES), BF16),
                        pltpu.VMEM((SB_PAIRS, SB_QROWS, 2 * SB_TILE), F32),
                        pltpu.VMEM((SB_PAIRS, SB_QROWS, LANES), F32)],
        compiler_params=_params("parallel", "parallel", "arbitrary"),
        name="sb_attention",
    )(qkv, qkv, qkv, out_g, _sb_cum_matrix())


def _rwkv_kernel(rw_ref, prev_ref, mu_ref, w0_ref, w2_ref, a0_ref, a2_ref, g2_ref, kk_ref, ka_ref,
                 rk_ref, lg_ref, lb_ref, cum_ref, grp_ref, o_ref, state_s):
    c = pl.program_id(1)
    rows = RWKV_ROWS
    ch = RWKV_CHUNK
    width = RWKV_WIDTH
    quad = RWKV_QUAD
    heads = quad // HEAD_DIM

    @pl.when(c == 0)
    def _():
        state_s[...] = jnp.zeros_like(state_s)

    def gsum(x):
        x = x.astype(BF16)
        return jnp.concatenate([_dot(x[:, q0:q0 + quad], grp_ref[...]) for q0 in range(0, width, quad)],
                               axis=1)

    p = rw_ref[...]
    prev_last = jnp.where(c == 0, 0.0, prev_ref[SUBLANES - 1:, :])
    rowi = lax.broadcasted_iota(jnp.int32, p.shape, 0)
    shifted = jnp.where(rowi == 0, prev_last, pltpu.roll(p, 1, axis=0))
    p = p + (shifted - p) * mu_ref[...]
    r = p[:, :width]
    k = p[:, width:2 * width]
    v = p[:, 2 * width:3 * width]
    xwa = p[:, 3 * width:3 * width + LANES]
    xg = p[:, 3 * width + LANES:]
    lane = lax.broadcasted_iota(jnp.int32, xwa.shape, 1)
    xw = jnp.where(lane < DECAY_LORA, jnp.tanh(xwa), 0.0).astype(BF16)
    xa = jnp.where(lane < DECAY_LORA, 0.0, xwa).astype(BF16)
    wlin = w0_ref[...] + _dot(xw, w2_ref[...])
    wlog = -(jnp.maximum(-wlin, 0.0) + jnp.log(1.0 + jnp.exp(-jnp.abs(wlin)))) - 0.5
    lw = -jnp.exp(wlog)
    lr = jax.nn.sigmoid(a0_ref[...] + _dot(xa, a2_ref[...]))
    gate = _dot(jax.nn.sigmoid(xg).astype(BF16), g2_ref[...])
    kk = k * kk_ref[...]
    kk = kk * lax.rsqrt(jnp.maximum(gsum(kk * kk), 1e-24))
    k = k * (1.0 + (lr - 1.0) * ka_ref[...])
    a_vec = -kk
    b_vec = kk * lr

    cs = _dot(cum_ref[...], jnp.concatenate(_split2(lw), axis=1))
    g_in = cs[:, :width] + cs[:, width:]
    gam_all = jnp.concatenate([jnp.broadcast_to(jnp.exp(g_in[c0 + ch - 1:c0 + ch, :]), (ch, width))
                               for c0 in range(0, rows, ch)], axis=0)
    e_in = jnp.exp(g_in)
    e_inv = jnp.exp(-g_in)
    e_end = gam_all * e_inv
    at_all = (a_vec * jnp.exp(g_in - lw)).astype(BF16)
    rt_all = (r * e_in).astype(BF16)
    bt_all = (b_vec * e_inv).astype(BF16)
    kt_all = (k * e_inv).astype(BF16)
    v_all = v.astype(BF16)
    bh_all = (b_vec * e_end).astype(BF16)
    kh_all = (k * e_end).astype(BF16)

    t_i = lax.broadcasted_iota(jnp.int32, (ch, quad), 0)
    l_i = lax.broadcasted_iota(jnp.int32, (ch, quad), 1)
    i_i = l_i & (HEAD_DIM - 1)
    strict = i_i < t_i
    incl = i_i <= t_i
    eye = (i_i == t_i).astype(F32)
    in_head = [(l_i // HEAD_DIM) == h for h in range(heads)]
    rr = lax.broadcasted_iota(jnp.int32, (quad, quad), 0) // HEAD_DIM
    cc = lax.broadcasted_iota(jnp.int32, (quad, quad), 1) // HEAD_DIM
    same_head = rr == cc

    def bd(y):
        y = y.astype(BF16)
        zero = jnp.zeros_like(y)
        return jnp.concatenate([jnp.where(m, y, zero) for m in in_head], axis=0)

    def pp(x, y):
        return _dot(x.astype(BF16), bd(y))

    def unpack_diag(full):
        out = jnp.where(in_head[0], full[:ch], 0.0)
        for h in range(1, heads):
            out = out + jnp.where(in_head[h], full[h * ch:(h + 1) * ch], 0.0)
        return out

    def pp2(x, y0, y1):
        x = x.astype(BF16)
        y0 = y0.astype(BF16)
        y1 = y1.astype(BF16)
        half_lane = lax.broadcasted_iota(jnp.int32, (ch, LANES), 1) // HEAD_DIM
        in_half = [half_lane == h for h in range(LANES // HEAD_DIM)]
        outs = []
        for lo in range(0, quad, LANES):
            blocks = []
            for y in (y0, y1):
                part = y[:, lo:lo + LANES]
                zero = jnp.zeros_like(part)
                blocks.append(jnp.concatenate([jnp.where(m, part, zero) for m in in_half], axis=0))
            outs.append(_dot(x[:, lo:lo + LANES], jnp.concatenate(blocks, axis=1)))
        first = jnp.concatenate([o[:, :LANES] for o in outs], axis=1)
        second = jnp.concatenate([o[:, LANES:] for o in outs], axis=1)
        return jnp.concatenate([first, second], axis=1)

    def state_terms(x, v_, b, k_):
        left = jnp.concatenate([x.astype(BF16), jnp.concatenate([jnp.zeros_like(v_), v_], axis=1)], axis=0)
        return _dot_tn(left, jnp.concatenate([b, k_], axis=0))

    nq = width // quad
    combos = [(ci, qd) for ci in range(rows // ch) for qd in range(nq)]

    def cut(x, ci, qd):
        return x[ci * ch:(ci + 1) * ch, qd * quad:(qd + 1) * quad]

    def each(fn, *lists):
        return [fn(*args) for args in zip(*lists)]

    def ahead(group, out):
        at = [cut(at_all, *cq) for cq in group]
        rt = [cut(rt_all, *cq) for cq in group]
        vv = [cut(v_all, *cq) for cq in group]
        bh = [cut(bh_all, *cq) for cq in group]
        kh = [cut(kh_all, *cq) for cq in group]
        aa = [_dot_nt(jnp.concatenate([a, r_], axis=0),
                      jnp.concatenate([bd(cut(bt_all, *cq)), bd(cut(kt_all, *cq))], axis=0))
              for a, r_, cq in zip(at, rt, group)]
        n_ab = [jnp.where(strict, x[:ch, :quad], 0.0) for x in aa]
        n_ak = [jnp.where(strict, x[:ch, quad:], 0.0) for x in aa]
        a_rb = [jnp.where(incl, x[ch:, :quad], 0.0) for x in aa]
        a_rk = [jnp.where(incl, x[ch:, quad:], 0.0) for x in aa]
        yield
        blk = 8
        diag = (t_i // blk) == (i_i // blk)
        nd = [jnp.where(diag, x, 0.0) for x in n_ab]
        n2 = each(pp, nd, nd)
        yield
        tm = [eye + x for x in nd]
        both = each(lambda t, n: _dot(jnp.concatenate([t, n], axis=0).astype(BF16), bd(n)), tm, n2)
        tm = each(lambda t, x: t + x[:ch], tm, both)
        yield
        tm = each(lambda t, x: t + pp(t, x[ch:]), tm, both)
        yield
        while blk < ch:
            off = ((t_i // (2 * blk)) == (i_i // (2 * blk))) & ((t_i // blk) != (i_i // blk))
            half = each(lambda t, n: pp(t, jnp.where(off, n, 0.0)), tm, n_ab)
            yield
            tm = each(lambda t, h: t + pp(h, t), tm, half)
            yield
            blk *= 2
        kv_both = each(lambda a, b, v_: _dot(jnp.concatenate([a, b], axis=0).astype(BF16), bd(v_)),
                       n_ak, a_rk, vv)
        akv = [x[:ch] for x in kv_both]
        yield
        wu = each(pp2, tm, at, akv)
        wa = [x[:, :quad] for x in wu]
        uv = [x[:, quad:] for x in wu]
        yield
        ry = each(pp2, a_rb, wa, uv)
        out["rhat"] = each(lambda r_, x: r_.astype(F32) + x[:, :quad], rt, ry)
        out["yv"] = each(lambda x, y: x[:, quad:] + y[ch:], ry, kv_both)
        yield
        gs = each(state_terms, wu, vv, bh, kh)
        out["gc"] = [jnp.where(same_head, x[:quad], 0.0).astype(BF16) for x in gs]
        out["sv"] = [unpack_diag(x[quad:]) for x in gs]

    states = [state_s[qd] for qd in range(nq)]
    y_out = {}

    def chain(group, res):
        for n, (ci, qd) in enumerate(group):
            s0 = states[qd]
            y_out[ci, qd] = _dot_nt(res["rhat"][n].astype(BF16), bd(s0)) + res["yv"][n]
            gam_row = gam_all[ci * ch:ci * ch + 1, qd * quad:(qd + 1) * quad]
            states[qd] = s0 * gam_row + _dot(s0.astype(BF16), res["gc"][n]) + res["sv"][n]
            if qd == nq - 1:
                yield

    half_n = len(combos) // 2
    first_half, second_half = combos[:half_n], combos[half_n:]
    res_a, res_b = {}, {}
    for _ in ahead(first_half, res_a):
        pass
    _interleave((ahead(second_half, res_b), 13), (chain(first_half, res_a), 13))
    for _ in chain(second_half, res_b):
        pass
    for qd in range(nq):
        state_s[qd] = states[qd]
    y_rows = [jnp.concatenate([y_out[ci, qd] for qd in range(nq)], axis=1)
              for ci in range(rows // ch)]

    y = jnp.concatenate(y_rows, axis=0)
    mean = gsum(y) * (1.0 / HEAD_DIM)
    d = y - mean
    var = gsum(d * d) * (1.0 / HEAD_DIM)
    y = d * lax.rsqrt(var + LNX_EPS) * lg_ref[...] + lb_ref[...]
    bonus = gsum(r * k * rk_ref[...]) * v
    o_ref[...] = ((y + bonus) * gate).astype(BF16)


def _rwkv_cum_matrix():
    t = jnp.arange(RWKV_ROWS)[:, None]
    i = jnp.arange(RWKV_ROWS)[None, :]
    return (((t // RWKV_CHUNK) == (i // RWKV_CHUNK)) & (i <= t)).astype(BF16)


def _group_matrix():
    a = jnp.arange(RWKV_QUAD)
    return (a[:, None] // HEAD_DIM == a[None, :] // HEAD_DIM).astype(BF16)


def _rwkv(rw, batch, seq, mu, w0, w2, a0, a2, g2, k_k, k_a, r_k, lnx_g, lnx_b):
    rw = rw.reshape(batch, seq, RWKV_PAD)
    rows = RWKV_ROWS
    vec = lambda n: _const_spec((1, n))
    return pl.pallas_call(
        _rwkv_kernel,
        grid=(batch, seq // rows),
        in_specs=[
            pl.BlockSpec((None, rows, RWKV_PAD), lambda b, c: (b, c, 0)),
            pl.BlockSpec((None, SUBLANES, RWKV_PAD),
                         lambda b, c: (b, jnp.maximum(c * (rows // SUBLANES) - 1, 0), 0)),
            vec(RWKV_PAD), vec(RWKV_WIDTH), _const_spec((LANES, RWKV_WIDTH)), vec(RWKV_WIDTH),
            _const_spec((LANES, RWKV_WIDTH)), _const_spec((GATE_PAD, RWKV_WIDTH)),
            vec(RWKV_WIDTH), vec(RWKV_WIDTH), vec(RWKV_WIDTH), vec(RWKV_WIDTH), vec(RWKV_WIDTH),
            _const_spec((rows, rows)), _const_spec((RWKV_QUAD, RWKV_QUAD)),
        ],
        out_specs=pl.BlockSpec((None, rows, RWKV_WIDTH), lambda b, c: (b, c, 0)),
        out_shape=jax.ShapeDtypeStruct((batch, seq, RWKV_WIDTH), BF16),
        scratch_shapes=[pltpu.VMEM((RWKV_WIDTH // RWKV_QUAD, RWKV_CHUNK, RWKV_QUAD), F32)],
        compiler_params=_params("parallel", "arbitrary"),
        name="rwkv7",
    )(rw, rw, mu, w0, w2, a0, a2, g2, k_k, k_a, r_k, lnx_g, lnx_b,
      _rwkv_cum_matrix(), _group_matrix())


def _row(a):
    return a.reshape(1, -1).astype(F32)


def _pad_rows(w, n):
    return jnp.pad(w, ((0, n - w.shape[0]), (0, 0)))


def kernel(x, mem, ffn1_pre, ffn1_post, ffn1_w_in, ffn1_w_out, mix_pre, mix_post, mix_w_in, rwkv_mu, rwkv_w0, rwkv_w2, rwkv_a0, rwkv_a2, rwkv_g2, rwkv_k_k, rwkv_k_a, rwkv_r_k, rwkv_lnx_g, rwkv_lnx_b, sb_out_g, mix_w_out, mem_pre, mem_post, mem_kv_g, mem_w_q, mem_w_kv, mem_w_o, ffn2_pre, ffn2_post, ffn2_w_in, ffn2_w_out):
    batch, seq, _ = x.shape
    depth = ffn1_pre.shape[0]
    h = x.reshape(batch * seq, D_MODEL)
    mem2 = mem.reshape(-1, D_MODEL)
    pad_cols = RWKV_PAD - RWKV_IN
    for l in range(depth):
        w_mix_in = jnp.pad(mix_w_in[l].astype(BF16), ((0, 0), (0, pad_cols)))
        h, qkv, rw = _pre_mix(h, _row(ffn1_pre[l]), _row(ffn1_post[l]), ffn1_w_in[l].astype(BF16),
                              ffn1_w_out[l].astype(BF16), _row(mix_pre[l]), w_mix_in)
        sb_o = _sb_attention(qkv, _row(sb_out_g[l]), batch, seq)
        w2 = _pad_rows(rwkv_w2[l], LANES).astype(BF16)
        a2 = jnp.pad(rwkv_a2[l], ((DECAY_LORA, 0), (0, 0))).astype(BF16)
        g2 = _pad_rows(rwkv_g2[l], GATE_PAD).astype(BF16)
        mu = jnp.pad(rwkv_mu[l], (0, pad_cols))
        rw_o = _rwkv(rw, batch, seq, _row(mu), _row(rwkv_w0[l]), w2, _row(rwkv_a0[l]), a2, g2,
                     _row(rwkv_k_k[l]), _row(rwkv_k_a[l]), _row(rwkv_r_k[l]),
                     _row(rwkv_lnx_g[l]), _row(rwkv_lnx_b[l]))
        kv = _mem_kv(mem2, _row(mem_kv_g[l]), mem_w_kv[l].astype(BF16))
        h = _post_mix(sb_o, rw_o, h, kv, mix_w_out[l].astype(BF16), _row(mix_post[l]),
                      _row(mem_pre[l]), mem_w_q[l].astype(BF16), mem_w_o[l].astype(BF16),
                      _row(mem_post[l]), _row(ffn2_pre[l]), _row(ffn2_post[l]),
                      ffn2_w_in[l].astype(BF16), ffn2_w_out[l].astype(BF16), batch, seq)
    return h.reshape(batch, seq, D_MODEL)
```
